```python
import math
import jax, jax.numpy as jnp
from jax import lax
import numpy as np

D_MODEL = 1024
BATCH = 16
SEQ = 2048
DEPTH = 2
DEC_BATCH = 32
DEC_SEQ = 32
PAST_LEN = 4096

CHUNK = 64
EPS = 1e-6
DH = 64
H_A = D_MODEL // (2 * DH)
A_W = H_A * DH
H_I = 8
D_I = 64
K_TOP = 256
QBLK = 64
N_BUCKETS = 32
T5_MAX_DIST = 1024
DK_B = 128
DV_B = 128
H_B = D_MODEL // (2 * DV_B)
B_KW = H_B * DK_B
B_VW = H_B * DV_B
H_C = D_MODEL // DH
C_W = H_C * DH
C_BACK = 8
C_WIN = C_BACK * CHUNK
REL_CLIP = 128
D_FF = ((-(-8 * D_MODEL // 3) + 255) // 256) * 256
N_EVEN = (DEPTH + 1) // 2
N_ODD = DEPTH // 2
EVEN_SIZES = (A_W, A_W, A_W, H_I * D_I, D_I, H_I, B_KW, B_KW, B_VW, B_VW)
EVEN_IN = sum(EVEN_SIZES)

kernel_name = "hybrid_streaming_dsa_hgrn2_chunkband"

F32 = jnp.float32


def _split(z, sizes):
    out, off = [], 0
    for s in sizes:
        out.append(z[..., off:off + s])
        off += s
    return out


def _rmsnorm(x, g):
    xf = x.astype(F32)
    y = xf * lax.rsqrt(jnp.mean(xf * xf, axis=-1, keepdims=True) + EPS)
    return (y * g.astype(F32)).astype(x.dtype)


def _adaln(c, w_ada, b_ada):
    mod = jnp.einsum('bd,de->be', jax.nn.silu(c), w_ada) + b_ada
    return jnp.split(mod, 6, axis=-1)


def _modulate(x, g, shift, scale):
    return _rmsnorm(x, g) * (1 + scale[:, None]) + shift[:, None]


def _swiglu(h, wg, wu, wd):
    a = jnp.einsum('btd,df->btf', h, wg)
    b = jnp.einsum('btd,df->btf', h, wu)
    return jnp.einsum('btf,fd->btd', jax.nn.silu(a) * b, wd)


def _t5_bucket(rel):
    nb = N_BUCKETS // 2
    max_exact = nb // 2
    base = jnp.where(rel > 0, nb, 0)
    n = jnp.abs(rel)
    large = max_exact + (jnp.log(jnp.maximum(n, 1).astype(F32) / max_exact)
                         / math.log(T5_MAX_DIST / max_exact) * (nb - max_exact)).astype(jnp.int32)
    large = jnp.minimum(large, nb - 1)
    return base + jnp.where(n < max_exact, n, large)


def _dsa_attend(q, qi, wi, k, v, ki, q_pos, k_pos, t5_table, k_sel):
    B, T = q.shape[:2]
    blk = min(QBLK, T)
    nb = T // blk

    def blocks(a):
        return jnp.moveaxis(a.reshape(B, nb, blk, *a.shape[2:]), 1, 0)

    k_chunk = k_pos // CHUNK

    def one_block(xs):
        qb, qib, wb, pb = xs
        s = jnp.einsum('bthd,bsd->bths', qib, ki).astype(F32) * (D_I ** -0.5)
        score = jnp.einsum('bth,bths->bts', wb.astype(F32) * (H_I ** -0.5), jax.nn.relu(s))
        q_chunk = pb // CHUNK
        adm = k_chunk[None, :] <= q_chunk[:, None]
        score = jnp.where(adm[None], score, -jnp.inf)
        _, idx = lax.top_k(score, k_sel)
        ks = jax.vmap(lambda kk, ii: kk[ii])(k, idx)
        vs = jax.vmap(lambda vv, ii: vv[ii])(v, idx)
        sel_pos = k_pos[idx]
        valid = (sel_pos // CHUNK) <= q_chunk[None, :, None]
        bias = t5_table[_t5_bucket(sel_pos - pb[None, :, None])]
        logits = (jnp.einsum('bthd,btkhd->bhtk', qb, ks).astype(F32) * (DH ** -0.5)
                  + jnp.moveaxis(bias, -1, 1).astype(F32))
        logits = jnp.where(valid[:, None], logits, -jnp.inf)
        p = jax.nn.softmax(logits, axis=-1)
        return jnp.einsum('bhtk,btkhd->bthd', p.astype(vs.dtype), vs)

    out = lax.map(one_block, (blocks(q), blocks(qi), blocks(wi), q_pos.reshape(nb, blk)))
    return jnp.moveaxis(out, 0, 1).reshape(B, T, H_A, DH)


def _hgrn2_chunk(S0, xs):
    q, k, v, g = xs
    T = q.shape[1]
    G = jnp.cumsum(g, axis=1)
    causal = jnp.tril(jnp.ones((T, T), dtype=bool))[None, :, :, None, None]
    decay = jnp.exp(jnp.where(causal, G[:, :, None] - G[:, None, :], -jnp.inf))
    att = jnp.einsum('bthk,btshk,bshk->bhts', q, decay, k)
    o = (jnp.einsum('bhts,bshv->bthv', att, v)
         + jnp.einsum('bthk,bhkv->bthv', q * jnp.exp(G), S0))
    G_end = G[:, -1]
    S = (jnp.exp(G_end)[..., None] * S0
         + jnp.einsum('bshk,bshv->bhkv', k * jnp.exp(G_end[:, None] - G), v))
    return S, o


def _hgrn2_scan(S0, q, k, v, g):
    B, T = q.shape[:2]
    blk = min(CHUNK, T)
    n = T // blk

    def to_chunks(a):
        return jnp.moveaxis(a.reshape(B, n, blk, *a.shape[2:]), 1, 0)

    S, o = lax.scan(_hgrn2_chunk, S0, (to_chunks(q), to_chunks(k), to_chunks(v), to_chunks(g)))
    return S, jnp.moveaxis(o, 0, 1).reshape(B, T, H_B, DV_B)


def _even_mixer(h, k_past, v_past, ki_past, S0, q_pos, k_pos, k_sel,
                w_in, w_out, t5_table, lb, g_norm):
    B, T = h.shape[:2]
    z = jnp.einsum('btd,de->bte', h, w_in)
    qa, ka, va, qi, ki, wi, qb, fb, ib, gb = _split(z, EVEN_SIZES)
    qa = qa.reshape(B, T, H_A, DH)
    ka = ka.reshape(B, T, H_A, DH)
    va = va.reshape(B, T, H_A, DH)
    qi = qi.reshape(B, T, H_I, D_I)
    if k_past is None:
        k_all, v_all, ki_all = ka, va, ki
    else:
        k_all = jnp.concatenate([k_past.astype(ka.dtype), ka], axis=1)
        v_all = jnp.concatenate([v_past.astype(va.dtype), va], axis=1)
        ki_all = jnp.concatenate([ki_past.astype(ki.dtype), ki], axis=1)
    oa = _dsa_attend(qa, qi, wi, k_all, v_all, ki_all, q_pos, k_pos, t5_table, k_sel)
    qh = jax.nn.silu(qb.astype(F32)).reshape(B, T, H_B, DK_B)
    f = lb + (1.0 - lb) * jax.nn.sigmoid(fb.astype(F32))
    kh = (1.0 - f).reshape(B, T, H_B, DK_B)
    gh = jnp.log(f).reshape(B, T, H_B, DK_B)
    vh = ib.astype(F32).reshape(B, T, H_B, DV_B)
    S, ob = _hgrn2_scan(S0.astype(F32), qh, kh, vh, gh)
    ob = ob * lax.rsqrt(jnp.mean(ob * ob, axis=-1, keepdims=True) + EPS)
    ob = ob.reshape(B, T, B_VW) * g_norm.astype(F32) * jax.nn.silu(gb.astype(F32))
    mix = jnp.concatenate([oa.reshape(B, T, A_W), ob.astype(oa.dtype)], axis=-1)
    return jnp.einsum('bte,ed->btd', mix, w_out), ka, va, ki, S


def _band_attend(q, k, v, q_pos, k_pos, rel_table):
    rel = jnp.clip(q_pos[:, None] - k_pos[None, :], -REL_CLIP, REL_CLIP) + REL_CLIP
    bias = jnp.moveaxis(rel_table[rel], -1, 0).astype(F32)
    logits = jnp.einsum('bthd,bshd->bhts', q, k).astype(F32) * (DH ** -0.5) + bias[None]
    logits = jnp.where((k_pos >= 0)[None, None, None, :], logits, -jnp.inf)
    p = jax.nn.softmax(logits, axis=-1)
    return jnp.einsum('bhts,bshd->bthd', p.astype(v.dtype), v)


def _chunk_band_prompt(q, k, v, rel_table):
    B, T = q.shape[:2]
    n = T // CHUNK
    pad = ((0, 0), (C_WIN, 0), (0, 0), (0, 0))
    kp = jnp.pad(k, pad)
    vp = jnp.pad(v, pad)
    qc = jnp.moveaxis(q.reshape(B, n, CHUNK, H_C, DH), 1, 0)

    def one_chunk(xs):
        qn, ci = xs
        start = ci * CHUNK
        kb = lax.dynamic_slice_in_dim(kp, start, C_WIN + CHUNK, axis=1)
        vb = lax.dynamic_slice_in_dim(vp, start, C_WIN + CHUNK, axis=1)
        q_pos = start + jnp.arange(CHUNK)
        k_pos = start - C_WIN + jnp.arange(C_WIN + CHUNK)
        return _band_attend(qn, kb, vb, q_pos, k_pos, rel_table)

    out = lax.map(one_chunk, (qc, jnp.arange(n)))
    return jnp.moveaxis(out, 0, 1).reshape(B, T, H_C, DH)


def _odd_project(h, w_in):
    B, T = h.shape[:2]
    z = jnp.einsum('btd,de->bte', h, w_in)
    q, k, v = _split(z, (C_W, C_W, C_W))
    return q.reshape(B, T, H_C, DH), k.reshape(B, T, H_C, DH), v.reshape(B, T, H_C, DH)


def setup_inputs(seed: int = 0) -> dict:
    key = jax.random.key(seed)
    ks = jax.random.split(key, 26)

    def nrm(k, shape, s):
        return jax.random.normal(k, shape, F32) * s

    c_buf = min(C_WIN, PAST_LEN)
    return {
        "x_prompt": nrm(ks[0], (BATCH, SEQ, D_MODEL), 1.0),
        "x_sample": nrm(ks[1], (DEC_BATCH, DEC_SEQ, D_MODEL), 1.0),
        "c_prompt": nrm(ks[2], (BATCH, D_MODEL), 1.0),
        "c_sample": nrm(ks[3], (DEC_BATCH, D_MODEL), 1.0),
        "cache_a_k": nrm(ks[4], (N_EVEN, DEC_BATCH, PAST_LEN, H_A, DH), 1.0),
        "cache_a_v": nrm(ks[5], (N_EVEN, DEC_BATCH, PAST_LEN, H_A, DH), 1.0),
        "cache_a_kidx": nrm(ks[6], (N_EVEN, DEC_BATCH, PAST_LEN, D_I), 1.0),
        "state_b": nrm(ks[7], (N_EVEN, DEC_BATCH, H_B, DK_B, DV_B), 0.5),
        "cache_c_k": nrm(ks[8], (N_ODD, DEC_BATCH, c_buf, H_C, DH), 1.0),
        "cache_c_v": nrm(ks[9], (N_ODD, DEC_BATCH, c_buf, H_C, DH), 1.0),
        "w_ada": nrm(ks[10], (DEPTH, D_MODEL, 6 * D_MODEL), 0.3 * D_MODEL ** -0.5),
        "b_ada": nrm(ks[11], (DEPTH, 6 * D_MODEL), 0.02),
        "norm_mix": 1.0 + nrm(ks[12], (DEPTH, D_MODEL), 0.05),
        "norm_ffn": 1.0 + nrm(ks[13], (DEPTH, D_MODEL), 0.05),
        "norm_final": 1.0 + nrm(ks[14], (D_MODEL,), 0.05),
        "w_in_even": nrm(ks[15], (N_EVEN, D_MODEL, EVEN_IN), D_MODEL ** -0.5),
        "w_out_even": nrm(ks[16], (N_EVEN, A_W + B_VW, D_MODEL), (A_W + B_VW) ** -0.5),
        "t5_table": nrm(ks[17], (N_BUCKETS, H_A), 0.5),
        "lb_logits": nrm(ks[18], (N_EVEN + 1, B_KW), 0.5),
        "norm_b_out": 1.0 + nrm(ks[19], (N_EVEN, B_VW), 0.05),
        "w_in_odd": nrm(ks[20], (N_ODD, D_MODEL, 3 * C_W), D_MODEL ** -0.5),
        "w_out_odd": nrm(ks[21], (N_ODD, C_W, D_MODEL), C_W ** -0.5),
        "rel_table_c": nrm(ks[22], (N_ODD, 2 * REL_CLIP + 1, H_C), 0.5),
        "w_ffn_gate": nrm(ks[23], (DEPTH, D_MODEL, D_FF), D_MODEL ** -0.5),
        "w_ffn_up": nrm(ks[24], (DEPTH, D_MODEL, D_FF), D_MODEL ** -0.5),
        "w_ffn_down": nrm(ks[25], (DEPTH, D_FF, D_MODEL), D_FF ** -0.5),
    }


def reference(x_prompt, x_sample, c_prompt, c_sample, cache_a_k, cache_a_v, cache_a_kidx, state_b,
              cache_c_k, cache_c_v, w_ada, b_ada, norm_mix, norm_ffn, norm_final, w_in_even, w_out_even,
              t5_table, lb_logits, norm_b_out, w_in_odd, w_out_odd, rel_table_c,
              w_ffn_gate, w_ffn_up, w_ffn_down):
    Bp, T = x_prompt.shape[:2]
    Bs, Ts = x_sample.shape[:2]
    P = cache_a_k.shape[2]
    c_buf = cache_c_k.shape[2]
    lb_all = jnp.cumsum(jax.nn.softmax(lb_logits.astype(F32), axis=0), axis=0)
    pos_p = jnp.arange(T)
    pos_s = P + jnp.arange(Ts)
    kpos_s = jnp.arange(P + Ts)
    cpos_s = jnp.concatenate([P - c_buf + jnp.arange(c_buf), pos_s])
    ksel_p = min(K_TOP, T // 4)
    ksel_s = min(K_TOP, (P + Ts) // 4)
    c_keep = min(C_WIN, T)

    xp, xs = x_prompt, x_sample
    akp, avp, aip, bsp, ckp, cvp = [], [], [], [], [], []
    aks, avs, ais, bss, cks, cvs = [], [], [], [], [], []
    for l in range(DEPTH):
        sh1p, sc1p, g1p, sh2p, sc2p, g2p = _adaln(c_prompt, w_ada[l], b_ada[l])
        sh1s, sc1s, g1s, sh2s, sc2s, g2s = _adaln(c_sample, w_ada[l], b_ada[l])
        hp = _modulate(xp, norm_mix[l], sh1p, sc1p)
        hs = _modulate(xs, norm_mix[l], sh1s, sc1s)
        if l % 2 == 0:
            e = l // 2
            lb = lb_all[e]
            mp, ka, va, ki, S = _even_mixer(hp, None, None, None, jnp.zeros((Bp, H_B, DK_B, DV_B), F32),
                                            pos_p, pos_p, ksel_p, w_in_even[e], w_out_even[e],
                                            t5_table, lb, norm_b_out[e])
            akp.append(ka); avp.append(va); aip.append(ki); bsp.append(S)
            ms, ka, va, ki, S = _even_mixer(hs, cache_a_k[e], cache_a_v[e], cache_a_kidx[e], state_b[e],
                                            pos_s, kpos_s, ksel_s, w_in_even[e], w_out_even[e],
                                            t5_table, lb, norm_b_out[e])
            aks.append(ka); avs.append(va); ais.append(ki); bss.append(S)
        else:
            o = l // 2
            q, k, v = _odd_project(hp, w_in_odd[o])
            op = _chunk_band_prompt(q, k, v, rel_table_c[o])
            mp = jnp.einsum('bte,ed->btd', op.reshape(Bp, T, C_W), w_out_odd[o])
            ckp.append(k[:, T - c_keep:]); cvp.append(v[:, T - c_keep:])
            q, k, v = _odd_project(hs, w_in_odd[o])
            k_all = jnp.concatenate([cache_c_k[o].astype(k.dtype), k], axis=1)
            v_all = jnp.concatenate([cache_c_v[o].astype(v.dtype), v], axis=1)
            osm = _band_attend(q, k_all, v_all, pos_s, cpos_s, rel_table_c[o])
            ms = jnp.einsum('bte,ed->btd', osm.reshape(Bs, Ts, C_W), w_out_odd[o])
            cks.append(k); cvs.append(v)
        xp = xp + g1p[:, None] * mp
        xs = xs + g1s[:, None] * ms
        hp = _modulate(xp, norm_ffn[l], sh2p, sc2p)
        hs = _modulate(xs, norm_ffn[l], sh2s, sc2s)
        xp = xp + g2p[:, None] * _swiglu(hp, w_ffn_gate[l], w_ffn_up[l], w_ffn_down[l])
        xs = xs + g2s[:, None] * _swiglu(hs, w_ffn_gate[l], w_ffn_up[l], w_ffn_down[l])

    y_prompt = _rmsnorm(xp, norm_final)
    y_sample = _rmsnorm(xs, norm_final)
    a_k_p = jnp.stack(akp); a_v_p = jnp.stack(avp); a_kidx_p = jnp.stack(aip); b_state_p = jnp.stack(bsp)
    c_k_p = jnp.stack(ckp); c_v_p = jnp.stack(cvp)
    a_k_s = jnp.stack(aks); a_v_s = jnp.stack(avs); a_kidx_s = jnp.stack(ais); b_state_s = jnp.stack(bss)
    c_k_s = jnp.stack(cks); c_v_s = jnp.stack(cvs)
    return (y_prompt, y_sample, a_k_p, a_v_p, a_kidx_p, b_state_p, c_k_p, c_v_p,
            a_k_s, a_v_s, a_kidx_s, b_state_s, c_k_s, c_v_s)
```

```python
import functools
import math

import jax
import jax.numpy as jnp
from jax import lax
from jax.experimental import pallas as pl
from jax.experimental.pallas import tpu as pltpu

F32, BF16, I32 = jnp.float32, jnp.bfloat16, jnp.int32

CHUNK = 64
EPS = 1e-6
DH = 64
H_I = 8
D_I = 64
K_TOP = 256
N_BUCKETS = 32
T5_MAX_DIST = 1024
DK_B = 128
DV_B = 128
C_BACK = 8
REL_CLIP = 128

V7X_LANES = 128
V7X_VMEM_LIMIT = 56 * 2**20

INT_MIN = -(2**31)
TIE_NONE = 2**30
HIGHEST = lax.Precision.HIGHEST


def _cparams(*sem):
    return pltpu.CompilerParams(dimension_semantics=sem, vmem_limit_bytes=V7X_VMEM_LIMIT)


def _dot(a, b):
    return jnp.dot(a, b, preferred_element_type=F32)


def _dot_nt(a, b):
    return lax.dot_general(a, b, (((1,), (1,)), ((), ())), preferred_element_type=F32)


def _split_bf16(x):
    hi = x.astype(BF16)
    lo = (x - hi.astype(F32)).astype(BF16)
    return hi, lo


def _dot_nt_3pass(a_hi, a_lo, b_hi, b_lo):
    return _dot_nt(a_hi, b_hi) + _dot_nt(a_hi, b_lo) + _dot_nt(a_lo, b_hi)


def _silu(x):
    return x * jax.nn.sigmoid(x)


def _norm_mod(x, g, scale, shift):
    ms = jnp.mean(x * x, axis=-1, keepdims=True)
    return (x * lax.rsqrt(ms + EPS) * g) * (1.0 + scale) + shift


def _sortable(x):
    b = pltpu.bitcast(x + 0.0, I32)
    return b ^ ((b >> 31) & I32(0x7FFFFFFF))


def _row_tiles(B, T, rows=256):
    tt = min(T, rows)
    bb = max(1, min(B, rows // tt))
    assert T % tt == 0 and B % bb == 0
    return bb, tt


def _ada_body(c_ref, w_ref, b_ref, o_ref):
    s = _silu(c_ref[...])
    o_ref[0] = jnp.dot(s, w_ref[0], precision=HIGHEST, preferred_element_type=F32) + b_ref[0]


def _adaln_all(c_all, w_ada, b_ada):
    depth, D, E = w_ada.shape
    R = c_all.shape[0]
    tn = 1024
    return pl.pallas_call(
        _ada_body,
        grid=(depth, E // tn),
        in_specs=[pl.BlockSpec((R, D), lambda l, n: (0, 0)),
                  pl.BlockSpec((1, D, tn), lambda l, n: (l, 0, n)),
                  pl.BlockSpec((1, 1, tn), lambda l, n: (l, 0, n))],
        out_specs=pl.BlockSpec((1, R, tn), lambda l, n: (l, 0, n)),
        out_shape=jax.ShapeDtypeStruct((depth, R, E), F32),
        compiler_params=_cparams("arbitrary", "arbitrary"),
        name="adaln",
    )(c_all, w_ada, b_ada.reshape(depth, 1, E))


def _t5_bucket(rel):
    nb = N_BUCKETS // 2
    max_exact = nb // 2
    base = jnp.where(rel > 0, nb, 0)
    n = jnp.abs(rel)
    large = max_exact + (jnp.log(jnp.maximum(n, 1).astype(F32) / max_exact)
                         / math.log(T5_MAX_DIST / max_exact) * (nb - max_exact)).astype(I32)
    large = jnp.minimum(large, nb - 1)
    return base + jnp.where(n < max_exact, n, large)


def _clip_index(rel):
    return jnp.clip(-rel, -REL_CLIP, REL_CLIP) + REL_CLIP


def _bias_body(tab_ref, o_ref, *, rows, cols, width, n_tab, idx_fn, group_fn, mask_fn):
    g = pl.program_id(0)
    head, off = group_fn(g)
    i = lax.broadcasted_iota(I32, (1, width), 1)
    diff = jnp.where(i < width - rows, i, i - width)
    idx = idx_fn(diff + off)
    val = jnp.zeros((1, width), F32)
    for b in range(n_tab):
        val = jnp.where(idx == b, tab_ref[head, b], val)
    x = jnp.broadcast_to(val, (rows, width))
    y = pltpu.roll(x, 0, 1, stride=1, stride_axis=0)
    tile = y[:, :cols]
    if mask_fn is not None:
        t = lax.broadcasted_iota(I32, (rows, cols), 0)
        s = lax.broadcasted_iota(I32, (rows, cols), 1)
        tile = jnp.where(mask_fn(g, t, s), tile, -jnp.inf)
    o_ref[0] = tile


def _bias_tiles(table, *, groups, rows, cols, idx_fn, group_fn, mask_fn=None):
    n_tab = table.shape[0]
    width = -(-(rows + cols) // V7X_LANES) * V7X_LANES
    body = functools.partial(_bias_body, rows=rows, cols=cols, width=width, n_tab=n_tab,
                             idx_fn=idx_fn, group_fn=group_fn, mask_fn=mask_fn)
    return pl.pallas_call(
        body,
        grid=(groups,),
        in_specs=[pl.BlockSpec(memory_space=pltpu.SMEM)],
        out_specs=pl.BlockSpec((1, rows, cols), lambda g: (g, 0, 0)),
        out_shape=jax.ShapeDtypeStruct((groups, rows, cols), F32),
        compiler_params=_cparams("arbitrary"),
        name="bias_tiles",
    )(table.T)


def _in_even_body(x_ref, sc_ref, sh_ref, g_ref, wa_ref, wh_ref, wl_ref, wb_ref,
                  q_ref, k_ref, v_ref, kb_ref, vb_ref, qi_ref, ki_ref, wi_ref, zb_ref, *, aw):
    bb, tt, D = x_ref.shape
    M = bb * tt
    h = _norm_mod(x_ref[...], g_ref[...].reshape(1, 1, D), sc_ref[...], sh_ref[...]).reshape(M, D)
    h_hi, h_lo = _split_bf16(h)
    za = _dot(h_hi, wa_ref[...])
    q_ref[...] = (za[:, :aw] * (DH ** -0.5)).astype(BF16).reshape(bb, tt, aw)
    k = za[:, aw:2 * aw]
    v = za[:, 2 * aw:3 * aw]
    k_ref[...] = k.reshape(bb, tt, aw)
    v_ref[...] = v.reshape(bb, tt, aw)
    kb_ref[...] = k.astype(BF16).reshape(bb, tt, aw)
    vb_ref[...] = v.astype(BF16).reshape(bb, tt, aw)
    wh = wh_ref[...]
    zi = _dot(h_hi, wh) + _dot(h_hi, wl_ref[...]) + _dot(h_lo, wh)
    iw = H_I * D_I
    qi_ref[...] = zi[:, :iw].reshape(bb, tt, iw)
    ki_ref[...] = zi[:, iw:iw + D_I].reshape(bb, tt, D_I)
    wi_ref[...] = zi[:, iw + D_I:iw + D_I + H_I].reshape(bb, tt, H_I)
    zb_ref[...] = _dot(h_hi, wb_ref[...]).reshape(bb, tt, zb_ref.shape[-1])


def _in_even(x, scale, shift, g, wa, wh, wl, wb):
    B, T, D = x.shape
    aw = wa.shape[1] // 3
    bw = wb.shape[1]
    iw = H_I * D_I
    bb, tt = _row_tiles(B, T)
    row = lambda n: pl.BlockSpec((bb, tt, n), lambda b, t: (b, t, 0))
    mod = pl.BlockSpec((bb, 1, D), lambda b, t: (b, 0, 0))
    full = lambda a: pl.BlockSpec(a.shape, lambda b, t: (0,) * a.ndim)
    outs = [(aw, BF16), (aw, F32), (aw, F32), (aw, BF16), (aw, BF16), (iw, F32), (D_I, F32),
            (H_I, F32), (bw, F32)]
    return pl.pallas_call(
        functools.partial(_in_even_body, aw=aw),
        grid=(B // bb, T // tt),
        in_specs=[row(D), mod, mod, full(g), full(wa), full(wh), full(wl), full(wb)],
        out_specs=[row(n) for n, _ in outs],
        out_shape=[jax.ShapeDtypeStruct((B, T, n), dt) for n, dt in outs],
        compiler_params=_cparams("arbitrary", "arbitrary"),
        name="in_even",
    )(x, scale, shift, g, wa, wh, wl, wb)


def _in_odd_body(x_ref, sc_ref, sh_ref, g_ref, w_ref, q_ref, k_ref, v_ref, kb_ref, vb_ref):
    bb, tt, D = x_ref.shape
    M = bb * tt
    cw = q_ref.shape[-1]
    h = _norm_mod(x_ref[...], g_ref[...].reshape(1, 1, D), sc_ref[...], sh_ref[...]).reshape(M, D)
    z = _dot(h.astype(BF16), w_ref[...])
    q_ref[...] = (z[:, :cw] * (DH ** -0.5)).astype(BF16).reshape(bb, tt, cw)
    k = z[:, cw:2 * cw]
    v = z[:, 2 * cw:]
    k_ref[...] = k.reshape(bb, tt, cw)
    v_ref[...] = v.reshape(bb, tt, cw)
    kb_ref[...] = k.astype(BF16).reshape(bb, tt, cw)
    vb_ref[...] = v.astype(BF16).reshape(bb, tt, cw)


def _in_odd(x, scale, shift, g, w):
    B, T, D = x.shape
    cw = w.shape[1] // 3
    bb, tt = _row_tiles(B, T)
    row = lambda n: pl.BlockSpec((bb, tt, n), lambda b, t: (b, t, 0))
    mod = pl.BlockSpec((bb, 1, D), lambda b, t: (b, 0, 0))
    full = lambda a: pl.BlockSpec(a.shape, lambda b, t: (0,) * a.ndim)
    dts = [BF16, F32, F32, BF16, BF16]
    return pl.pallas_call(
        _in_odd_body,
        grid=(B // bb, T // tt),
        in_specs=[row(D), mod, mod, full(g), full(w)],
        out_specs=[row(cw) for _ in dts],
        out_shape=[jax.ShapeDtypeStruct((B, T, cw), dt) for dt in dts],
        compiler_params=_cparams("arbitrary", "arbitrary"),
        name="in_odd",
    )(x, scale, shift, g, w)


def _mix_ffn_body(*refs, n_mix, final):
    x_ref, g1_ref, sc2_ref, sh2_ref, g2_ref, gn_ref = refs[:6]
    mix_refs = refs[6:6 + n_mix]
    wmix_refs = refs[6 + n_mix:6 + 2 * n_mix]
    wg_ref, wu_ref, wd_ref = refs[6 + 2 * n_mix:9 + 2 * n_mix]
    gf_ref = refs[9 + 2 * n_mix] if final else None
    o_ref = refs[-1]
    bb, tt, D = x_ref.shape
    M = bb * tt
    m = None
    for mr, wr in zip(mix_refs, wmix_refs):
        part = _dot(mr[...].reshape(M, mr.shape[-1]), wr[...])
        m = part if m is None else m + part
    x1 = x_ref[...] + g1_ref[...] * m.reshape(bb, tt, D)
    h2 = _norm_mod(x1, gn_ref[...].reshape(1, 1, D), sc2_ref[...], sh2_ref[...])
    h2 = h2.reshape(M, D).astype(BF16)
    a = _dot(h2, wg_ref[...])
    b = _dot(h2, wu_ref[...])
    u = (_silu(a) * b).astype(BF16)
    f = _dot(u, wd_ref[...])
    x2 = x1 + g2_ref[...] * f.reshape(bb, tt, D)
    if final:
        ms = jnp.mean(x2 * x2, axis=-1, keepdims=True)
        x2 = x2 * lax.rsqrt(ms + EPS) * gf_ref[...].reshape(1, 1, D)
    o_ref[...] = x2


def _mix_ffn(x, g1, sc2, sh2, g2, gn, mixes, wmixes, wg, wu, wd, gf=None):
    B, T, D = x.shape
    bb, tt = _row_tiles(B, T)
    row = lambda n: pl.BlockSpec((bb, tt, n), lambda b, t: (b, t, 0))
    mod = pl.BlockSpec((bb, 1, D), lambda b, t: (b, 0, 0))
    full = lambda a: pl.BlockSpec(a.shape, lambda b, t: (0,) * a.ndim,
                                  pipeline_mode=pl.Buffered(1))
    final = gf is not None
    args = [x, g1, sc2, sh2, g2, gn, *mixes, *wmixes, wg, wu, wd] + ([gf] if final else [])
    specs = ([row(D), mod, mod, mod, mod, full(gn)] + [row(m.shape[-1]) for m in mixes]
             + [full(w) for w in wmixes] + [full(wg), full(wu), full(wd)]
             + ([full(gf)] if final else []))
    return pl.pallas_call(
        functools.partial(_mix_ffn_body, n_mix=len(mixes), final=final),
        grid=(B // bb, T // tt),
        in_specs=specs,
        out_specs=row(D),
        out_shape=jax.ShapeDtypeStruct((B, T, D), F32),
        compiler_params=_cparams("arbitrary", "arbitrary"),
        name="mix_ffn",
    )(*args)


def _hgrn_body(*refs, C, n_sub, e, has_s0):
    if has_s0:
        z_ref, lbl_ref, gn_ref, s0_ref, ob_ref, sout_ref, st_scr = refs
    else:
        z_ref, lbl_ref, gn_ref, ob_ref, sout_ref, st_scr = refs
    step = pl.program_id(1)
    n_heads = st_scr.shape[0]
    kw = n_heads * DK_B

    @pl.when(step == 0)
    def _():
        for h in range(n_heads):
            st_scr[h] = s0_ref[0, h].T if has_s0 else jnp.zeros((DV_B, DK_B), F32)

    ll = lbl_ref[...]
    ex = jnp.exp(ll - jnp.max(ll, axis=0, keepdims=True))
    sm = ex / jnp.sum(ex, axis=0, keepdims=True)
    lb = jnp.sum(sm[:e + 1], axis=0, keepdims=True)
    r = lax.broadcasted_iota(I32, (C, C), 0)
    c = lax.broadcasted_iota(I32, (C, C), 1)
    causal = r >= c
    tri = jnp.where(causal, 1.0, 0.0)
    gn = gn_ref[...]
    for sub in range(n_sub):
        z = z_ref[0, sub * C:(sub + 1) * C, :]
        qh = _silu(z[:, :kw])
        f = lb + (1.0 - lb) * jax.nn.sigmoid(z[:, kw:2 * kw])
        kh = 1.0 - f
        vh = z[:, 2 * kw:3 * kw]
        gate = _silu(z[:, 3 * kw:])
        G = jnp.dot(tri, jnp.log(f), precision=HIGHEST, preferred_element_type=F32)
        g_end = G[C - 1:C, :]
        mid = 0.5 * g_end
        q_in = qh * jnp.exp(G - mid)
        k_in = kh * jnp.exp(mid - G)
        q_st = qh * jnp.exp(G)
        k_st = kh * jnp.exp(g_end - G)
        dec = jnp.exp(g_end)
        outs = []
        for h in range(n_heads):
            sk = slice(h * DK_B, (h + 1) * DK_B)
            sv = slice(h * DV_B, (h + 1) * DV_B)
            att = _dot_nt(q_in[:, sk].astype(BF16), k_in[:, sk].astype(BF16))
            att = jnp.where(causal, att, 0.0)
            v_h = vh[:, sv]
            st = st_scr[h]
            o = (_dot(att.astype(BF16), v_h.astype(BF16))
                 + _dot_nt(q_st[:, sk].astype(BF16), st.astype(BF16)))
            st_scr[h] = st * dec[:, sk] + _dot(v_h.T.astype(BF16), k_st[:, sk].astype(BF16))
            o = o * lax.rsqrt(jnp.mean(o * o, axis=-1, keepdims=True) + EPS)
            outs.append(o)
        ob = jnp.concatenate(outs, axis=1) * gn * gate
        ob_ref[0, sub * C:(sub + 1) * C, :] = ob.astype(BF16)

    @pl.when(step == pl.num_programs(1) - 1)
    def _():
        for h in range(n_heads):
            sout_ref[0, h] = st_scr[h].T


def _hgrn(zb, lb_logits, g_norm, s0, e):
    B, T, zw = zb.shape
    kw = zw // 4
    n_heads = kw // DK_B
    C = min(CHUNK, T)
    ct = min(T, 4 * C)
    has_s0 = s0 is not None
    full = lambda a: pl.BlockSpec(a.shape, lambda b, t: (0,) * a.ndim)
    state = pl.BlockSpec((1, n_heads, DK_B, DV_B), lambda b, t: (b, 0, 0, 0))
    args = [zb, lb_logits, g_norm] + ([s0] if has_s0 else [])
    specs = ([pl.BlockSpec((1, ct, zw), lambda b, t: (b, t, 0)), full(lb_logits), full(g_norm)]
             + ([state] if has_s0 else []))
    return pl.pallas_call(
        functools.partial(_hgrn_body, C=C, n_sub=ct // C, e=e, has_s0=has_s0),
        grid=(B, T // ct),
        in_specs=specs,
        out_specs=[pl.BlockSpec((1, ct, kw), lambda b, t: (b, t, 0)), state],
        out_shape=[jax.ShapeDtypeStruct((B, T, kw), BF16),
                   jax.ShapeDtypeStruct((B, n_heads, DK_B, DV_B), F32)],
        scratch_shapes=[pltpu.VMEM((n_heads, DV_B, DK_B), F32)],
        compiler_params=_cparams("arbitrary", "arbitrary"),
        name="hgrn2",
    )(*args)


def _topk_threshold(count_ge, count_gt, count_tie_below, write_ties, rows, ksel, idx_bits):
    def thr_step(i, thr):
        cand = thr + lax.shift_left(I32(1), 31 - i)
        return jnp.where(count_ge(cand) >= ksel, cand, thr)

    thr = lax.fori_loop(0, 32, thr_step, jnp.full((rows, V7X_LANES), INT_MIN, I32))
    need = jnp.where(thr == INT_MIN, 0, ksel - count_gt(thr))
    write_ties(thr)

    def tie_step(i, end):
        cand = end + lax.shift_left(I32(1), idx_bits - 1 - i)
        return jnp.where(count_tie_below(cand) <= need, cand, end)

    tie_end = lax.fori_loop(0, idx_bits, tie_step, jnp.zeros((rows, V7X_LANES), I32))
    return thr, tie_end


def _lane_total(cnt):
    return jnp.broadcast_to(jnp.sum(cnt, axis=1, keepdims=True), cnt.shape)


def _dsa_prompt_body(q_ref, qi_ref, wi_ref, k_ref, v_ref, ki_ref, bias_ref, o_ref,
                     key_scr, tix_scr, wb_scr, qz_scr, m_scr, l_scr, acc_scr,
                     *, tq, ksel, idx_bits):
    L = V7X_LANES
    nl = tq // L
    n_heads = qz_scr.shape[0]
    qt = pl.program_id(1)
    nkb = qt + 1
    lane = lax.broadcasted_iota(I32, (tq, L), 1)
    low_half = lane < DH
    shift = CHUNK.bit_length() - 1

    qi = qi_ref[0]
    qi_hm = jnp.concatenate([qi[:, h * D_I:(h + 1) * D_I] for h in range(H_I)], axis=0)
    qi_hi, qi_lo = _split_bf16(qi_hm)
    w = wi_ref[0] * (H_I ** -0.5 * D_I ** -0.5)
    for h in range(H_I):
        wb_scr[h] = jnp.broadcast_to(w[:, h:h + 1], (tq, L))
    t_chunk = (qt * tq + lax.broadcasted_iota(I32, (tq, tq), 0)) >> shift

    def index_block(kb, carry):
        ks = pl.multiple_of(kb * tq, tq)
        ki_hi, ki_lo = _split_bf16(ki_ref[0, pl.ds(ks, tq), :])
        s = _dot_nt_3pass(qi_hi, qi_lo, ki_hi, ki_lo)
        cols = []
        for c in range(nl):
            acc = None
            for h in range(H_I):
                term = wb_scr[h] * jnp.maximum(s[h * tq:(h + 1) * tq, c * L:(c + 1) * L], 0.0)
                acc = term if acc is None else acc + term
            cols.append(acc)
        key = _sortable(jnp.concatenate(cols, axis=1))
        s_chunk = (ks + lax.broadcasted_iota(I32, (tq, tq), 1)) >> shift
        key_scr[kb] = jnp.where(s_chunk <= t_chunk, key, INT_MIN)
        return carry

    lax.fori_loop(0, nkb, index_block, 0)

    def count(scr, pred):
        def body(kb, cnt):
            for c in range(nl):
                cnt = cnt + jnp.where(pred(scr[kb, :, c * L:(c + 1) * L]), 1, 0)
            return cnt
        return _lane_total(lax.fori_loop(0, nkb, body, jnp.zeros((tq, L), I32)))

    def write_ties(thr):
        def body(kb, carry):
            for c in range(nl):
                idx = kb * tq + c * L + lane
                tix_scr[kb, :, c * L:(c + 1) * L] = jnp.where(
                    key_scr[kb, :, c * L:(c + 1) * L] == thr, idx, TIE_NONE)
            return carry
        lax.fori_loop(0, nkb, body, 0)

    thr, tie_end = _topk_threshold(
        lambda x: count(key_scr, lambda k: k >= x),
        lambda x: count(key_scr, lambda k: k > x),
        lambda x: count(tix_scr, lambda t: t < x),
        write_ties, tq, ksel, idx_bits)

    q = q_ref[0].astype(F32)
    for h in range(n_heads):
        qp = q[:, (h // 2) * L:(h // 2 + 1) * L]
        keep = low_half if h % 2 == 0 else jnp.logical_not(low_half)
        qz_scr[h] = jnp.where(keep, qp, 0.0).astype(BF16)
        m_scr[h] = jnp.full((tq, L), -1e30, F32)
        l_scr[h] = jnp.zeros((tq, L), F32)
    for j in range(n_heads // 2):
        acc_scr[j] = jnp.zeros((tq, L), F32)

    def attend_block(kb, carry):
        ks = pl.multiple_of(kb * tq, tq)
        negs = []
        for c in range(nl):
            sel = jnp.logical_or(key_scr[kb, :, c * L:(c + 1) * L] > thr,
                                 tix_scr[kb, :, c * L:(c + 1) * L] < tie_end)
            negs.append(jnp.where(sel, 0.0, -jnp.inf))
        neg = jnp.concatenate(negs, axis=1)
        d = qt - kb
        for j in range(n_heads // 2):
            kp = k_ref[0, pl.ds(ks, tq), j * L:(j + 1) * L]
            vp = v_ref[0, pl.ds(ks, tq), j * L:(j + 1) * L]
            alphas, pvs = [], []
            for h in (2 * j, 2 * j + 1):
                s = _dot_nt(qz_scr[h], kp) + bias_ref[d, h] + neg
                m_old = m_scr[h]
                m_new = jnp.maximum(m_old, jnp.max(s, axis=1, keepdims=True))
                alpha = jnp.exp(m_old - m_new)
                p = jnp.exp(s - jnp.concatenate([m_new] * nl, axis=1))
                l_scr[h] = alpha * l_scr[h] + jnp.sum(p, axis=1, keepdims=True)
                m_scr[h] = m_new
                alphas.append(alpha)
                pvs.append(_dot(p.astype(BF16), vp))
            acc_scr[j] = (acc_scr[j] * jnp.where(low_half, alphas[0], alphas[1])
                          + jnp.where(low_half, pvs[0], pvs[1]))
        return carry

    lax.fori_loop(0, nkb, attend_block, 0)
    for j in range(n_heads // 2):
        den = jnp.where(low_half, l_scr[2 * j], l_scr[2 * j + 1])
        o_ref[0, :, j * L:(j + 1) * L] = (acc_scr[j] / den).astype(BF16)


def _dsa_prompt(q, qi, wi, kb16, vb16, ki, bias, ksel):
    B, T, aw = q.shape
    n_heads = aw // DH
    tq = bias.shape[-1]
    nq = T // tq
    L = V7X_LANES
    row = lambda n: pl.BlockSpec((1, tq, n), lambda b, t: (b, t, 0))
    seq = lambda n: pl.BlockSpec((1, T, n), lambda b, t: (b, 0, 0))
    return pl.pallas_call(
        functools.partial(_dsa_prompt_body, tq=tq, ksel=ksel, idx_bits=T.bit_length()),
        grid=(B, nq),
        in_specs=[row(aw), row(H_I * D_I), row(H_I), seq(aw), seq(aw), seq(D_I),
                  pl.BlockSpec(bias.shape, lambda b, t: (0, 0, 0, 0),
                               pipeline_mode=pl.Buffered(1))],
        out_specs=row(aw),
        out_shape=jax.ShapeDtypeStruct((B, T, aw), BF16),
        scratch_shapes=[pltpu.VMEM((nq, tq, tq), I32), pltpu.VMEM((nq, tq, tq), I32),
                        pltpu.VMEM((H_I, tq, L), F32), pltpu.VMEM((n_heads, tq, L), BF16),
                        pltpu.VMEM((n_heads, tq, L), F32), pltpu.VMEM((n_heads, tq, L), F32),
                        pltpu.VMEM((n_heads // 2, tq, L), F32)],
        compiler_params=_cparams("arbitrary", "arbitrary"),
        name="dsa_prompt",
    )(q, qi, wi, kb16, vb16, ki, bias)


def _pack_heads(q, n_heads):
    T, W = q.shape
    head_of_lane = lax.broadcasted_iota(I32, (T, W), 1) >> (DH.bit_length() - 1)
    qf = q.astype(F32)
    return jnp.concatenate([jnp.where(head_of_lane == h, qf, 0.0) for h in range(n_heads)],
                           axis=0).astype(BF16)


def _unpack_heads(acc, den, n_heads):
    T = acc.shape[0] // n_heads
    W = acc.shape[1]
    head_of_lane = lax.broadcasted_iota(I32, (T, W), 1) >> (DH.bit_length() - 1)
    out = jnp.zeros((T, W), F32)
    for h in range(n_heads):
        blk = acc[h * T:(h + 1) * T, :] / den[h * T:(h + 1) * T, :]
        out = jnp.where(head_of_lane == h, blk, out)
    return out


def _dsa_sample_body(q_ref, qi_ref, wi_ref, kn_ref, vn_ref, kin_ref, kc_ref, vc_ref, kic_ref,
                     bias_ref, biasn_ref, o_ref,
                     key_scr, tix_scr, keyn_scr, tixn_scr, sel_scr, qbd_scr, m_scr, l_scr, acc_scr,
                     *, ts, kc_len, n_new, ksel, idx_bits):
    L = V7X_LANES
    n_heads = qbd_scr.shape[0] // ts
    nc = key_scr.shape[0]
    step = pl.program_id(1)
    past = nc * kc_len
    lane = lax.broadcasted_iota(I32, (ts, L), 1)

    @pl.when(step == 0)
    def _():
        qi = qi_ref[0]
        qi_hm = jnp.concatenate([qi[:, h * D_I:(h + 1) * D_I] for h in range(H_I)], axis=0)
        qi_hi, qi_lo = _split_bf16(qi_hm)
        w = wi_ref[0] * (H_I ** -0.5 * D_I ** -0.5)

        def scores(ki):
            ki_hi, ki_lo = _split_bf16(ki)
            s = _dot_nt_3pass(qi_hi, qi_lo, ki_hi, ki_lo)
            acc = None
            for h in range(H_I):
                term = w[:, h:h + 1] * jnp.maximum(s[h * ts:(h + 1) * ts, :], 0.0)
                acc = term if acc is None else acc + term
            return _sortable(acc)

        for c in range(nc):
            key_scr[c] = scores(kic_ref[0, c * kc_len:(c + 1) * kc_len, :])
        keyn_scr[...] = jnp.where(lane < n_new, scores(kin_ref[0]), INT_MIN)

        def count(pred_key=None, pred_tie=None):
            cnt = jnp.zeros((ts, L), I32)
            for c in range(nc):
                for i in range(kc_len // L):
                    if pred_key is not None:
                        hit = pred_key(key_scr[c, :, i * L:(i + 1) * L])
                    else:
                        hit = pred_tie(tix_scr[c, :, i * L:(i + 1) * L])
                    cnt = cnt + jnp.where(hit, 1, 0)
            hit = pred_key(keyn_scr[...]) if pred_key is not None else pred_tie(tixn_scr[...])
            return _lane_total(cnt + jnp.where(hit, 1, 0))

        def write_ties(thr):
            for c in range(nc):
                for i in range(kc_len // L):
                    idx = c * kc_len + i * L + lane
                    tix_scr[c, :, i * L:(i + 1) * L] = jnp.where(
                        key_scr[c, :, i * L:(i + 1) * L] == thr, idx, TIE_NONE)
            tixn_scr[...] = jnp.where(keyn_scr[...] == thr, past + lane, TIE_NONE)

        thr, tie_end = _topk_threshold(
            lambda x: count(pred_key=lambda k: k >= x),
            lambda x: count(pred_key=lambda k: k > x),
            lambda x: count(pred_tie=lambda t: t < x),
            write_ties, ts, ksel, idx_bits)
        sel_scr[0] = thr
        sel_scr[1] = tie_end
        qbd_scr[...] = _pack_heads(q_ref[0], n_heads)
        m_scr[...] = jnp.full(m_scr.shape, -1e30, F32)
        l_scr[...] = jnp.zeros(l_scr.shape, F32)
        acc_scr[...] = jnp.zeros(acc_scr.shape, F32)

    thr = sel_scr[0]
    tie_end = sel_scr[1]

    def attend(keys, ties, k16, v16, bias):
        n = keys.shape[1]
        negs = []
        for i in range(n // L):
            sel = jnp.logical_or(keys[:, i * L:(i + 1) * L] > thr,
                                 ties[:, i * L:(i + 1) * L] < tie_end)
            negs.append(jnp.where(sel, 0.0, -jnp.inf))
        neg = jnp.concatenate(negs, axis=1) if len(negs) > 1 else negs[0]
        neg = jnp.concatenate([neg] * n_heads, axis=0)
        s = _dot_nt(qbd_scr[...], k16) + bias + neg
        m_old = m_scr[...]
        m_new = jnp.maximum(m_old, jnp.max(s, axis=1, keepdims=True))
        alpha = jnp.exp(m_old - m_new)
        p = jnp.exp(s - m_new)
        l_scr[...] = alpha * l_scr[...] + jnp.sum(p, axis=1, keepdims=True)
        m_scr[...] = m_new
        acc_scr[...] = acc_scr[...] * alpha + _dot(p.astype(BF16), v16)

    attend(key_scr[step], tix_scr[step], kc_ref[0].astype(BF16), vc_ref[0].astype(BF16),
           bias_ref[...])

    @pl.when(step == nc - 1)
    def _():
        attend(keyn_scr[...], tixn_scr[...], kn_ref[0], vn_ref[0], biasn_ref[...])
        o_ref[0] = _unpack_heads(acc_scr[...], l_scr[...], n_heads).astype(BF16)


def _dsa_sample(q, qi, wi, kn16, vn16, kin, cache_k, cache_v, cache_ki, bias, n_new, ksel):
    B, ts, aw = q.shape
    n_heads = aw // DH
    P = cache_k.shape[1]
    npad = kn16.shape[1]
    kc_len = min(P, 1024)
    nc = P // kc_len
    L = V7X_LANES
    per_b = lambda a: pl.BlockSpec((1,) + a.shape[1:], lambda b, c: (b, 0, 0))
    chunk = lambda n: pl.BlockSpec((1, kc_len, n), lambda b, c: (b, c, 0))
    rows = n_heads * ts
    return pl.pallas_call(
        functools.partial(_dsa_sample_body, ts=ts, kc_len=kc_len, n_new=n_new, ksel=ksel,
                          idx_bits=(P + npad).bit_length()),
        grid=(B, nc),
        in_specs=[per_b(q), per_b(qi), per_b(wi), per_b(kn16), per_b(vn16), per_b(kin),
                  chunk(aw), chunk(aw), per_b(cache_ki),
                  pl.BlockSpec((rows, kc_len), lambda b, c: (0, c)),
                  pl.BlockSpec((rows, npad), lambda b, c: (0, P // npad))],
        out_specs=pl.BlockSpec((1, ts, aw), lambda b, c: (b, 0, 0)),
        out_shape=jax.ShapeDtypeStruct((B, ts, aw), BF16),
        scratch_shapes=[pltpu.VMEM((nc, ts, kc_len), I32), pltpu.VMEM((nc, ts, kc_len), I32),
                        pltpu.VMEM((ts, npad), I32), pltpu.VMEM((ts, npad), I32),
                        pltpu.VMEM((2, ts, L), I32), pltpu.VMEM((rows, aw), BF16),
                        pltpu.VMEM((rows, 1), F32), pltpu.VMEM((rows, 1), F32),
                        pltpu.VMEM((rows, aw), F32)],
        compiler_params=_cparams("arbitrary", "arbitrary"),
        name="dsa_sample",
    )(q, qi, wi, kn16, vn16, kin, cache_k, cache_v, cache_ki, bias, bias)


def _band_prompt_body(q_ref, k_ref, v_ref, bias_ref, o_ref, *, tq, n_back):
    L = V7X_LANES
    n_heads = q_ref.shape[-1] // DH
    qt = pl.program_id(1)
    lane = lax.broadcasted_iota(I32, (tq, L), 1)
    low_half = lane < DH
    starts, pads = [], []
    for j in range(n_back + 1):
        kb = qt - n_back + j
        starts.append(pl.multiple_of(jnp.maximum(kb, 0) * tq, tq))
        pads.append(jnp.where(kb >= 0, 0.0, -jnp.inf))
    for pair in range(n_heads // 2):
        qp = q_ref[0, :, pair * L:(pair + 1) * L].astype(F32)
        outs, dens = [], []
        for h in (2 * pair, 2 * pair + 1):
            keep = low_half if h % 2 == 0 else jnp.logical_not(low_half)
            qz = jnp.where(keep, qp, 0.0).astype(BF16)
            ss = []
            for j in range(n_back + 1):
                kp = k_ref[0, pl.ds(starts[j], tq), pair * L:(pair + 1) * L]
                ss.append(_dot_nt(qz, kp) + (bias_ref[h, j] + pads[j]))
            m = functools.reduce(jnp.maximum, [jnp.max(s, axis=1, keepdims=True) for s in ss])
            den, out = None, None
            for j in range(n_back + 1):
                p = jnp.exp(ss[j] - m)
                vp = v_ref[0, pl.ds(starts[j], tq), pair * L:(pair + 1) * L]
                dj = jnp.sum(p, axis=1, keepdims=True)
                oj = _dot(p.astype(BF16), vp)
                den = dj if den is None else den + dj
                out = oj if out is None else out + oj
            outs.append(out)
            dens.append(jnp.broadcast_to(den, (tq, L)))
        o_ref[0, :, pair * L:(pair + 1) * L] = (
            jnp.where(low_half, outs[0], outs[1]) / jnp.where(low_half, dens[0], dens[1])
        ).astype(BF16)


def _band_prompt(q, kb16, vb16, bias):
    B, T, cw = q.shape
    tq = bias.shape[-1]
    n_back = bias.shape[1] - 1
    row = pl.BlockSpec((1, tq, cw), lambda b, t: (b, t, 0))
    seq = pl.BlockSpec((1, T, cw), lambda b, t: (b, 0, 0))
    return pl.pallas_call(
        functools.partial(_band_prompt_body, tq=tq, n_back=n_back),
        grid=(B, T // tq),
        in_specs=[row, seq, seq,
                  pl.BlockSpec(bias.shape, lambda b, t: (0, 0, 0, 0), pipeline_mode=pl.Buffered(1))],
        out_specs=row,
        out_shape=jax.ShapeDtypeStruct((B, T, cw), BF16),
        compiler_params=_cparams("arbitrary", "arbitrary"),
        name="band_prompt",
    )(q, kb16, vb16, bias)


def _band_sample_body(q_ref, kn_ref, vn_ref, kc_ref, vc_ref, bias_ref, biasn_ref, o_ref):
    n_heads = q_ref.shape[-1] // DH
    qbd = _pack_heads(q_ref[0], n_heads)
    s_c = _dot_nt(qbd, kc_ref[0].astype(BF16)) + bias_ref[...]
    s_n = _dot_nt(qbd, kn_ref[0]) + biasn_ref[...]
    m = jnp.maximum(jnp.max(s_c, axis=1, keepdims=True), jnp.max(s_n, axis=1, keepdims=True))
    p_c = jnp.exp(s_c - m)
    p_n = jnp.exp(s_n - m)
    den = jnp.sum(p_c, axis=1, keepdims=True) + jnp.sum(p_n, axis=1, keepdims=True)
    acc = _dot(p_c.astype(BF16), vc_ref[0].astype(BF16)) + _dot(p_n.astype(BF16), vn_ref[0])
    o_ref[0] = _unpack_heads(acc, den, n_heads).astype(BF16)


def _band_sample(q, kn16, vn16, cache_k, cache_v, bias):
    B, ts, cw = q.shape
    n_heads = cw // DH
    cb = cache_k.shape[1]
    npad = kn16.shape[1]
    rows = n_heads * ts
    per_b = lambda a: pl.BlockSpec((1,) + a.shape[1:], lambda b: (b, 0, 0))
    return pl.pallas_call(
        _band_sample_body,
        grid=(B,),
        in_specs=[per_b(q), per_b(kn16), per_b(vn16), per_b(cache_k), per_b(cache_v),
                  pl.BlockSpec((rows, cb), lambda b: (0, 0)),
                  pl.BlockSpec((rows, npad), lambda b: (0, cb // npad))],
        out_specs=per_b(q),
        out_shape=jax.ShapeDtypeStruct((B, ts, cw), BF16),
        compiler_params=_cparams("arbitrary"),
        name="band_sample",
    )(q, kn16, vn16, cache_k, cache_v, bias, bias)


def _pad_rows(a, n):
    return jnp.pad(a, ((0, 0), (0, n - a.shape[1]), (0, 0)))


def kernel(x_prompt, x_sample, c_prompt, c_sample, cache_a_k, cache_a_v, cache_a_kidx, state_b,
           cache_c_k, cache_c_v, w_ada, b_ada, norm_mix, norm_ffn, norm_final, w_in_even,
           w_out_even, t5_table, lb_logits, norm_b_out, w_in_odd, w_out_odd, rel_table_c,
           w_ffn_gate, w_ffn_up, w_ffn_down):
    Bp, T, D = x_prompt.shape
    Bs, Ts, _ = x_sample.shape
    P = cache_a_k.shape[2]
    c_buf = cache_c_k.shape[2]
    depth = w_ada.shape[0]
    L = V7X_LANES
    n_heads_a = cache_a_k.shape[3]
    aw = n_heads_a * DH
    n_heads_c = cache_c_k.shape[3]
    bw = w_out_even.shape[1] - aw
    iw = H_I * D_I
    ksel_p = min(K_TOP, T // 4)
    ksel_s = min(K_TOP, (P + Ts) // 4)
    c_keep = min(C_BACK * CHUNK, T)
    tq = min(T, 256)
    npad = L
    assert Ts <= npad and P % npad == 0 and c_buf % npad == 0 and T % tq == 0

    mod = _adaln_all(jnp.concatenate([c_prompt, c_sample], axis=0), w_ada, b_ada)

    def mods(l):
        parts = jnp.split(mod[l], 6, axis=-1)
        return ([p[:Bp].reshape(Bp, 1, D) for p in parts],
                [p[Bp:].reshape(Bs, 1, D) for p in parts])

    nq = T // tq
    t5_prompt = _bias_tiles(
        t5_table, groups=nq * n_heads_a, rows=tq, cols=tq, idx_fn=_t5_bucket,
        group_fn=lambda g: (g % n_heads_a, -(g // n_heads_a) * tq),
    ).reshape(nq, n_heads_a, tq, tq)
    t5_sample = _bias_tiles(
        t5_table, groups=n_heads_a, rows=Ts, cols=P + npad, idx_fn=_t5_bucket,
        group_fn=lambda g: (g, -P),
    ).reshape(n_heads_a * Ts, P + npad)

    xp, xs = x_prompt, x_sample
    outs_p = {k: [] for k in ("ak", "av", "ai", "bs", "ck", "cv")}
    outs_s = {k: [] for k in ("ak", "av", "ai", "bs", "ck", "cv")}
    for l in range(depth):
        (sh1p, sc1p, g1p, sh2p, sc2p, g2p), (sh1s, sc1s, g1s, sh2s, sc2s, g2s) = mods(l)
        gm = norm_mix[l].reshape(1, D)
        gn = norm_ffn[l].reshape(1, D)
        wg = w_ffn_gate[l].astype(BF16)
        wu = w_ffn_up[l].astype(BF16)
        wd = w_ffn_down[l].astype(BF16)
        gf = norm_final.reshape(1, D) if l == depth - 1 else None
        if l % 2 == 0:
            e = l // 2
            w_in = w_in_even[e]
            wa = w_in[:, :3 * aw].astype(BF16)
            wi_cols = w_in[:, 3 * aw:3 * aw + iw + D_I + H_I]
            wi_cols = jnp.pad(wi_cols, ((0, 0), (0, -wi_cols.shape[1] % L)))
            wh = wi_cols.astype(BF16)
            wl = (wi_cols - wh.astype(F32)).astype(BF16)
            wb = w_in[:, 3 * aw + iw + D_I + H_I:].astype(BF16)
            w_out = w_out_even[e].astype(BF16)
            gb_norm = norm_b_out[e].reshape(1, bw)

            q, k, v, k16, v16, qi, ki, wi, zb = _in_even(xp, sc1p, sh1p, gm, wa, wh, wl, wb)
            oa = _dsa_prompt(q, qi, wi, k16, v16, ki, t5_prompt, ksel_p)
            ob, st = _hgrn(zb, lb_logits, gb_norm, None, e)
            outs_p["ak"].append(k.reshape(Bp, T, n_heads_a, DH))
            outs_p["av"].append(v.reshape(Bp, T, n_heads_a, DH))
            outs_p["ai"].append(ki)
            outs_p["bs"].append(st)
            xp = _mix_ffn(xp, g1p, sc2p, sh2p, g2p, gn, [oa, ob], [w_out[:aw], w_out[aw:]],
                          wg, wu, wd, gf)

            q, k, v, k16, v16, qi, ki, wi, zb = _in_even(xs, sc1s, sh1s, gm, wa, wh, wl, wb)
            oa = _dsa_sample(q, qi, wi, _pad_rows(k16, npad), _pad_rows(v16, npad),
                             _pad_rows(ki, npad), cache_a_k[e].reshape(Bs, P, aw),
                             cache_a_v[e].reshape(Bs, P, aw), cache_a_kidx[e], t5_sample,
                             Ts, ksel_s)
            ob, st = _hgrn(zb, lb_logits, gb_norm, state_b[e], e)
            outs_s["ak"].append(k.reshape(Bs, Ts, n_heads_a, DH))
            outs_s["av"].append(v.reshape(Bs, Ts, n_heads_a, DH))
            outs_s["ai"].append(ki)
            outs_s["bs"].append(st)
            xs = _mix_ffn(xs, g1s, sc2s, sh2s, g2s, gn, [oa, ob], [w_out[:aw], w_out[aw:]],
                          wg, wu, wd, gf)
        else:
            o = l // 2
            cw = n_heads_c * DH
            w_in = w_in_odd[o].astype(BF16)
            w_out = w_out_odd[o].astype(BF16)
            n_back = -(-(C_BACK * CHUNK) // tq)
            shift = CHUNK.bit_length() - 1

            def band_mask(g, t, s, n_back=n_back):
                j = g % (n_back + 1)
                kc = (s + (j - n_back) * tq) >> shift
                tc = t >> shift
                return jnp.logical_and(kc <= tc, kc >= tc - C_BACK)

            band_p = _bias_tiles(
                rel_table_c[o], groups=n_heads_c * (n_back + 1), rows=tq, cols=tq,
                idx_fn=_clip_index,
                group_fn=lambda g, n_back=n_back: (g // (n_back + 1),
                                                   (g % (n_back + 1) - n_back) * tq),
                mask_fn=band_mask,
            ).reshape(n_heads_c, n_back + 1, tq, tq)
            band_s = _bias_tiles(
                rel_table_c[o], groups=n_heads_c, rows=Ts, cols=c_buf + npad, idx_fn=_clip_index,
                group_fn=lambda g: (g, -c_buf),
                mask_fn=lambda g, t, s: s < c_buf + Ts,
            ).reshape(n_heads_c * Ts, c_buf + npad)

            q, k, v, k16, v16 = _in_odd(xp, sc1p, sh1p, gm, w_in)
            op = _band_prompt(q, k16, v16, band_p)
            outs_p["ck"].append(k[:, T - c_keep:].reshape(Bp, c_keep, n_heads_c, DH))
            outs_p["cv"].append(v[:, T - c_keep:].reshape(Bp, c_keep, n_heads_c, DH))
            xp = _mix_ffn(xp, g1p, sc2p, sh2p, g2p, gn, [op], [w_out], wg, wu, wd, gf)

            q, k, v, k16, v16 = _in_odd(xs, sc1s, sh1s, gm, w_in)
            osm = _band_sample(q, _pad_rows(k16, npad), _pad_rows(v16, npad),
                               cache_c_k[o].reshape(Bs, c_buf, cw),
                               cache_c_v[o].reshape(Bs, c_buf, cw), band_s)
            outs_s["ck"].append(k.reshape(Bs, Ts, n_heads_c, DH))
            outs_s["cv"].append(v.reshape(Bs, Ts, n_heads_c, DH))
            xs = _mix_ffn(xs, g1s, sc2s, sh2s, g2s, gn, [osm], [w_out], wg, wu, wd, gf)

    st = lambda xs_: jnp.stack(xs_)
    return (xp, xs, st(outs_p["ak"]), st(outs_p["av"]), st(outs_p["ai"]), st(outs_p["bs"]),
            st(outs_p["ck"]), st(outs_p["cv"]), st(outs_s["ak"]), st(outs_s["av"]),
            st(outs_s["ai"]), st(outs_s["bs"]), st(outs_s["ck"]), st(outs_s["cv"]))
```

```python
import functools
import math

import jax
import jax.numpy as jnp
from jax import lax
from jax.experimental import pallas as pl
from jax.experimental.pallas import tpu as pltpu

F32, BF16, I32 = jnp.float32, jnp.bfloat16, jnp.int32

CHUNK = 64
EPS = 1e-6
DH = 64
H_I = 8
D_I = 64
K_TOP = 256
N_BUCKETS = 32
T5_MAX_DIST = 1024
DK_B = 128
DV_B = 128
C_BACK = 8
REL_CLIP = 128

V7X_LANES = 128
V7X_VMEM_LIMIT = 56 * 2**20

INT_MIN = -(2**31)
TIE_NONE = 2**30
HIGHEST = lax.Precision.HIGHEST


def _cparams(*sem):
    return pltpu.CompilerParams(dimension_semantics=sem, vmem_limit_bytes=V7X_VMEM_LIMIT)


def _dot(a, b):
    return jnp.dot(a, b, preferred_element_type=F32)


def _dot_nt(a, b):
    return lax.dot_general(a, b, (((1,), (1,)), ((), ())), preferred_element_type=F32)


def _split_bf16(x):
    hi = x.astype(BF16)
    lo = (x - hi.astype(F32)).astype(BF16)
    return hi, lo


def _dot_nt_3pass(a_hi, a_lo, b_hi, b_lo):
    return _dot_nt(a_hi, b_hi) + _dot_nt(a_hi, b_lo) + _dot_nt(a_lo, b_hi)


def _silu(x):
    return x * jax.nn.sigmoid(x)


def _norm_mod(x, g, scale, shift):
    ms = jnp.mean(x * x, axis=-1, keepdims=True)
    return (x * lax.rsqrt(ms + EPS) * g) * (1.0 + scale) + shift


def _sortable(x):
    b = pltpu.bitcast(x + 0.0, I32)
    return b ^ ((b >> 31) & I32(0x7FFFFFFF))


def _row_tiles(B, T, rows=256):
    tt = min(T, rows)
    bb = max(1, min(B, rows // tt))
    assert T % tt == 0 and B % bb == 0
    return bb, tt


def _ada_body(c_ref, w_ref, b_ref, o_ref):
    s = _silu(c_ref[...])
    o_ref[0] = jnp.dot(s, w_ref[0], precision=HIGHEST, preferred_element_type=F32) + b_ref[0]


def _adaln_all(c_all, w_ada, b_ada):
    depth, D, E = w_ada.shape
    R = c_all.shape[0]
    tn = 1024
    return pl.pallas_call(
        _ada_body,
        grid=(depth, E // tn),
        in_specs=[pl.BlockSpec((R, D), lambda l, n: (0, 0)),
                  pl.BlockSpec((1, D, tn), lambda l, n: (l, 0, n)),
                  pl.BlockSpec((1, 1, tn), lambda l, n: (l, 0, n))],
        out_specs=pl.BlockSpec((1, R, tn), lambda l, n: (l, 0, n)),
        out_shape=jax.ShapeDtypeStruct((depth, R, E), F32),
        compiler_params=_cparams("arbitrary", "arbitrary"),
        name="adaln",
    )(c_all, w_ada, b_ada.reshape(depth, 1, E))


def _t5_bucket(rel):
    nb = N_BUCKETS // 2
    max_exact = nb // 2
    base = jnp.where(rel > 0, nb, 0)
    n = jnp.abs(rel)
    large = max_exact + (jnp.log(jnp.maximum(n, 1).astype(F32) / max_exact)
                         / math.log(T5_MAX_DIST / max_exact) * (nb - max_exact)).astype(I32)
    large = jnp.minimum(large, nb - 1)
    return base + jnp.where(n < max_exact, n, large)


def _clip_index(rel):
    return jnp.clip(-rel, -REL_CLIP, REL_CLIP) + REL_CLIP


def _bias_body(tab_ref, o_ref, *, rows, cols, width, n_tab, idx_fn, group_fn, mask_fn,
               keys_on_rows):
    g = pl.program_id(0)
    head, off = group_fn(g)
    i = lax.broadcasted_iota(I32, (1, width), 1)
    diff = jnp.where(i < width - rows, i, i - width)
    idx = idx_fn(off - diff if keys_on_rows else off + diff)
    val = jnp.zeros((1, width), F32)
    for b in range(n_tab):
        val = jnp.where(idx == b, tab_ref[head, b], val)
    x = jnp.broadcast_to(val, (rows, width))
    y = pltpu.roll(x, 0, 1, stride=1, stride_axis=0)
    tile = y[:, :cols]
    if mask_fn is not None:
        t = lax.broadcasted_iota(I32, (rows, cols), 0)
        s = lax.broadcasted_iota(I32, (rows, cols), 1)
        tile = jnp.where(mask_fn(g, t, s), tile, -jnp.inf)
    o_ref[0] = tile


def _bias_tiles(table, *, groups, rows, cols, idx_fn, group_fn, mask_fn=None,
                keys_on_rows=False):
    n_tab = table.shape[0]
    width = -(-(rows + cols) // V7X_LANES) * V7X_LANES
    body = functools.partial(_bias_body, rows=rows, cols=cols, width=width, n_tab=n_tab,
                             idx_fn=idx_fn, group_fn=group_fn, mask_fn=mask_fn,
                             keys_on_rows=keys_on_rows)
    return pl.pallas_call(
        body,
        grid=(groups,),
        in_specs=[pl.BlockSpec(memory_space=pltpu.SMEM)],
        out_specs=pl.BlockSpec((1, rows, cols), lambda g: (g, 0, 0)),
        out_shape=jax.ShapeDtypeStruct((groups, rows, cols), F32),
        compiler_params=_cparams("arbitrary"),
        name="bias_tiles",
    )(table.T)


def _in_even_body(x_ref, sc_ref, sh_ref, g_ref, wa_ref, wh_ref, wl_ref, wb_ref,
                  q_ref, k_ref, v_ref, kb_ref, vb_ref, qi_ref, ki_ref, ki3_ref, wi_ref, zb_ref,
                  *, aw, keys_on_rows):
    bb, tt, D = x_ref.shape
    M = bb * tt
    h = _norm_mod(x_ref[...], g_ref[...].reshape(1, 1, D), sc_ref[...], sh_ref[...]).reshape(M, D)
    h_hi, h_lo = _split_bf16(h)
    za = _dot(h_hi, wa_ref[...])
    q_ref[...] = (za[:, :aw] * (DH ** -0.5)).astype(BF16).reshape(bb, tt, aw)
    k = za[:, aw:2 * aw]
    v = za[:, 2 * aw:3 * aw]
    k_ref[...] = k.reshape(bb, tt, aw)
    v_ref[...] = v.reshape(bb, tt, aw)
    kb_ref[...] = k.astype(BF16).reshape(bb, tt, aw)
    if keys_on_rows:
        vb_ref[0, 0] = v.T.astype(BF16)
    else:
        vb_ref[...] = v.astype(BF16).reshape(bb, tt, aw)
    wh = wh_ref[...]
    zi = _dot(h_hi, wh) + _dot(h_hi, wl_ref[...]) + _dot(h_lo, wh)
    iw = H_I * D_I
    ki = zi[:, iw:iw + D_I]
    qi_ref[...] = zi[:, :iw].reshape(bb, tt, iw)
    ki_ref[...] = ki.reshape(bb, tt, D_I)
    ki_hi = ki.astype(BF16).astype(F32)
    ki3 = jnp.concatenate([ki_hi, ki - ki_hi, ki_hi, jnp.zeros_like(ki)], axis=1)
    ki3_ref[...] = ki3.astype(BF16).reshape(bb, tt, 4 * D_I)
    wi_ref[...] = zi[:, iw + D_I:iw + D_I + H_I].reshape(bb, tt, H_I)
    zb_ref[...] = _dot(h_hi, wb_ref[...]).reshape(bb, tt, zb_ref.shape[-1])


def _in_even(x, scale, shift, g, wa, wh, wl, wb, keys_on_rows):
    B, T, D = x.shape
    aw = wa.shape[1] // 3
    bw = wb.shape[1]
    iw = H_I * D_I
    bb, tt = _row_tiles(B, T)
    assert not keys_on_rows or bb == 1
    row = lambda n: pl.BlockSpec((bb, tt, n), lambda b, t: (b, t, 0))
    mod = pl.BlockSpec((bb, 1, D), lambda b, t: (b, 0, 0))
    full = lambda a: pl.BlockSpec(a.shape, lambda b, t: (0,) * a.ndim)
    outs = [(aw, BF16), (aw, F32), (aw, F32), (aw, BF16), (aw, BF16), (iw, F32), (D_I, F32),
            (4 * D_I, BF16), (H_I, F32), (bw, F32)]
    out_specs = [row(n) for n, _ in outs]
    out_shape = [jax.ShapeDtypeStruct((B, T, n), dt) for n, dt in outs]
    if keys_on_rows:
        out_specs[4] = pl.BlockSpec((1, 1, aw, tt), lambda b, t: (b, t, 0, 0))
        out_shape[4] = jax.ShapeDtypeStruct((B, T // tt, aw, tt), BF16)
    return pl.pallas_call(
        functools.partial(_in_even_body, aw=aw, keys_on_rows=keys_on_rows),
        grid=(B // bb, T // tt),
        in_specs=[row(D), mod, mod, full(g), full(wa), full(wh), full(wl), full(wb)],
        out_specs=out_specs,
        out_shape=out_shape,
        compiler_params=_cparams("arbitrary", "arbitrary"),
        name="in_even",
    )(x, scale, shift, g, wa, wh, wl, wb)


def _in_odd_body(x_ref, sc_ref, sh_ref, g_ref, w_ref, q_ref, k_ref, v_ref, kb_ref, vb_ref):
    bb, tt, D = x_ref.shape
    M = bb * tt
    cw = q_ref.shape[-1]
    h = _norm_mod(x_ref[...], g_ref[...].reshape(1, 1, D), sc_ref[...], sh_ref[...]).reshape(M, D)
    z = _dot(h.astype(BF16), w_ref[...])
    q_ref[...] = (z[:, :cw] * (DH ** -0.5)).astype(BF16).reshape(bb, tt, cw)
    k = z[:, cw:2 * cw]
    v = z[:, 2 * cw:]
    k_ref[...] = k.reshape(bb, tt, cw)
    v_ref[...] = v.reshape(bb, tt, cw)
    kb_ref[...] = k.astype(BF16).reshape(bb, tt, cw)
    vb_ref[...] = v.astype(BF16).reshape(bb, tt, cw)


def _in_odd(x, scale, shift, g, w):
    B, T, D = x.shape
    cw = w.shape[1] // 3
    bb, tt = _row_tiles(B, T)
    row = lambda n: pl.BlockSpec((bb, tt, n), lambda b, t: (b, t, 0))
    mod = pl.BlockSpec((bb, 1, D), lambda b, t: (b, 0, 0))
    full = lambda a: pl.BlockSpec(a.shape, lambda b, t: (0,) * a.ndim)
    dts = [BF16, F32, F32, BF16, BF16]
    return pl.pallas_call(
        _in_odd_body,
        grid=(B // bb, T // tt),
        in_specs=[row(D), mod, mod, full(g), full(w)],
        out_specs=[row(cw) for _ in dts],
        out_shape=[jax.ShapeDtypeStruct((B, T, cw), dt) for dt in dts],
        compiler_params=_cparams("arbitrary", "arbitrary"),
        name="in_odd",
    )(x, scale, shift, g, w)


def _mix_ffn_body(*refs, n_mix, final):
    x_ref, g1_ref, sc2_ref, sh2_ref, g2_ref, gn_ref = refs[:6]
    mix_refs = refs[6:6 + n_mix]
    wmix_refs = refs[6 + n_mix:6 + 2 * n_mix]
    wg_ref, wu_ref, wd_ref = refs[6 + 2 * n_mix:9 + 2 * n_mix]
    gf_ref = refs[9 + 2 * n_mix] if final else None
    o_ref = refs[-1]
    bb, tt, D = x_ref.shape
    M = bb * tt
    m = None
    for mr, wr in zip(mix_refs, wmix_refs):
        part = _dot(mr[...].reshape(M, mr.shape[-1]), wr[...])
        m = part if m is None else m + part
    x1 = x_ref[...] + g1_ref[...] * m.reshape(bb, tt, D)
    h2 = _norm_mod(x1, gn_ref[...].reshape(1, 1, D), sc2_ref[...], sh2_ref[...])
    h2 = h2.reshape(M, D).astype(BF16)
    a = _dot(h2, wg_ref[...])
    b = _dot(h2, wu_ref[...])
    u = (_silu(a) * b).astype(BF16)
    f = _dot(u, wd_ref[...])
    x2 = x1 + g2_ref[...] * f.reshape(bb, tt, D)
    if final:
        ms = jnp.mean(x2 * x2, axis=-1, keepdims=True)
        x2 = x2 * lax.rsqrt(ms + EPS) * gf_ref[...].reshape(1, 1, D)
    o_ref[...] = x2


def _mix_ffn(x, g1, sc2, sh2, g2, gn, mixes, wmixes, wg, wu, wd, gf=None):
    B, T, D = x.shape
    bb, tt = _row_tiles(B, T)
    row = lambda n: pl.BlockSpec((bb, tt, n), lambda b, t: (b, t, 0))
    mod = pl.BlockSpec((bb, 1, D), lambda b, t: (b, 0, 0))
    full = lambda a: pl.BlockSpec(a.shape, lambda b, t: (0,) * a.ndim,
                                  pipeline_mode=pl.Buffered(1))
    final = gf is not None
    args = [x, g1, sc2, sh2, g2, gn, *mixes, *wmixes, wg, wu, wd] + ([gf] if final else [])
    specs = ([row(D), mod, mod, mod, mod, full(gn)] + [row(m.shape[-1]) for m in mixes]
             + [full(w) for w in wmixes] + [full(wg), full(wu), full(wd)]
             + ([full(gf)] if final else []))
    return pl.pallas_call(
        functools.partial(_mix_ffn_body, n_mix=len(mixes), final=final),
        grid=(B // bb, T // tt),
        in_specs=specs,
        out_specs=row(D),
        out_shape=jax.ShapeDtypeStruct((B, T, D), F32),
        compiler_params=_cparams("arbitrary", "arbitrary"),
        name="mix_ffn",
    )(*args)


def _hgrn_body(*refs, C, n_sub, e, has_s0):
    if has_s0:
        z_ref, lbl_ref, gn_ref, s0_ref, ob_ref, sout_ref, st_scr = refs
    else:
        z_ref, lbl_ref, gn_ref, ob_ref, sout_ref, st_scr = refs
    step = pl.program_id(1)
    n_heads = st_scr.shape[0]
    kw = n_heads * DK_B

    @pl.when(step == 0)
    def _():
        for h in range(n_heads):
            st_scr[h] = s0_ref[0, h].T if has_s0 else jnp.zeros((DV_B, DK_B), F32)

    ll = lbl_ref[...]
    ex = jnp.exp(ll - jnp.max(ll, axis=0, keepdims=True))
    sm = ex / jnp.sum(ex, axis=0, keepdims=True)
    lb = jnp.sum(sm[:e + 1], axis=0, keepdims=True)
    r = lax.broadcasted_iota(I32, (C, C), 0)
    c = lax.broadcasted_iota(I32, (C, C), 1)
    causal = r >= c
    tri = jnp.where(causal, 1.0, 0.0)
    gn = gn_ref[...]
    for sub in range(n_sub):
        z = z_ref[0, sub * C:(sub + 1) * C, :]
        qh = _silu(z[:, :kw])
        f = lb + (1.0 - lb) * jax.nn.sigmoid(z[:, kw:2 * kw])
        kh = 1.0 - f
        vh = z[:, 2 * kw:3 * kw]
        gate = _silu(z[:, 3 * kw:])
        G = jnp.dot(tri, jnp.log(f), precision=HIGHEST, preferred_element_type=F32)
        g_end = G[C - 1:C, :]
        mid = 0.5 * g_end
        q_in = qh * jnp.exp(G - mid)
        k_in = kh * jnp.exp(mid - G)
        q_st = qh * jnp.exp(G)
        k_st = kh * jnp.exp(g_end - G)
        dec = jnp.exp(g_end)
        outs = []
        for h in range(n_heads):
            sk = slice(h * DK_B, (h + 1) * DK_B)
            sv = slice(h * DV_B, (h + 1) * DV_B)
            att = _dot_nt(q_in[:, sk].astype(BF16), k_in[:, sk].astype(BF16))
            att = jnp.where(causal, att, 0.0)
            v_h = vh[:, sv]
            st = st_scr[h]
            o = (_dot(att.astype(BF16), v_h.astype(BF16))
                 + _dot_nt(q_st[:, sk].astype(BF16), st.astype(BF16)))
            st_scr[h] = st * dec[:, sk] + _dot(v_h.T.astype(BF16), k_st[:, sk].astype(BF16))
            o = o * lax.rsqrt(jnp.mean(o * o, axis=-1, keepdims=True) + EPS)
            outs.append(o)
        ob = jnp.concatenate(outs, axis=1) * gn * gate
        ob_ref[0, sub * C:(sub + 1) * C, :] = ob.astype(BF16)

    @pl.when(step == pl.num_programs(1) - 1)
    def _():
        for h in range(n_heads):
            sout_ref[0, h] = st_scr[h].T


def _hgrn(zb, lb_logits, g_norm, s0, e):
    B, T, zw = zb.shape
    kw = zw // 4
    n_heads = kw // DK_B
    C = min(CHUNK, T)
    ct = min(T, 4 * C)
    has_s0 = s0 is not None
    full = lambda a: pl.BlockSpec(a.shape, lambda b, t: (0,) * a.ndim)
    state = pl.BlockSpec((1, n_heads, DK_B, DV_B), lambda b, t: (b, 0, 0, 0))
    args = [zb, lb_logits, g_norm] + ([s0] if has_s0 else [])
    specs = ([pl.BlockSpec((1, ct, zw), lambda b, t: (b, t, 0)), full(lb_logits), full(g_norm)]
             + ([state] if has_s0 else []))
    return pl.pallas_call(
        functools.partial(_hgrn_body, C=C, n_sub=ct // C, e=e, has_s0=has_s0),
        grid=(B, T // ct),
        in_specs=specs,
        out_specs=[pl.BlockSpec((1, ct, kw), lambda b, t: (b, t, 0)), state],
        out_shape=[jax.ShapeDtypeStruct((B, T, kw), BF16),
                   jax.ShapeDtypeStruct((B, n_heads, DK_B, DV_B), F32)],
        scratch_shapes=[pltpu.VMEM((n_heads, DV_B, DK_B), F32)],
        compiler_params=_cparams("arbitrary", "arbitrary"),
        name="hgrn2",
    )(*args)


def _topk_threshold(count_ge, count_gt, count_tie_below, write_ties, rows, ksel, idx_bits):
    def thr_step(i, thr):
        cand = thr + lax.shift_left(I32(1), 31 - i)
        return jnp.where(count_ge(cand) >= ksel, cand, thr)

    thr = lax.fori_loop(0, 32, thr_step, jnp.full((rows, V7X_LANES), INT_MIN, I32))
    need = jnp.where(thr == INT_MIN, 0, ksel - count_gt(thr))
    write_ties(thr)

    def tie_step(i, end):
        cand = end + lax.shift_left(I32(1), idx_bits - 1 - i)
        return jnp.where(count_tie_below(cand) <= need, cand, end)

    tie_end = lax.fori_loop(0, idx_bits, tie_step, jnp.zeros((rows, V7X_LANES), I32))
    return thr, tie_end


def _lane_total(cnt):
    return jnp.broadcast_to(jnp.sum(cnt, axis=1, keepdims=True), cnt.shape)


I16_MIN = -(2**15)


def _dsa_prompt_t_body(q_ref, qi_ref, wi_ref, k_ref, vt_ref, ki3_ref, bias_ref, o_ref,
                       key_scr, hi_scr, lo_scr, tix_scr, neg_scr, qi3_scr, qz_scr,
                       m_scr, l_scr, acc_scr, s_scr, p_scr, *, tq, ksel, idx_bits):
    L = V7X_LANES
    n_heads = qz_scr.shape[0]
    qt = pl.program_id(1)
    nkb = qt + 1
    shift = CHUNK.bit_length() - 1
    row_i = lax.broadcasted_iota(I32, (tq, tq), 0)
    col_i = lax.broadcasted_iota(I32, (tq, tq), 1)
    t_chunk = (qt * tq + col_i) >> shift

    qi = qi_ref[0]
    qi_hi = qi.astype(BF16).astype(F32)
    qi_lo = qi - qi_hi
    zero = jnp.zeros((tq, D_I), F32)
    for h in range(H_I):
        sl = slice(h * D_I, (h + 1) * D_I)
        qi3_scr[h] = jnp.concatenate([qi_hi[:, sl], qi_hi[:, sl], qi_lo[:, sl], zero],
                                     axis=1).astype(BF16)
    w_t = (wi_ref[0] * (H_I ** -0.5 * D_I ** -0.5)).T

    def index_block(kb, carry):
        ks = pl.multiple_of(kb * tq, tq)
        ki3 = ki3_ref[0, pl.ds(ks, tq), :]
        score = None
        for h in range(H_I):
            term = w_t[h:h + 1, :] * jnp.maximum(_dot_nt(ki3, qi3_scr[h]), 0.0)
            score = term if score is None else score + term
        key = _sortable(score)
        s_chunk = (ks + row_i) >> shift
        key = jnp.where(s_chunk <= t_chunk, key, INT_MIN)
        key_scr[kb] = key
        hi_scr[kb] = (key >> 16).astype(jnp.int16)
        lo_scr[kb] = ((key & 0xFFFF) + I16_MIN).astype(jnp.int16)
        return carry

    lax.fori_loop(0, nkb, index_block, 0)

    def count(scr, pred, rows, dtype):
        n_acc = 4
        one, nil = jnp.ones((), dtype), jnp.zeros((), dtype)

        def body(kb, accs):
            accs = list(accs)
            for r in range(tq // rows):
                hit = pred(scr[kb, r * rows:(r + 1) * rows, :])
                accs[r % n_acc] = accs[r % n_acc] + jnp.where(hit, one, nil)
            return tuple(accs)
        accs = lax.fori_loop(0, nkb, body, tuple(jnp.zeros((rows, tq), dtype) for _ in range(n_acc)))
        tot = functools.reduce(lambda a, b: a + b, [a.astype(I32) for a in accs])
        return jnp.sum(tot, axis=0, keepdims=True)

    def count16(scr, pred):
        return count(scr, pred, 16, jnp.int16)

    def count32(scr, pred):
        return count(scr, pred, 8, I32)

    def search16(scr, target):
        def step(i, thr):
            cand = thr + lax.shift_left(I32(1), 15 - i)
            cand16 = jnp.broadcast_to(cand.astype(jnp.int16), (16, tq))
            return jnp.where(count16(scr, lambda x: x >= cand16) >= target, cand, thr)
        return lax.fori_loop(0, 16, step, jnp.full((1, tq), I16_MIN, I32))

    t_hi = search16(hi_scr, ksel)
    t_hi16 = jnp.broadcast_to(t_hi.astype(jnp.int16), (16, tq))
    above = count16(hi_scr, lambda x: x > t_hi16)

    def keep_bucket(kb, carry):
        for r in range(tq // 16):
            rows = slice(r * 16, (r + 1) * 16)
            lo_scr[kb, rows, :] = jnp.where(hi_scr[kb, rows, :] == t_hi16, lo_scr[kb, rows, :],
                                            jnp.int16(I16_MIN))
        return carry

    lax.fori_loop(0, nkb, keep_bucket, 0)
    t_lo = search16(lo_scr, ksel - above)
    thr = lax.shift_left(t_hi, 16) + (t_lo - I16_MIN)
    thr8 = jnp.broadcast_to(thr, (8, tq))
    n_ge = count32(key_scr, lambda x: x >= thr8)
    no_thr = thr == INT_MIN
    excess = jnp.where(jnp.logical_or(no_thr, n_ge <= ksel), 0, 1)

    def resolve_ties():
        need = jnp.where(no_thr, 0, ksel - count32(key_scr, lambda x: x > thr8))

        def write(kb, carry):
            tix_scr[kb] = jnp.where(key_scr[kb] == jnp.broadcast_to(thr, (tq, tq)),
                                    kb * tq + row_i, TIE_NONE)
            return carry

        lax.fori_loop(0, nkb, write, 0)

        def step(i, end):
            cand = end + lax.shift_left(I32(1), idx_bits - 1 - i)
            cand8 = jnp.broadcast_to(cand, (8, tq))
            return jnp.where(count32(tix_scr, lambda x: x < cand8) <= need, cand, end)

        return lax.fori_loop(0, idx_bits, step, jnp.zeros((1, tq), I32))

    tie_end = lax.cond(jnp.max(excess) > 0, resolve_ties,
                       lambda: jnp.where(no_thr, 0, TIE_NONE).astype(I32))

    def write_mask(kb, carry):
        key = key_scr[kb]
        thr_b = jnp.broadcast_to(thr, (tq, tq))
        tied = jnp.where(key == thr_b, kb * tq + row_i, TIE_NONE) < jnp.broadcast_to(tie_end, (tq, tq))
        neg_scr[kb] = jnp.where(key > thr_b, 0.0, jnp.where(tied, 0.0, -jnp.inf))
        return carry

    lax.fori_loop(0, nkb, write_mask, 0)

    lane = lax.broadcasted_iota(I32, (tq, L), 1)
    q = q_ref[0].astype(F32)
    for h in range(n_heads):
        qp = q[:, (h // 2) * L:(h // 2 + 1) * L]
        keep = (lane < DH) if h % 2 == 0 else (lane >= DH)
        qz_scr[h] = jnp.where(keep, qp, 0.0).astype(BF16)
        m_scr[h] = jnp.full((1, tq), -1e30, F32)
        l_scr[h] = jnp.zeros((1, tq), F32)
        acc_scr[h] = jnp.zeros((DH, tq), F32)

    rc = 32
    n_rc = tq // rc
    n_slots = s_scr.shape[0]

    def attend_block(kb, carry):
        ks = pl.multiple_of(kb * tq, tq)
        d = qt - kb

        def scores(h):
            kp = k_ref[0, pl.ds(ks, tq), (h // 2) * L:(h // 2 + 1) * L]
            s_scr[h % n_slots] = _dot_nt(kp, qz_scr[h])

        for h in range(n_slots - 1):
            scores(h)
        for h in range(n_heads):
            if h + n_slots - 1 < n_heads:
                scores(h + n_slots - 1)
            slot = h % n_slots
            mx = None
            for r in range(n_rc):
                rows = slice(r * rc, (r + 1) * rc)
                x = s_scr[slot, rows, :] + bias_ref[d, h, rows, :] + neg_scr[kb, rows, :]
                s_scr[slot, rows, :] = x
                xm = jnp.max(x.reshape(rc // 8, 8, tq), axis=0)
                mx = xm if mx is None else jnp.maximum(mx, xm)
            m_old = m_scr[h]
            m_new = jnp.maximum(m_old, jnp.max(mx, axis=0, keepdims=True))
            alpha = jnp.exp(m_old - m_new)
            m_scr[h] = m_new
            m_b = jnp.broadcast_to(m_new, (rc, tq))
            ps = None
            for r in range(n_rc):
                rows = slice(r * rc, (r + 1) * rc)
                p = jnp.exp(s_scr[slot, rows, :] - m_b)
                p_scr[slot, rows, :] = p.astype(BF16)
                pr = jnp.sum(p.reshape(rc // 8, 8, tq), axis=0)
                ps = pr if ps is None else ps + pr
            l_scr[h] = alpha * l_scr[h] + jnp.sum(ps, axis=0, keepdims=True)
            vt = vt_ref[0, kb, h * DH:(h + 1) * DH, :]
            acc_scr[h] = acc_scr[h] * alpha + _dot(vt, p_scr[slot])
        return carry

    lax.fori_loop(0, nkb, attend_block, 0)
    for j in range(n_heads // 2):
        o_t = jnp.concatenate([acc_scr[2 * j] / l_scr[2 * j],
                               acc_scr[2 * j + 1] / l_scr[2 * j + 1]], axis=0)
        o_ref[0, :, j * L:(j + 1) * L] = o_t.T.astype(BF16)


def _dsa_prompt_t(q, qi, wi, kb16, vt16, ki3, bias_t, ksel):
    B, T, aw = q.shape
    n_heads = aw // DH
    tq = bias_t.shape[-1]
    nq = T // tq
    assert T < 2**15 and vt16.shape == (B, nq, aw, tq)
    row = lambda n: pl.BlockSpec((1, tq, n), lambda b, t: (b, t, 0))
    seq = lambda n: pl.BlockSpec((1, T, n), lambda b, t: (b, 0, 0))
    return pl.pallas_call(
        functools.partial(_dsa_prompt_t_body, tq=tq, ksel=ksel, idx_bits=T.bit_length()),
        grid=(B, nq),
        in_specs=[row(aw), row(H_I * D_I), row(H_I), seq(aw),
                  pl.BlockSpec((1, nq, aw, tq), lambda b, t: (b, 0, 0, 0)), seq(4 * D_I),
                  pl.BlockSpec(bias_t.shape, lambda b, t: (0, 0, 0, 0),
                               pipeline_mode=pl.Buffered(1))],
        out_specs=row(aw),
        out_shape=jax.ShapeDtypeStruct((B, T, aw), BF16),
        scratch_shapes=[pltpu.VMEM((nq, tq, tq), I32), pltpu.VMEM((nq, tq, tq), jnp.int16),
                        pltpu.VMEM((nq, tq, tq), jnp.int16), pltpu.VMEM((nq, tq, tq), I32),
                        pltpu.VMEM((nq, tq, tq), F32), pltpu.VMEM((H_I, tq, 4 * D_I), BF16),
                        pltpu.VMEM((n_heads, tq, V7X_LANES), BF16),
                        pltpu.VMEM((n_heads, 1, tq), F32), pltpu.VMEM((n_heads, 1, tq), F32),
                        pltpu.VMEM((n_heads, DH, tq), F32),
                        pltpu.VMEM((4, tq, tq), F32), pltpu.VMEM((4, tq, tq), BF16)],
        compiler_params=_cparams("arbitrary", "arbitrary"),
        name="dsa_prompt",
    )(q, qi, wi, kb16, vt16, ki3, bias_t)


def _pack_heads(q, n_heads):
    T, W = q.shape
    head_of_lane = lax.broadcasted_iota(I32, (T, W), 1) >> (DH.bit_length() - 1)
    qf = q.astype(F32)
    return jnp.concatenate([jnp.where(head_of_lane == h, qf, 0.0) for h in range(n_heads)],
                           axis=0).astype(BF16)


def _unpack_heads(acc, den, n_heads):
    T = acc.shape[0] // n_heads
    W = acc.shape[1]
    head_of_lane = lax.broadcasted_iota(I32, (T, W), 1) >> (DH.bit_length() - 1)
    out = jnp.zeros((T, W), F32)
    for h in range(n_heads):
        blk = acc[h * T:(h + 1) * T, :] / den[h * T:(h + 1) * T, :]
        out = jnp.where(head_of_lane == h, blk, out)
    return out


def _dsa_sample_body(q_ref, qi_ref, wi_ref, kn_ref, vn_ref, kin_ref, kc_ref, vc_ref, kic_ref,
                     bias_ref, biasn_ref, o_ref,
                     key_scr, tix_scr, keyn_scr, tixn_scr, sel_scr, qbd_scr, m_scr, l_scr, acc_scr,
                     *, ts, kc_len, n_new, ksel, idx_bits):
    L = V7X_LANES
    n_heads = qbd_scr.shape[0] // ts
    nc = key_scr.shape[0]
    step = pl.program_id(1)
    past = nc * kc_len
    lane = lax.broadcasted_iota(I32, (ts, L), 1)

    @pl.when(step == 0)
    def _():
        qi = qi_ref[0]
        qi_hm = jnp.concatenate([qi[:, h * D_I:(h + 1) * D_I] for h in range(H_I)], axis=0)
        qi_hi, qi_lo = _split_bf16(qi_hm)
        w = wi_ref[0] * (H_I ** -0.5 * D_I ** -0.5)

        def scores(ki):
            ki_hi, ki_lo = _split_bf16(ki)
            s = _dot_nt_3pass(qi_hi, qi_lo, ki_hi, ki_lo)
            acc = None
            for h in range(H_I):
                term = w[:, h:h + 1] * jnp.maximum(s[h * ts:(h + 1) * ts, :], 0.0)
                acc = term if acc is None else acc + term
            return _sortable(acc)

        for c in range(nc):
            key_scr[c] = scores(kic_ref[0, c * kc_len:(c + 1) * kc_len, :])
        keyn_scr[...] = jnp.where(lane < n_new, scores(kin_ref[0]), INT_MIN)

        def count(pred_key=None, pred_tie=None):
            cnt = jnp.zeros((ts, L), I32)
            for c in range(nc):
                for i in range(kc_len // L):
                    if pred_key is not None:
                        hit = pred_key(key_scr[c, :, i * L:(i + 1) * L])
                    else:
                        hit = pred_tie(tix_scr[c, :, i * L:(i + 1) * L])
                    cnt = cnt + jnp.where(hit, 1, 0)
            hit = pred_key(keyn_scr[...]) if pred_key is not None else pred_tie(tixn_scr[...])
            return _lane_total(cnt + jnp.where(hit, 1, 0))

        def write_ties(thr):
            for c in range(nc):
                for i in range(kc_len // L):
                    idx = c * kc_len + i * L + lane
                    tix_scr[c, :, i * L:(i + 1) * L] = jnp.where(
                        key_scr[c, :, i * L:(i + 1) * L] == thr, idx, TIE_NONE)
            tixn_scr[...] = jnp.where(keyn_scr[...] == thr, past + lane, TIE_NONE)

        thr, tie_end = _topk_threshold(
            lambda x: count(pred_key=lambda k: k >= x),
            lambda x: count(pred_key=lambda k: k > x),
            lambda x: count(pred_tie=lambda t: t < x),
            write_ties, ts, ksel, idx_bits)
        sel_scr[0] = thr
        sel_scr[1] = tie_end
        qbd_scr[...] = _pack_heads(q_ref[0], n_heads)
        m_scr[...] = jnp.full(m_scr.shape, -1e30, F32)
        l_scr[...] = jnp.zeros(l_scr.shape, F32)
        acc_scr[...] = jnp.zeros(acc_scr.shape, F32)

    thr = sel_scr[0]
    tie_end = sel_scr[1]

    def attend(keys, ties, k16, v16, bias):
        n = keys.shape[1]
        negs = []
        for i in range(n // L):
            sel = jnp.logical_or(keys[:, i * L:(i + 1) * L] > thr,
                                 ties[:, i * L:(i + 1) * L] < tie_end)
            negs.append(jnp.where(sel, 0.0, -jnp.inf))
        neg = jnp.concatenate(negs, axis=1) if len(negs) > 1 else negs[0]
        neg = jnp.concatenate([neg] * n_heads, axis=0)
        s = _dot_nt(qbd_scr[...], k16) + bias + neg
        m_old = m_scr[...]
        m_new = jnp.maximum(m_old, jnp.max(s, axis=1, keepdims=True))
        alpha = jnp.exp(m_old - m_new)
        p = jnp.exp(s - m_new)
        l_scr[...] = alpha * l_scr[...] + jnp.sum(p, axis=1, keepdims=True)
        m_scr[...] = m_new
        acc_scr[...] = acc_scr[...] * alpha + _dot(p.astype(BF16), v16)

    attend(key_scr[step], tix_scr[step], kc_ref[0].astype(BF16), vc_ref[0].astype(BF16),
           bias_ref[...])

    @pl.when(step == nc - 1)
    def _():
        attend(keyn_scr[...], tixn_scr[...], kn_ref[0], vn_ref[0], biasn_ref[...])
        o_ref[0] = _unpack_heads(acc_scr[...], l_scr[...], n_heads).astype(BF16)


def _dsa_sample(q, qi, wi, kn16, vn16, kin, cache_k, cache_v, cache_ki, bias, n_new, ksel):
    B, ts, aw = q.shape
    n_heads = aw // DH
    P = cache_k.shape[1]
    npad = kn16.shape[1]
    kc_len = min(P, 1024)
    nc = P // kc_len
    L = V7X_LANES
    per_b = lambda a: pl.BlockSpec((1,) + a.shape[1:], lambda b, c: (b, 0, 0))
    chunk = lambda n: pl.BlockSpec((1, kc_len, n), lambda b, c: (b, c, 0))
    rows = n_heads * ts
    return pl.pallas_call(
        functools.partial(_dsa_sample_body, ts=ts, kc_len=kc_len, n_new=n_new, ksel=ksel,
                          idx_bits=(P + npad).bit_length()),
        grid=(B, nc),
        in_specs=[per_b(q), per_b(qi), per_b(wi), per_b(kn16), per_b(vn16), per_b(kin),
                  chunk(aw), chunk(aw), per_b(cache_ki),
                  pl.BlockSpec((rows, kc_len), lambda b, c: (0, c)),
                  pl.BlockSpec((rows, npad), lambda b, c: (0, P // npad))],
        out_specs=pl.BlockSpec((1, ts, aw), lambda b, c: (b, 0, 0)),
        out_shape=jax.ShapeDtypeStruct((B, ts, aw), BF16),
        scratch_shapes=[pltpu.VMEM((nc, ts, kc_len), I32), pltpu.VMEM((nc, ts, kc_len), I32),
                        pltpu.VMEM((ts, npad), I32), pltpu.VMEM((ts, npad), I32),
                        pltpu.VMEM((2, ts, L), I32), pltpu.VMEM((rows, aw), BF16),
                        pltpu.VMEM((rows, 1), F32), pltpu.VMEM((rows, 1), F32),
                        pltpu.VMEM((rows, aw), F32)],
        compiler_params=_cparams("arbitrary", "arbitrary"),
        name="dsa_sample",
    )(q, qi, wi, kn16, vn16, kin, cache_k, cache_v, cache_ki, bias, bias)


def _band_prompt_body(q_ref, k_ref, v_ref, bias_ref, o_ref, *, tq, n_back):
    L = V7X_LANES
    n_heads = q_ref.shape[-1] // DH
    qt = pl.program_id(1)
    lane = lax.broadcasted_iota(I32, (tq, L), 1)
    low_half = lane < DH
    starts, pads = [], []
    for j in range(n_back + 1):
        kb = qt - n_back + j
        starts.append(pl.multiple_of(jnp.maximum(kb, 0) * tq, tq))
        pads.append(jnp.where(kb >= 0, 0.0, -jnp.inf))
    for pair in range(n_heads // 2):
        qp = q_ref[0, :, pair * L:(pair + 1) * L].astype(F32)
        outs, dens = [], []
        for h in (2 * pair, 2 * pair + 1):
            keep = low_half if h % 2 == 0 else jnp.logical_not(low_half)
            qz = jnp.where(keep, qp, 0.0).astype(BF16)
            ss = []
            for j in range(n_back + 1):
                kp = k_ref[0, pl.ds(starts[j], tq), pair * L:(pair + 1) * L]
                ss.append(_dot_nt(qz, kp) + (bias_ref[h, j] + pads[j]))
            m = functools.reduce(jnp.maximum, [jnp.max(s, axis=1, keepdims=True) for s in ss])
            den, out = None, None
            for j in range(n_back + 1):
                p = jnp.exp(ss[j] - m)
                vp = v_ref[0, pl.ds(starts[j], tq), pair * L:(pair + 1) * L]
                dj = jnp.sum(p, axis=1, keepdims=True)
                oj = _dot(p.astype(BF16), vp)
                den = dj if den is None else den + dj
                out = oj if out is None else out + oj
            outs.append(out)
            dens.append(jnp.broadcast_to(den, (tq, L)))
        o_ref[0, :, pair * L:(pair + 1) * L] = (
            jnp.where(low_half, outs[0], outs[1]) / jnp.where(low_half, dens[0], dens[1])
        ).astype(BF16)


def _band_prompt(q, kb16, vb16, bias):
    B, T, cw = q.shape
    tq = bias.shape[-1]
    n_back = bias.shape[1] - 1
    row = pl.BlockSpec((1, tq, cw), lambda b, t: (b, t, 0))
    seq = pl.BlockSpec((1, T, cw), lambda b, t: (b, 0, 0))
    return pl.pallas_call(
        functools.partial(_band_prompt_body, tq=tq, n_back=n_back),
        grid=(B, T // tq),
        in_specs=[row, seq, seq,
                  pl.BlockSpec(bias.shape, lambda b, t: (0, 0, 0, 0), pipeline_mode=pl.Buffered(1))],
        out_specs=row,
        out_shape=jax.ShapeDtypeStruct((B, T, cw), BF16),
        compiler_params=_cparams("arbitrary", "arbitrary"),
        name="band_prompt",
    )(q, kb16, vb16, bias)


def _band_sample_body(q_ref, kn_ref, vn_ref, kc_ref, vc_ref, bias_ref, biasn_ref, o_ref):
    n_heads = q_ref.shape[-1] // DH
    qbd = _pack_heads(q_ref[0], n_heads)
    s_c = _dot_nt(qbd, kc_ref[0].astype(BF16)) + bias_ref[...]
    s_n = _dot_nt(qbd, kn_ref[0]) + biasn_ref[...]
    m = jnp.maximum(jnp.max(s_c, axis=1, keepdims=True), jnp.max(s_n, axis=1, keepdims=True))
    p_c = jnp.exp(s_c - m)
    p_n = jnp.exp(s_n - m)
    den = jnp.sum(p_c, axis=1, keepdims=True) + jnp.sum(p_n, axis=1, keepdims=True)
    acc = _dot(p_c.astype(BF16), vc_ref[0].astype(BF16)) + _dot(p_n.astype(BF16), vn_ref[0])
    o_ref[0] = _unpack_heads(acc, den, n_heads).astype(BF16)


def _band_sample(q, kn16, vn16, cache_k, cache_v, bias):
    B, ts, cw = q.shape
    n_heads = cw // DH
    cb = cache_k.shape[1]
    npad = kn16.shape[1]
    rows = n_heads * ts
    per_b = lambda a: pl.BlockSpec((1,) + a.shape[1:], lambda b: (b, 0, 0))
    return pl.pallas_call(
        _band_sample_body,
        grid=(B,),
        in_specs=[per_b(q), per_b(kn16), per_b(vn16), per_b(cache_k), per_b(cache_v),
                  pl.BlockSpec((rows, cb), lambda b: (0, 0)),
                  pl.BlockSpec((rows, npad), lambda b: (0, cb // npad))],
        out_specs=per_b(q),
        out_shape=jax.ShapeDtypeStruct((B, ts, cw), BF16),
        compiler_params=_cparams("arbitrary"),
        name="band_sample",
    )(q, kn16, vn16, cache_k, cache_v, bias, bias)


def _pad_rows(a, n):
    return jnp.pad(a, ((0, 0), (0, n - a.shape[1]), (0, 0)))


def kernel(x_prompt, x_sample, c_prompt, c_sample, cache_a_k, cache_a_v, cache_a_kidx, state_b,
           cache_c_k, cache_c_v, w_ada, b_ada, norm_mix, norm_ffn, norm_final, w_in_even,
           w_out_even, t5_table, lb_logits, norm_b_out, w_in_odd, w_out_odd, rel_table_c,
           w_ffn_gate, w_ffn_up, w_ffn_down):
    Bp, T, D = x_prompt.shape
    Bs, Ts, _ = x_sample.shape
    P = cache_a_k.shape[2]
    c_buf = cache_c_k.shape[2]
    depth = w_ada.shape[0]
    L = V7X_LANES
    n_heads_a = cache_a_k.shape[3]
    aw = n_heads_a * DH
    n_heads_c = cache_c_k.shape[3]
    bw = w_out_even.shape[1] - aw
    iw = H_I * D_I
    ksel_p = min(K_TOP, T // 4)
    ksel_s = min(K_TOP, (P + Ts) // 4)
    c_keep = min(C_BACK * CHUNK, T)
    tq = min(T, 256)
    npad = L
    assert Ts <= npad and P % npad == 0 and c_buf % npad == 0 and T % tq == 0

    mod = _adaln_all(jnp.concatenate([c_prompt, c_sample], axis=0), w_ada, b_ada)

    def mods(l):
        parts = jnp.split(mod[l], 6, axis=-1)
        return ([p[:Bp].reshape(Bp, 1, D) for p in parts],
                [p[Bp:].reshape(Bs, 1, D) for p in parts])

    nq = T // tq
    t5_prompt = _bias_tiles(
        t5_table, groups=nq * n_heads_a, rows=tq, cols=tq, idx_fn=_t5_bucket,
        group_fn=lambda g: (g % n_heads_a, -(g // n_heads_a) * tq), keys_on_rows=True,
    ).reshape(nq, n_heads_a, tq, tq)
    t5_sample = _bias_tiles(
        t5_table, groups=n_heads_a, rows=Ts, cols=P + npad, idx_fn=_t5_bucket,
        group_fn=lambda g: (g, -P),
    ).reshape(n_heads_a * Ts, P + npad)

    xp, xs = x_prompt, x_sample
    outs_p = {k: [] for k in ("ak", "av", "ai", "bs", "ck", "cv")}
    outs_s = {k: [] for k in ("ak", "av", "ai", "bs", "ck", "cv")}
    for l in range(depth):
        (sh1p, sc1p, g1p, sh2p, sc2p, g2p), (sh1s, sc1s, g1s, sh2s, sc2s, g2s) = mods(l)
        gm = norm_mix[l].reshape(1, D)
        gn = norm_ffn[l].reshape(1, D)
        wg = w_ffn_gate[l].astype(BF16)
        wu = w_ffn_up[l].astype(BF16)
        wd = w_ffn_down[l].astype(BF16)
        gf = norm_final.reshape(1, D) if l == depth - 1 else None
        if l % 2 == 0:
            e = l // 2
            w_in = w_in_even[e]
            wa = w_in[:, :3 * aw].astype(BF16)
            wi_cols = w_in[:, 3 * aw:3 * aw + iw + D_I + H_I]
            wi_cols = jnp.pad(wi_cols, ((0, 0), (0, -wi_cols.shape[1] % L)))
            wh = wi_cols.astype(BF16)
            wl = (wi_cols - wh.astype(F32)).astype(BF16)
            wb = w_in[:, 3 * aw + iw + D_I + H_I:].astype(BF16)
            w_out = w_out_even[e].astype(BF16)
            gb_norm = norm_b_out[e].reshape(1, bw)

            q, k, v, k16, vt16, qi, ki, ki3, wi, zb = _in_even(xp, sc1p, sh1p, gm, wa, wh, wl, wb,
                                                              keys_on_rows=True)
            oa = _dsa_prompt_t(q, qi, wi, k16, vt16, ki3, t5_prompt, ksel_p)
            ob, st = _hgrn(zb, lb_logits, gb_norm, None, e)
            outs_p["ak"].append(k.reshape(Bp, T, n_heads_a, DH))
            outs_p["av"].append(v.reshape(Bp, T, n_heads_a, DH))
            outs_p["ai"].append(ki)
            outs_p["bs"].append(st)
            xp = _mix_ffn(xp, g1p, sc2p, sh2p, g2p, gn, [oa, ob], [w_out[:aw], w_out[aw:]],
                          wg, wu, wd, gf)

            q, k, v, k16, v16, qi, ki, _, wi, zb = _in_even(xs, sc1s, sh1s, gm, wa, wh, wl, wb,
                                                            keys_on_rows=False)
            oa = _dsa_sample(q, qi, wi, _pad_rows(k16, npad), _pad_rows(v16, npad),
                             _pad_rows(ki, npad), cache_a_k[e].reshape(Bs, P, aw),
                             cache_a_v[e].reshape(Bs, P, aw), cache_a_kidx[e], t5_sample,
                             Ts, ksel_s)
            ob, st = _hgrn(zb, lb_logits, gb_norm, state_b[e], e)
            outs_s["ak"].append(k.reshape(Bs, Ts, n_heads_a, DH))
            outs_s["av"].append(v.reshape(Bs, Ts, n_heads_a, DH))
            outs_s["ai"].append(ki)
            outs_s["bs"].append(st)
            xs = _mix_ffn(xs, g1s, sc2s, sh2s, g2s, gn, [oa, ob], [w_out[:aw], w_out[aw:]],
                          wg, wu, wd, gf)
        else:
            o = l // 2
            cw = n_heads_c * DH
            w_in = w_in_odd[o].astype(BF16)
            w_out = w_out_odd[o].astype(BF16)
            n_back = -(-(C_BACK * CHUNK) // tq)
            shift = CHUNK.bit_length() - 1

            def band_mask(g, t, s, n_back=n_back):
                j = g % (n_back + 1)
                kc = (s + (j - n_back) * tq) >> shift
                tc = t >> shift
                return jnp.logical_and(kc <= tc, kc >= tc - C_BACK)

            band_p = _bias_tiles(
                rel_table_c[o], groups=n_heads_c * (n_back + 1), rows=tq, cols=tq,
                idx_fn=_clip_index,
                group_fn=lambda g, n_back=n_back: (g // (n_back + 1),
                                                   (g % (n_back + 1) - n_back) * tq),
                mask_fn=band_mask,
            ).reshape(n_heads_c, n_back + 1, tq, tq)
            band_s = _bias_tiles(
                rel_table_c[o], groups=n_heads_c, rows=Ts, cols=c_buf + npad, idx_fn=_clip_index,
                group_fn=lambda g: (g, -c_buf),
                mask_fn=lambda g, t, s: s < c_buf + Ts,
            ).reshape(n_heads_c * Ts, c_buf + npad)

            q, k, v, k16, v16 = _in_odd(xp, sc1p, sh1p, gm, w_in)
            op = _band_prompt(q, k16, v16, band_p)
            outs_p["ck"].append(k[:, T - c_keep:].reshape(Bp, c_keep, n_heads_c, DH))
            outs_p["cv"].append(v[:, T - c_keep:].reshape(Bp, c_keep, n_heads_c, DH))
            xp = _mix_ffn(xp, g1p, sc2p, sh2p, g2p, gn, [op], [w_out], wg, wu, wd, gf)

            q, k, v, k16, v16 = _in_odd(xs, sc1s, sh1s, gm, w_in)
            osm = _band_sample(q, _pad_rows(k16, npad), _pad_rows(v16, npad),
                               cache_c_k[o].reshape(Bs, c_buf, cw),
                               cache_c_v[o].reshape(Bs, c_buf, cw), band_s)
            outs_s["ck"].append(k.reshape(Bs, Ts, n_heads_c, DH))
            outs_s["cv"].append(v.reshape(Bs, Ts, n_heads_c, DH))
            xs = _mix_ffn(xs, g1s, sc2s, sh2s, g2s, gn, [osm], [w_out], wg, wu, wd, gf)

    st = lambda xs_: jnp.stack(xs_)
    return (xp, xs, st(outs_p["ak"]), st(outs_p["av"]), st(outs_p["ai"]), st(outs_p["bs"]),
            st(outs_p["ck"]), st(outs_p["cv"]), st(outs_s["ak"]), st(outs_s["av"]),
            st(outs_s["ai"]), st(outs_s["bs"]), st(outs_s["ck"]), st(outs_s["cv"]))
```

```python
import functools
import math

import jax
import jax.numpy as jnp
from jax import lax
from jax.experimental import pallas as pl
from jax.experimental.pallas import tpu as pltpu

F32, BF16, I32 = jnp.float32, jnp.bfloat16, jnp.int32

CHUNK = 64
EPS = 1e-6
DH = 64
H_I = 8
D_I = 64
K_TOP = 256
N_BUCKETS = 32
T5_MAX_DIST = 1024
DK_B = 128
DV_B = 128
C_BACK = 8
REL_CLIP = 128

V7X_LANES = 128
V7X_VMEM_LIMIT = 56 * 2**20

INT_MIN = -(2**31)
TIE_NONE = 2**30
HIGHEST = lax.Precision.HIGHEST


def _cparams(*sem):
    return pltpu.CompilerParams(dimension_semantics=sem, vmem_limit_bytes=V7X_VMEM_LIMIT)


def _dot(a, b):
    return jnp.dot(a, b, preferred_element_type=F32)


def _dot_nt(a, b):
    return lax.dot_general(a, b, (((1,), (1,)), ((), ())), preferred_element_type=F32)


def _split_bf16(x):
    hi = x.astype(BF16)
    lo = (x - hi.astype(F32)).astype(BF16)
    return hi, lo


def _dot_nt_3pass(a_hi, a_lo, b_hi, b_lo):
    return _dot_nt(a_hi, b_hi) + _dot_nt(a_hi, b_lo) + _dot_nt(a_lo, b_hi)


def _silu(x):
    return x * jax.nn.sigmoid(x)


def _norm_mod(x, g, scale, shift):
    ms = jnp.mean(x * x, axis=-1, keepdims=True)
    return (x * lax.rsqrt(ms + EPS) * g) * (1.0 + scale) + shift


def _sortable(x):
    b = pltpu.bitcast(x + 0.0, I32)
    return b ^ ((b >> 31) & I32(0x7FFFFFFF))


LOG2E = math.log2(math.e)


def _q_scale(base2):
    return DH ** -0.5 * (LOG2E if base2 else 1.0)


def _row_tiles(B, T, rows=256):
    tt = min(T, rows)
    bb = max(1, min(B, rows // tt))
    assert T % tt == 0 and B % bb == 0
    return bb, tt


def _ada_body(c_ref, w_ref, b_ref, o_ref):
    s = _silu(c_ref[...])
    o_ref[0] = jnp.dot(s, w_ref[0], precision=HIGHEST, preferred_element_type=F32) + b_ref[0]


def _adaln_all(c_all, w_ada, b_ada):
    depth, D, E = w_ada.shape
    R = c_all.shape[0]
    tn = 1024
    return pl.pallas_call(
        _ada_body,
        grid=(depth, E // tn),
        in_specs=[pl.BlockSpec((R, D), lambda l, n: (0, 0)),
                  pl.BlockSpec((1, D, tn), lambda l, n: (l, 0, n)),
                  pl.BlockSpec((1, 1, tn), lambda l, n: (l, 0, n))],
        out_specs=pl.BlockSpec((1, R, tn), lambda l, n: (l, 0, n)),
        out_shape=jax.ShapeDtypeStruct((depth, R, E), F32),
        compiler_params=_cparams("arbitrary", "arbitrary"),
        name="adaln",
    )(c_all, w_ada, b_ada.reshape(depth, 1, E))


def _t5_bucket(rel):
    nb = N_BUCKETS // 2
    max_exact = nb // 2
    base = jnp.where(rel > 0, nb, 0)
    n = jnp.abs(rel)
    large = max_exact + (jnp.log(jnp.maximum(n, 1).astype(F32) / max_exact)
                         / math.log(T5_MAX_DIST / max_exact) * (nb - max_exact)).astype(I32)
    large = jnp.minimum(large, nb - 1)
    return base + jnp.where(n < max_exact, n, large)


def _clip_index(rel):
    return jnp.clip(-rel, -REL_CLIP, REL_CLIP) + REL_CLIP


def _bias_body(tab_ref, o_ref, *, rows, cols, width, n_tab, idx_fn, group_fn, mask_fn,
               keys_on_rows):
    g = pl.program_id(0)
    head, off = group_fn(g)
    i = lax.broadcasted_iota(I32, (1, width), 1)
    diff = jnp.where(i < width - rows, i, i - width)
    idx = idx_fn(off - diff if keys_on_rows else off + diff)
    val = jnp.zeros((1, width), F32)
    for b in range(n_tab):
        val = jnp.where(idx == b, tab_ref[head, b], val)
    x = jnp.broadcast_to(val, (rows, width))
    y = pltpu.roll(x, 0, 1, stride=1, stride_axis=0)
    tile = y[:, :cols]
    if keys_on_rows:
        tile = tile * LOG2E
    if mask_fn is not None:
        t = lax.broadcasted_iota(I32, (rows, cols), 1 if keys_on_rows else 0)
        s = lax.broadcasted_iota(I32, (rows, cols), 0 if keys_on_rows else 1)
        tile = jnp.where(mask_fn(g, t, s), tile, -jnp.inf)
    o_ref[0] = tile


def _bias_tiles(table, *, groups, rows, cols, idx_fn, group_fn, mask_fn=None,
                keys_on_rows=False):
    n_tab = table.shape[0]
    width = -(-(rows + cols) // V7X_LANES) * V7X_LANES
    body = functools.partial(_bias_body, rows=rows, cols=cols, width=width, n_tab=n_tab,
                             idx_fn=idx_fn, group_fn=group_fn, mask_fn=mask_fn,
                             keys_on_rows=keys_on_rows)
    return pl.pallas_call(
        body,
        grid=(groups,),
        in_specs=[pl.BlockSpec(memory_space=pltpu.SMEM)],
        out_specs=pl.BlockSpec((1, rows, cols), lambda g: (g, 0, 0)),
        out_shape=jax.ShapeDtypeStruct((groups, rows, cols), F32),
        compiler_params=_cparams("arbitrary"),
        name="bias_tiles",
    )(table.T)


def _in_even_body(x_ref, sc_ref, sh_ref, g_ref, wa_ref, wh_ref, wl_ref, wb_ref,
                  q_ref, k_ref, v_ref, kb_ref, vb_ref, qi_ref, ki_ref, ki3_ref, wi_ref, zb_ref,
                  *, aw, keys_on_rows):
    bb, tt, D = x_ref.shape
    M = bb * tt
    h = _norm_mod(x_ref[...], g_ref[...].reshape(1, 1, D), sc_ref[...], sh_ref[...]).reshape(M, D)
    h_hi, h_lo = _split_bf16(h)
    za = _dot(h_hi, wa_ref[...])
    q_ref[...] = (za[:, :aw] * _q_scale(keys_on_rows)).astype(BF16).reshape(bb, tt, aw)
    k = za[:, aw:2 * aw]
    v = za[:, 2 * aw:3 * aw]
    k_ref[...] = k.reshape(bb, tt, aw)
    v_ref[...] = v.reshape(bb, tt, aw)
    kb_ref[...] = k.astype(BF16).reshape(bb, tt, aw)
    if keys_on_rows:
        vb_ref[0, 0] = v.T.astype(BF16)
    else:
        vb_ref[...] = v.astype(BF16).reshape(bb, tt, aw)
    wh = wh_ref[...]
    zi = _dot(h_hi, wh) + _dot(h_hi, wl_ref[...]) + _dot(h_lo, wh)
    iw = H_I * D_I
    ki = zi[:, iw:iw + D_I]
    qi_ref[...] = zi[:, :iw].reshape(bb, tt, iw)
    ki_ref[...] = ki.reshape(bb, tt, D_I)
    ki_hi = ki.astype(BF16).astype(F32)
    ki3 = jnp.concatenate([ki_hi, ki - ki_hi, ki_hi, jnp.zeros_like(ki)], axis=1)
    ki3_ref[...] = ki3.astype(BF16).reshape(bb, tt, 4 * D_I)
    wi_ref[...] = zi[:, iw + D_I:iw + D_I + H_I].reshape(bb, tt, H_I)
    zb_ref[...] = _dot(h_hi, wb_ref[...]).reshape(bb, tt, zb_ref.shape[-1])


def _in_even(x, scale, shift, g, wa, wh, wl, wb, keys_on_rows):
    B, T, D = x.shape
    aw = wa.shape[1] // 3
    bw = wb.shape[1]
    iw = H_I * D_I
    bb, tt = _row_tiles(B, T)
    assert not keys_on_rows or bb == 1
    row = lambda n: pl.BlockSpec((bb, tt, n), lambda b, t: (b, t, 0))
    mod = pl.BlockSpec((bb, 1, D), lambda b, t: (b, 0, 0))
    full = lambda a: pl.BlockSpec(a.shape, lambda b, t: (0,) * a.ndim)
    outs = [(aw, BF16), (aw, F32), (aw, F32), (aw, BF16), (aw, BF16), (iw, F32), (D_I, F32),
            (4 * D_I, BF16), (H_I, F32), (bw, F32)]
    out_specs = [row(n) for n, _ in outs]
    out_shape = [jax.ShapeDtypeStruct((B, T, n), dt) for n, dt in outs]
    if keys_on_rows:
        out_specs[4] = pl.BlockSpec((1, 1, aw, tt), lambda b, t: (b, t, 0, 0))
        out_shape[4] = jax.ShapeDtypeStruct((B, T // tt, aw, tt), BF16)
    return pl.pallas_call(
        functools.partial(_in_even_body, aw=aw, keys_on_rows=keys_on_rows),
        grid=(B // bb, T // tt),
        in_specs=[row(D), mod, mod, full(g), full(wa), full(wh), full(wl), full(wb)],
        out_specs=out_specs,
        out_shape=out_shape,
        compiler_params=_cparams("arbitrary", "arbitrary"),
        name="in_even",
    )(x, scale, shift, g, wa, wh, wl, wb)


def _in_odd_body(x_ref, sc_ref, sh_ref, g_ref, w_ref, q_ref, k_ref, v_ref, kb_ref, vb_ref,
                 *, keys_on_rows):
    bb, tt, D = x_ref.shape
    M = bb * tt
    cw = q_ref.shape[-1]
    h = _norm_mod(x_ref[...], g_ref[...].reshape(1, 1, D), sc_ref[...], sh_ref[...]).reshape(M, D)
    z = _dot(h.astype(BF16), w_ref[...])
    q_ref[...] = (z[:, :cw] * _q_scale(keys_on_rows)).astype(BF16).reshape(bb, tt, cw)
    k = z[:, cw:2 * cw]
    v = z[:, 2 * cw:]
    k_ref[...] = k.reshape(bb, tt, cw)
    v_ref[...] = v.reshape(bb, tt, cw)
    kb_ref[...] = k.astype(BF16).reshape(bb, tt, cw)
    if keys_on_rows:
        vb_ref[0, 0] = v.T.astype(BF16)
    else:
        vb_ref[...] = v.astype(BF16).reshape(bb, tt, cw)


def _in_odd(x, scale, shift, g, w, keys_on_rows):
    B, T, D = x.shape
    cw = w.shape[1] // 3
    bb, tt = _row_tiles(B, T)
    assert not keys_on_rows or bb == 1
    row = lambda n: pl.BlockSpec((bb, tt, n), lambda b, t: (b, t, 0))
    mod = pl.BlockSpec((bb, 1, D), lambda b, t: (b, 0, 0))
    full = lambda a: pl.BlockSpec(a.shape, lambda b, t: (0,) * a.ndim)
    dts = [BF16, F32, F32, BF16, BF16]
    out_specs = [row(cw) for _ in dts]
    out_shape = [jax.ShapeDtypeStruct((B, T, cw), dt) for dt in dts]
    if keys_on_rows:
        out_specs[4] = pl.BlockSpec((1, 1, cw, tt), lambda b, t: (b, t, 0, 0))
        out_shape[4] = jax.ShapeDtypeStruct((B, T // tt, cw, tt), BF16)
    return pl.pallas_call(
        functools.partial(_in_odd_body, keys_on_rows=keys_on_rows),
        grid=(B // bb, T // tt),
        in_specs=[row(D), mod, mod, full(g), full(w)],
        out_specs=out_specs,
        out_shape=out_shape,
        compiler_params=_cparams("arbitrary", "arbitrary"),
        name="in_odd",
    )(x, scale, shift, g, w)


def _mix_ffn_body(*refs, n_mix, final):
    x_ref, g1_ref, sc2_ref, sh2_ref, g2_ref, gn_ref = refs[:6]
    mix_refs = refs[6:6 + n_mix]
    wmix_refs = refs[6 + n_mix:6 + 2 * n_mix]
    wg_ref, wu_ref, wd_ref = refs[6 + 2 * n_mix:9 + 2 * n_mix]
    gf_ref = refs[9 + 2 * n_mix] if final else None
    o_ref = refs[-1]
    bb, tt, D = x_ref.shape
    M = bb * tt
    m = None
    for mr, wr in zip(mix_refs, wmix_refs):
        part = _dot(mr[...].reshape(M, mr.shape[-1]), wr[...])
        m = part if m is None else m + part
    x1 = x_ref[...] + g1_ref[...] * m.reshape(bb, tt, D)
    h2 = _norm_mod(x1, gn_ref[...].reshape(1, 1, D), sc2_ref[...], sh2_ref[...])
    h2 = h2.reshape(M, D).astype(BF16)
    a = _dot(h2, wg_ref[...])
    b = _dot(h2, wu_ref[...])
    u = (_silu(a) * b).astype(BF16)
    f = _dot(u, wd_ref[...])
    x2 = x1 + g2_ref[...] * f.reshape(bb, tt, D)
    if final:
        ms = jnp.mean(x2 * x2, axis=-1, keepdims=True)
        x2 = x2 * lax.rsqrt(ms + EPS) * gf_ref[...].reshape(1, 1, D)
    o_ref[...] = x2


def _mix_ffn(x, g1, sc2, sh2, g2, gn, mixes, wmixes, wg, wu, wd, gf=None):
    B, T, D = x.shape
    bb, tt = _row_tiles(B, T)
    row = lambda n: pl.BlockSpec((bb, tt, n), lambda b, t: (b, t, 0))
    mod = pl.BlockSpec((bb, 1, D), lambda b, t: (b, 0, 0))
    full = lambda a: pl.BlockSpec(a.shape, lambda b, t: (0,) * a.ndim,
                                  pipeline_mode=pl.Buffered(1))
    final = gf is not None
    args = [x, g1, sc2, sh2, g2, gn, *mixes, *wmixes, wg, wu, wd] + ([gf] if final else [])
    specs = ([row(D), mod, mod, mod, mod, full(gn)] + [row(m.shape[-1]) for m in mixes]
             + [full(w) for w in wmixes] + [full(wg), full(wu), full(wd)]
             + ([full(gf)] if final else []))
    return pl.pallas_call(
        functools.partial(_mix_ffn_body, n_mix=len(mixes), final=final),
        grid=(B // bb, T // tt),
        in_specs=specs,
        out_specs=row(D),
        out_shape=jax.ShapeDtypeStruct((B, T, D), F32),
        compiler_params=_cparams("arbitrary", "arbitrary"),
        name="mix_ffn",
    )(*args)


def _hgrn_body(*refs, C, n_sub, e, has_s0):
    if has_s0:
        z_ref, lbl_ref, gn_ref, s0_ref, ob_ref, sout_ref, st_scr = refs
    else:
        z_ref, lbl_ref, gn_ref, ob_ref, sout_ref, st_scr = refs
    step = pl.program_id(1)
    n_heads = st_scr.shape[0]
    kw = n_heads * DK_B

    @pl.when(step == 0)
    def _():
        for h in range(n_heads):
            st_scr[h] = s0_ref[0, h].T if has_s0 else jnp.zeros((DV_B, DK_B), F32)

    ll = lbl_ref[...]
    ex = jnp.exp(ll - jnp.max(ll, axis=0, keepdims=True))
    sm = ex / jnp.sum(ex, axis=0, keepdims=True)
    lb = jnp.sum(sm[:e + 1], axis=0, keepdims=True)
    r = lax.broadcasted_iota(I32, (C, C), 0)
    c = lax.broadcasted_iota(I32, (C, C), 1)
    causal = r >= c
    tri = jnp.where(causal, 1.0, 0.0)
    gn = gn_ref[...]
    heads = [(slice(h * DK_B, (h + 1) * DK_B), slice(h * DV_B, (h + 1) * DV_B))
             for h in range(n_heads)]
    intra, q_sts, decs, updates, gates = [], [], [], [], []
    for sub in range(n_sub):
        z = z_ref[0, sub * C:(sub + 1) * C, :]
        qh = _silu(z[:, :kw])
        f = lb + (1.0 - lb) * jax.nn.sigmoid(z[:, kw:2 * kw])
        kh = 1.0 - f
        vh = z[:, 2 * kw:3 * kw]
        gates.append(_silu(z[:, 3 * kw:]))
        G = jnp.dot(tri, jnp.log(f), precision=HIGHEST, preferred_element_type=F32)
        g_end = G[C - 1:C, :]
        mid = 0.5 * g_end
        q_in = (qh * jnp.exp(G - mid)).astype(BF16)
        k_in = (kh * jnp.exp(mid - G)).astype(BF16)
        k_st = (kh * jnp.exp(g_end - G)).astype(BF16)
        q_sts.append((qh * jnp.exp(G)).astype(BF16))
        decs.append(jnp.exp(g_end))
        o_sub, u_sub = [], []
        for sk, sv in heads:
            att = jnp.where(causal, _dot_nt(q_in[:, sk], k_in[:, sk]), 0.0)
            v_h = vh[:, sv]
            o_sub.append(_dot(att.astype(BF16), v_h.astype(BF16)))
            u_sub.append(_dot(v_h.T.astype(BF16), k_st[:, sk]))
        intra.append(o_sub)
        updates.append(u_sub)
    states = [st_scr[h] for h in range(n_heads)]
    for sub in range(n_sub):
        outs = []
        for h, (sk, sv) in enumerate(heads):
            o = intra[sub][h] + _dot_nt(q_sts[sub][:, sk], states[h].astype(BF16))
            states[h] = states[h] * decs[sub][:, sk] + updates[sub][h]
            outs.append(o * lax.rsqrt(jnp.mean(o * o, axis=-1, keepdims=True) + EPS))
        ob = jnp.concatenate(outs, axis=1) * gn * gates[sub]
        ob_ref[0, sub * C:(sub + 1) * C, :] = ob.astype(BF16)
    for h in range(n_heads):
        st_scr[h] = states[h]

    @pl.when(step == pl.num_programs(1) - 1)
    def _():
        for h in range(n_heads):
            sout_ref[0, h] = st_scr[h].T


def _hgrn(zb, lb_logits, g_norm, s0, e):
    B, T, zw = zb.shape
    kw = zw // 4
    n_heads = kw // DK_B
    C = min(CHUNK, T)
    ct = min(T, 4 * C)
    has_s0 = s0 is not None
    full = lambda a: pl.BlockSpec(a.shape, lambda b, t: (0,) * a.ndim)
    state = pl.BlockSpec((1, n_heads, DK_B, DV_B), lambda b, t: (b, 0, 0, 0))
    args = [zb, lb_logits, g_norm] + ([s0] if has_s0 else [])
    specs = ([pl.BlockSpec((1, ct, zw), lambda b, t: (b, t, 0)), full(lb_logits), full(g_norm)]
             + ([state] if has_s0 else []))
    return pl.pallas_call(
        functools.partial(_hgrn_body, C=C, n_sub=ct // C, e=e, has_s0=has_s0),
        grid=(B, T // ct),
        in_specs=specs,
        out_specs=[pl.BlockSpec((1, ct, kw), lambda b, t: (b, t, 0)), state],
        out_shape=[jax.ShapeDtypeStruct((B, T, kw), BF16),
                   jax.ShapeDtypeStruct((B, n_heads, DK_B, DV_B), F32)],
        scratch_shapes=[pltpu.VMEM((n_heads, DV_B, DK_B), F32)],
        compiler_params=_cparams("arbitrary", "arbitrary"),
        name="hgrn2",
    )(*args)


def _topk_threshold(count_ge, count_gt, count_tie_below, write_ties, rows, ksel, idx_bits):
    def thr_step(i, thr):
        cand = thr + lax.shift_left(I32(1), 31 - i)
        return jnp.where(count_ge(cand) >= ksel, cand, thr)

    thr = lax.fori_loop(0, 32, thr_step, jnp.full((rows, V7X_LANES), INT_MIN, I32))
    no_thr = thr == INT_MIN
    write_ties(thr)
    excess = jnp.where(jnp.logical_or(no_thr, count_ge(thr) <= ksel), 0, 1)

    def resolve_ties():
        need = jnp.where(no_thr, 0, ksel - count_gt(thr))

        def tie_step(i, end):
            cand = end + lax.shift_left(I32(1), idx_bits - 1 - i)
            return jnp.where(count_tie_below(cand) <= need, cand, end)

        return lax.fori_loop(0, idx_bits, tie_step, jnp.zeros((rows, V7X_LANES), I32))

    tie_end = lax.cond(jnp.max(excess) > 0, resolve_ties,
                       lambda: jnp.where(no_thr, 0, TIE_NONE).astype(I32))
    return thr, tie_end


def _lane_total(cnt):
    return jnp.broadcast_to(jnp.sum(cnt, axis=1, keepdims=True), cnt.shape)


I16_MIN = -(2**15)


def _dsa_prompt_t_body(q_ref, qi_ref, wi_ref, k_ref, vt_ref, ki3_ref, bias_ref, o_ref,
                       key_scr, hi_scr, lo_scr, tix_scr, neg_scr, qi3_scr, qz_scr,
                       m_scr, l_scr, acc_scr, s_scr, p_scr, *, tq, ksel, idx_bits):
    L = V7X_LANES
    n_heads = qz_scr.shape[0]
    qt = pl.program_id(1)
    nkb = qt + 1
    shift = CHUNK.bit_length() - 1
    row_i = lax.broadcasted_iota(I32, (tq, tq), 0)
    col_i = lax.broadcasted_iota(I32, (tq, tq), 1)
    t_chunk = (qt * tq + col_i) >> shift

    qi = qi_ref[0]
    qi_hi = qi.astype(BF16).astype(F32)
    qi_lo = qi - qi_hi
    zero = jnp.zeros((tq, D_I), F32)
    for h in range(H_I):
        sl = slice(h * D_I, (h + 1) * D_I)
        qi3_scr[h] = jnp.concatenate([qi_hi[:, sl], qi_hi[:, sl], qi_lo[:, sl], zero],
                                     axis=1).astype(BF16)
    w_t = (wi_ref[0] * (H_I ** -0.5 * D_I ** -0.5)).T

    def index_block(kb, carry):
        ks = pl.multiple_of(kb * tq, tq)
        ki3 = ki3_ref[0, pl.ds(ks, tq), :]
        score = None
        for h in range(H_I):
            term = w_t[h:h + 1, :] * jnp.maximum(_dot_nt(ki3, qi3_scr[h]), 0.0)
            score = term if score is None else score + term
        key = _sortable(score)
        s_chunk = (ks + row_i) >> shift
        key = jnp.where(s_chunk <= t_chunk, key, INT_MIN)
        key_scr[kb] = key
        hi_scr[kb] = (key >> 16).astype(jnp.int16)
        lo_scr[kb] = ((key & 0xFFFF) + I16_MIN).astype(jnp.int16)
        return carry

    lax.fori_loop(0, nkb, index_block, 0)

    def count(scr, pred, rows, dtype):
        n_acc = 4
        one, nil = jnp.ones((), dtype), jnp.zeros((), dtype)

        def body(kb, accs):
            accs = list(accs)
            for r in range(tq // rows):
                hit = pred(scr[kb, r * rows:(r + 1) * rows, :])
                accs[r % n_acc] = accs[r % n_acc] + jnp.where(hit, one, nil)
            return tuple(accs)
        accs = lax.fori_loop(0, nkb, body, tuple(jnp.zeros((rows, tq), dtype) for _ in range(n_acc)))
        tot = functools.reduce(lambda a, b: a + b, [a.astype(I32) for a in accs])
        return jnp.sum(tot, axis=0, keepdims=True)

    def count16(scr, pred):
        return count(scr, pred, 16, jnp.int16)

    def count32(scr, pred):
        return count(scr, pred, 8, I32)

    def search16(scr, target):
        def step(i, thr):
            cand = thr + lax.shift_left(I32(1), 15 - i)
            cand16 = jnp.broadcast_to(cand.astype(jnp.int16), (16, tq))
            return jnp.where(count16(scr, lambda x: x >= cand16) >= target, cand, thr)
        return lax.fori_loop(0, 16, step, jnp.full((1, tq), I16_MIN, I32))

    t_hi = search16(hi_scr, ksel)
    t_hi16 = jnp.broadcast_to(t_hi.astype(jnp.int16), (16, tq))
    above = count16(hi_scr, lambda x: x > t_hi16)

    def keep_bucket(kb, carry):
        for r in range(tq // 16):
            rows = slice(r * 16, (r + 1) * 16)
            lo_scr[kb, rows, :] = jnp.where(hi_scr[kb, rows, :] == t_hi16, lo_scr[kb, rows, :],
                                            jnp.int16(I16_MIN))
        return carry

    lax.fori_loop(0, nkb, keep_bucket, 0)
    t_lo = search16(lo_scr, ksel - above)
    thr = lax.shift_left(t_hi, 16) + (t_lo - I16_MIN)
    thr8 = jnp.broadcast_to(thr, (8, tq))
    n_ge = count32(key_scr, lambda x: x >= thr8)
    no_thr = thr == INT_MIN
    excess = jnp.where(jnp.logical_or(no_thr, n_ge <= ksel), 0, 1)

    def resolve_ties():
        need = jnp.where(no_thr, 0, ksel - count32(key_scr, lambda x: x > thr8))

        def write(kb, carry):
            tix_scr[kb] = jnp.where(key_scr[kb] == jnp.broadcast_to(thr, (tq, tq)),
                                    kb * tq + row_i, TIE_NONE)
            return carry

        lax.fori_loop(0, nkb, write, 0)

        def step(i, end):
            cand = end + lax.shift_left(I32(1), idx_bits - 1 - i)
            cand8 = jnp.broadcast_to(cand, (8, tq))
            return jnp.where(count32(tix_scr, lambda x: x < cand8) <= need, cand, end)

        return lax.fori_loop(0, idx_bits, step, jnp.zeros((1, tq), I32))

    tie_end = lax.cond(jnp.max(excess) > 0, resolve_ties,
                       lambda: jnp.where(no_thr, 0, TIE_NONE).astype(I32))

    def write_mask(kb, carry):
        key = key_scr[kb]
        thr_b = jnp.broadcast_to(thr, (tq, tq))
        tied = jnp.where(key == thr_b, kb * tq + row_i, TIE_NONE) < jnp.broadcast_to(tie_end, (tq, tq))
        neg_scr[kb] = jnp.where(key > thr_b, 0.0, jnp.where(tied, 0.0, -jnp.inf))
        return carry

    lax.fori_loop(0, nkb, write_mask, 0)

    lane = lax.broadcasted_iota(I32, (tq, L), 1)
    q = q_ref[0].astype(F32)
    for h in range(n_heads):
        qp = q[:, (h // 2) * L:(h // 2 + 1) * L]
        keep = (lane < DH) if h % 2 == 0 else (lane >= DH)
        qz_scr[h] = jnp.where(keep, qp, 0.0).astype(BF16)
        m_scr[h] = jnp.full((1, tq), -1e30, F32)
        l_scr[h] = jnp.zeros((1, tq), F32)
        acc_scr[h] = jnp.zeros((DH, tq), F32)

    rc = 32
    n_rc = tq // rc
    n_slots = s_scr.shape[0]

    def attend_block(kb, carry):
        ks = pl.multiple_of(kb * tq, tq)
        d = qt - kb

        def scores(h):
            kp = k_ref[0, pl.ds(ks, tq), (h // 2) * L:(h // 2 + 1) * L]
            x = _dot_nt(kp, qz_scr[h]) + bias_ref[d, h] + neg_scr[kb]
            s_scr[h % n_slots] = x
            return jnp.max(x.reshape(tq // 8, 8, tq), axis=0)

        maxes = [scores(h) for h in range(n_slots - 1)]
        for h in range(n_heads):
            if h + n_slots - 1 < n_heads:
                maxes.append(scores(h + n_slots - 1))
            slot = h % n_slots
            m_old = m_scr[h]
            m_new = jnp.maximum(m_old, jnp.max(maxes[h], axis=0, keepdims=True))
            alpha = jnp.exp2(m_old - m_new)
            m_scr[h] = m_new
            m_b = jnp.broadcast_to(m_new, (rc, tq))
            ps = None
            for r in range(n_rc):
                rows = slice(r * rc, (r + 1) * rc)
                p = jnp.exp2(s_scr[slot, rows, :] - m_b)
                p_scr[slot, rows, :] = p.astype(BF16)
                pr = jnp.sum(p.reshape(rc // 8, 8, tq), axis=0)
                ps = pr if ps is None else ps + pr
            l_scr[h] = alpha * l_scr[h] + jnp.sum(ps, axis=0, keepdims=True)
            vt = vt_ref[0, kb, h * DH:(h + 1) * DH, :]
            acc_scr[h] = acc_scr[h] * alpha + _dot(vt, p_scr[slot])
        return carry

    lax.fori_loop(0, nkb, attend_block, 0)
    for j in range(n_heads // 2):
        o_t = jnp.concatenate([acc_scr[2 * j] / l_scr[2 * j],
                               acc_scr[2 * j + 1] / l_scr[2 * j + 1]], axis=0)
        o_ref[0, :, j * L:(j + 1) * L] = o_t.T.astype(BF16)


def _dsa_prompt_t(q, qi, wi, kb16, vt16, ki3, bias_t, ksel):
    B, T, aw = q.shape
    n_heads = aw // DH
    tq = bias_t.shape[-1]
    nq = T // tq
    assert T < 2**15 and vt16.shape == (B, nq, aw, tq)
    row = lambda n: pl.BlockSpec((1, tq, n), lambda b, t: (b, t, 0))
    seq = lambda n: pl.BlockSpec((1, T, n), lambda b, t: (b, 0, 0))
    return pl.pallas_call(
        functools.partial(_dsa_prompt_t_body, tq=tq, ksel=ksel, idx_bits=T.bit_length()),
        grid=(B, nq),
        in_specs=[row(aw), row(H_I * D_I), row(H_I), seq(aw),
                  pl.BlockSpec((1, nq, aw, tq), lambda b, t: (b, 0, 0, 0)), seq(4 * D_I),
                  pl.BlockSpec(bias_t.shape, lambda b, t: (0, 0, 0, 0),
                               pipeline_mode=pl.Buffered(1))],
        out_specs=row(aw),
        out_shape=jax.ShapeDtypeStruct((B, T, aw), BF16),
        scratch_shapes=[pltpu.VMEM((nq, tq, tq), I32), pltpu.VMEM((nq, tq, tq), jnp.int16),
                        pltpu.VMEM((nq, tq, tq), jnp.int16), pltpu.VMEM((nq, tq, tq), I32),
                        pltpu.VMEM((nq, tq, tq), F32), pltpu.VMEM((H_I, tq, 4 * D_I), BF16),
                        pltpu.VMEM((n_heads, tq, V7X_LANES), BF16),
                        pltpu.VMEM((n_heads, 1, tq), F32), pltpu.VMEM((n_heads, 1, tq), F32),
                        pltpu.VMEM((n_heads, DH, tq), F32),
                        pltpu.VMEM((4, tq, tq), F32), pltpu.VMEM((4, tq, tq), BF16)],
        compiler_params=_cparams("arbitrary", "arbitrary"),
        name="dsa_prompt",
    )(q, qi, wi, kb16, vt16, ki3, bias_t)


def _pack_heads(q, n_heads):
    T, W = q.shape
    head_of_lane = lax.broadcasted_iota(I32, (T, W), 1) >> (DH.bit_length() - 1)
    qf = q.astype(F32)
    return jnp.concatenate([jnp.where(head_of_lane == h, qf, 0.0) for h in range(n_heads)],
                           axis=0).astype(BF16)


def _unpack_heads(acc, den, n_heads):
    T = acc.shape[0] // n_heads
    W = acc.shape[1]
    head_of_lane = lax.broadcasted_iota(I32, (T, W), 1) >> (DH.bit_length() - 1)
    out = jnp.zeros((T, W), F32)
    for h in range(n_heads):
        blk = acc[h * T:(h + 1) * T, :] / den[h * T:(h + 1) * T, :]
        out = jnp.where(head_of_lane == h, blk, out)
    return out


def _dsa_sample_body(q_ref, qi_ref, wi_ref, kn_ref, vn_ref, kin_ref, kc_ref, vc_ref, kic_ref,
                     bias_ref, biasn_ref, o_ref,
                     key_scr, tix_scr, keyn_scr, tixn_scr, sel_scr, qbd_scr, m_scr, l_scr, acc_scr,
                     *, ts, kc_len, n_new, ksel, idx_bits):
    L = V7X_LANES
    n_heads = qbd_scr.shape[0] // ts
    nc = key_scr.shape[0]
    step = pl.program_id(1)
    past = nc * kc_len
    lane = lax.broadcasted_iota(I32, (ts, L), 1)

    @pl.when(step == 0)
    def _():
        qi = qi_ref[0]
        qi_hm = jnp.concatenate([qi[:, h * D_I:(h + 1) * D_I] for h in range(H_I)], axis=0)
        qi_hi, qi_lo = _split_bf16(qi_hm)
        w = wi_ref[0] * (H_I ** -0.5 * D_I ** -0.5)

        def scores(ki):
            ki_hi, ki_lo = _split_bf16(ki)
            s = _dot_nt_3pass(qi_hi, qi_lo, ki_hi, ki_lo)
            acc = None
            for h in range(H_I):
                term = w[:, h:h + 1] * jnp.maximum(s[h * ts:(h + 1) * ts, :], 0.0)
                acc = term if acc is None else acc + term
            return _sortable(acc)

        for c in range(nc):
            key_scr[c] = scores(kic_ref[0, c * kc_len:(c + 1) * kc_len, :])
        keyn_scr[...] = jnp.where(lane < n_new, scores(kin_ref[0]), INT_MIN)

        def count(pred_key=None, pred_tie=None):
            cnt = jnp.zeros((ts, L), I32)
            for c in range(nc):
                for i in range(kc_len // L):
                    if pred_key is not None:
                        hit = pred_key(key_scr[c, :, i * L:(i + 1) * L])
                    else:
                        hit = pred_tie(tix_scr[c, :, i * L:(i + 1) * L])
                    cnt = cnt + jnp.where(hit, 1, 0)
            hit = pred_key(keyn_scr[...]) if pred_key is not None else pred_tie(tixn_scr[...])
            return _lane_total(cnt + jnp.where(hit, 1, 0))

        def write_ties(thr):
            for c in range(nc):
                for i in range(kc_len // L):
                    idx = c * kc_len + i * L + lane
                    tix_scr[c, :, i * L:(i + 1) * L] = jnp.where(
                        key_scr[c, :, i * L:(i + 1) * L] == thr, idx, TIE_NONE)
            tixn_scr[...] = jnp.where(keyn_scr[...] == thr, past + lane, TIE_NONE)

        thr, tie_end = _topk_threshold(
            lambda x: count(pred_key=lambda k: k >= x),
            lambda x: count(pred_key=lambda k: k > x),
            lambda x: count(pred_tie=lambda t: t < x),
            write_ties, ts, ksel, idx_bits)
        sel_scr[0] = thr
        sel_scr[1] = tie_end
        qbd_scr[...] = _pack_heads(q_ref[0], n_heads)
        m_scr[...] = jnp.full(m_scr.shape, -1e30, F32)
        l_scr[...] = jnp.zeros(l_scr.shape, F32)
        acc_scr[...] = jnp.zeros(acc_scr.shape, F32)

    thr = sel_scr[0]
    tie_end = sel_scr[1]

    def attend(keys, ties, k16, v16, bias):
        n = keys.shape[1]
        negs = []
        for i in range(n // L):
            sel = jnp.logical_or(keys[:, i * L:(i + 1) * L] > thr,
                                 ties[:, i * L:(i + 1) * L] < tie_end)
            negs.append(jnp.where(sel, 0.0, -jnp.inf))
        neg = jnp.concatenate(negs, axis=1) if len(negs) > 1 else negs[0]
        neg = jnp.concatenate([neg] * n_heads, axis=0)
        s = _dot_nt(qbd_scr[...], k16) + bias + neg
        m_old = m_scr[...]
        m_new = jnp.maximum(m_old, jnp.max(s, axis=1, keepdims=True))
        alpha = jnp.exp(m_old - m_new)
        p = jnp.exp(s - m_new)
        l_scr[...] = alpha * l_scr[...] + jnp.sum(p, axis=1, keepdims=True)
        m_scr[...] = m_new
        acc_scr[...] = acc_scr[...] * alpha + _dot(p.astype(BF16), v16)

    attend(key_scr[step], tix_scr[step], kc_ref[0], vc_ref[0],
           bias_ref[...])

    @pl.when(step == nc - 1)
    def _():
        attend(keyn_scr[...], tixn_scr[...], kn_ref[0], vn_ref[0], biasn_ref[...])
        o_ref[0] = _unpack_heads(acc_scr[...], l_scr[...], n_heads).astype(BF16)


def _dsa_sample(q, qi, wi, kn16, vn16, kin, cache_k, cache_v, cache_ki, bias, n_new, ksel):
    B, ts, aw = q.shape
    n_heads = aw // DH
    P = cache_k.shape[1]
    npad = kn16.shape[1]
    kc_len = min(P, 1024)
    nc = P // kc_len
    L = V7X_LANES
    per_b = lambda a: pl.BlockSpec((1,) + a.shape[1:], lambda b, c: (b, 0, 0))
    chunk = lambda n: pl.BlockSpec((1, kc_len, n), lambda b, c: (b, c, 0))
    rows = n_heads * ts
    return pl.pallas_call(
        functools.partial(_dsa_sample_body, ts=ts, kc_len=kc_len, n_new=n_new, ksel=ksel,
                          idx_bits=(P + npad).bit_length()),
        grid=(B, nc),
        in_specs=[per_b(q), per_b(qi), per_b(wi), per_b(kn16), per_b(vn16), per_b(kin),
                  chunk(aw), chunk(aw), per_b(cache_ki),
                  pl.BlockSpec((rows, kc_len), lambda b, c: (0, c)),
                  pl.BlockSpec((rows, npad), lambda b, c: (0, P // npad))],
        out_specs=pl.BlockSpec((1, ts, aw), lambda b, c: (b, 0, 0)),
        out_shape=jax.ShapeDtypeStruct((B, ts, aw), BF16),
        scratch_shapes=[pltpu.VMEM((nc, ts, kc_len), I32), pltpu.VMEM((nc, ts, kc_len), I32),
                        pltpu.VMEM((ts, npad), I32), pltpu.VMEM((ts, npad), I32),
                        pltpu.VMEM((2, ts, L), I32), pltpu.VMEM((rows, aw), BF16),
                        pltpu.VMEM((rows, 1), F32), pltpu.VMEM((rows, 1), F32),
                        pltpu.VMEM((rows, aw), F32)],
        compiler_params=_cparams("arbitrary", "arbitrary"),
        name="dsa_sample",
    )(q, qi, wi, kn16, vn16, kin, cache_k, cache_v, cache_ki, bias, bias)


def _band_prompt_body(q_ref, k_ref, vt_ref, bias_ref, o_ref, qz_scr, s_scr, p_scr, ot_scr,
                      *, tq, n_back):
    L = V7X_LANES
    n_heads = q_ref.shape[-1] // DH
    n_slots = s_scr.shape[0]
    nb = n_back + 1
    rc = 32
    n_rc = tq // rc
    qt = pl.program_id(1)
    lane = lax.broadcasted_iota(I32, (tq, L), 1)
    n_tiles = bias_ref.shape[0]
    blocks, starts, exists = [], [], []
    for j in range(nb):
        kb = qt - n_back + j
        blocks.append(jnp.maximum(kb, 0))
        starts.append(pl.multiple_of(jnp.maximum(kb, 0) * tq, tq))
        exists.append(kb >= 0)
    q = q_ref[0].astype(F32)
    for h in range(n_heads):
        qp = q[:, (h // 2) * L:(h // 2 + 1) * L]
        keep = (lane < DH) if h % 2 == 0 else (lane >= DH)
        qz_scr[h] = jnp.where(keep, qp, 0.0).astype(BF16)

    def scores(h):
        mx = None
        for j in range(nb):
            kp = k_ref[0, pl.ds(starts[j], tq), (h // 2) * L:(h // 2 + 1) * L]
            tile = jnp.where(exists[j], h * nb + j, n_tiles - 1)
            x = _dot_nt(kp, qz_scr[h]) + bias_ref[tile]
            s_scr[h % n_slots, j] = x
            xm = jnp.max(x.reshape(tq // 8, 8, tq), axis=0)
            mx = xm if mx is None else jnp.maximum(mx, xm)
        return mx

    maxes = [scores(h) for h in range(n_slots - 1)]
    for h in range(n_heads):
        if h + n_slots - 1 < n_heads:
            maxes.append(scores(h + n_slots - 1))
        slot = h % n_slots
        m_b = jnp.broadcast_to(jnp.max(maxes[h], axis=0, keepdims=True), (rc, tq))
        ps = None
        for j in range(nb):
            for r in range(n_rc):
                rows = slice(r * rc, (r + 1) * rc)
                p = jnp.exp2(s_scr[slot, j, rows, :] - m_b)
                p_scr[slot, j, rows, :] = p.astype(BF16)
                pr = jnp.sum(p.reshape(rc // 8, 8, tq), axis=0)
                ps = pr if ps is None else ps + pr
        out = None
        for j in range(nb):
            vt = vt_ref[0, blocks[j], h * DH:(h + 1) * DH, :]
            oj = _dot(vt, p_scr[slot, j])
            out = oj if out is None else out + oj
        ot_scr[h * DH:(h + 1) * DH, :] = out / jnp.sum(ps, axis=0, keepdims=True)
    for pair in range(n_heads // 2):
        o_ref[0, :, pair * L:(pair + 1) * L] = ot_scr[pair * L:(pair + 1) * L, :].T.astype(BF16)


def _band_prompt(q, kb16, vt16, bias_t):
    B, T, cw = q.shape
    n_heads = cw // DH
    tq = bias_t.shape[-1]
    nq = T // tq
    n_back = (bias_t.shape[0] - 1) // n_heads - 1
    assert vt16.shape == (B, nq, cw, tq)
    row = pl.BlockSpec((1, tq, cw), lambda b, t: (b, t, 0))
    return pl.pallas_call(
        functools.partial(_band_prompt_body, tq=tq, n_back=n_back),
        grid=(B, nq),
        in_specs=[row, pl.BlockSpec((1, T, cw), lambda b, t: (b, 0, 0)),
                  pl.BlockSpec((1, nq, cw, tq), lambda b, t: (b, 0, 0, 0)),
                  pl.BlockSpec(bias_t.shape, lambda b, t: (0, 0, 0),
                               pipeline_mode=pl.Buffered(1))],
        out_specs=row,
        out_shape=jax.ShapeDtypeStruct((B, T, cw), BF16),
        scratch_shapes=[pltpu.VMEM((n_heads, tq, V7X_LANES), BF16),
                        pltpu.VMEM((3, n_back + 1, tq, tq), F32),
                        pltpu.VMEM((3, n_back + 1, tq, tq), BF16),
                        pltpu.VMEM((cw, tq), F32)],
        compiler_params=_cparams("arbitrary", "arbitrary"),
        name="band_prompt",
    )(q, kb16, vt16, bias_t)


def _band_sample_body(q_ref, kn_ref, vn_ref, kc_ref, vc_ref, bias_ref, biasn_ref, o_ref):
    n_heads = q_ref.shape[-1] // DH
    qbd = _pack_heads(q_ref[0], n_heads)
    s_c = _dot_nt(qbd, kc_ref[0]) + bias_ref[...]
    s_n = _dot_nt(qbd, kn_ref[0]) + biasn_ref[...]
    m = jnp.maximum(jnp.max(s_c, axis=1, keepdims=True), jnp.max(s_n, axis=1, keepdims=True))
    p_c = jnp.exp(s_c - m)
    p_n = jnp.exp(s_n - m)
    den = jnp.sum(p_c, axis=1, keepdims=True) + jnp.sum(p_n, axis=1, keepdims=True)
    acc = _dot(p_c.astype(BF16), vc_ref[0]) + _dot(p_n.astype(BF16), vn_ref[0])
    o_ref[0] = _unpack_heads(acc, den, n_heads).astype(BF16)


def _band_sample(q, kn16, vn16, cache_k, cache_v, bias):
    B, ts, cw = q.shape
    n_heads = cw // DH
    cb = cache_k.shape[1]
    npad = kn16.shape[1]
    rows = n_heads * ts
    per_b = lambda a: pl.BlockSpec((1,) + a.shape[1:], lambda b: (b, 0, 0))
    return pl.pallas_call(
        _band_sample_body,
        grid=(B,),
        in_specs=[per_b(q), per_b(kn16), per_b(vn16), per_b(cache_k), per_b(cache_v),
                  pl.BlockSpec((rows, cb), lambda b: (0, 0)),
                  pl.BlockSpec((rows, npad), lambda b: (0, cb // npad))],
        out_specs=per_b(q),
        out_shape=jax.ShapeDtypeStruct((B, ts, cw), BF16),
        compiler_params=_cparams("arbitrary"),
        name="band_sample",
    )(q, kn16, vn16, cache_k, cache_v, bias, bias)


def _pad_rows(a, n):
    return jnp.pad(a, ((0, 0), (0, n - a.shape[1]), (0, 0)))


def kernel(x_prompt, x_sample, c_prompt, c_sample, cache_a_k, cache_a_v, cache_a_kidx, state_b,
           cache_c_k, cache_c_v, w_ada, b_ada, norm_mix, norm_ffn, norm_final, w_in_even,
           w_out_even, t5_table, lb_logits, norm_b_out, w_in_odd, w_out_odd, rel_table_c,
           w_ffn_gate, w_ffn_up, w_ffn_down):
    Bp, T, D = x_prompt.shape
    Bs, Ts, _ = x_sample.shape
    P = cache_a_k.shape[2]
    c_buf = cache_c_k.shape[2]
    depth = w_ada.shape[0]
    L = V7X_LANES
    n_heads_a = cache_a_k.shape[3]
    aw = n_heads_a * DH
    n_heads_c = cache_c_k.shape[3]
    bw = w_out_even.shape[1] - aw
    iw = H_I * D_I
    ksel_p = min(K_TOP, T // 4)
    ksel_s = min(K_TOP, (P + Ts) // 4)
    c_keep = min(C_BACK * CHUNK, T)
    tq = min(T, 256)
    npad = L
    assert Ts <= npad and P % npad == 0 and c_buf % npad == 0 and T % tq == 0

    mod = _adaln_all(jnp.concatenate([c_prompt, c_sample], axis=0), w_ada, b_ada)

    def mods(l):
        parts = jnp.split(mod[l], 6, axis=-1)
        return ([p[:Bp].reshape(Bp, 1, D) for p in parts],
                [p[Bp:].reshape(Bs, 1, D) for p in parts])

    nq = T // tq
    t5_prompt = _bias_tiles(
        t5_table, groups=nq * n_heads_a, rows=tq, cols=tq, idx_fn=_t5_bucket,
        group_fn=lambda g: (g % n_heads_a, -(g // n_heads_a) * tq), keys_on_rows=True,
    ).reshape(nq, n_heads_a, tq, tq)
    t5_sample = _bias_tiles(
        t5_table, groups=n_heads_a, rows=Ts, cols=P + npad, idx_fn=_t5_bucket,
        group_fn=lambda g: (g, -P),
    ).reshape(n_heads_a * Ts, P + npad)

    xp, xs = x_prompt, x_sample
    outs_p = {k: [] for k in ("ak", "av", "ai", "bs", "ck", "cv")}
    outs_s = {k: [] for k in ("ak", "av", "ai", "bs", "ck", "cv")}
    for l in range(depth):
        (sh1p, sc1p, g1p, sh2p, sc2p, g2p), (sh1s, sc1s, g1s, sh2s, sc2s, g2s) = mods(l)
        gm = norm_mix[l].reshape(1, D)
        gn = norm_ffn[l].reshape(1, D)
        wg = w_ffn_gate[l].astype(BF16)
        wu = w_ffn_up[l].astype(BF16)
        wd = w_ffn_down[l].astype(BF16)
        gf = norm_final.reshape(1, D) if l == depth - 1 else None
        if l % 2 == 0:
            e = l // 2
            w_in = w_in_even[e]
            wa = w_in[:, :3 * aw].astype(BF16)
            wi_cols = w_in[:, 3 * aw:3 * aw + iw + D_I + H_I]
            wi_cols = jnp.pad(wi_cols, ((0, 0), (0, -wi_cols.shape[1] % L)))
            wh = wi_cols.astype(BF16)
            wl = (wi_cols - wh.astype(F32)).astype(BF16)
            wb = w_in[:, 3 * aw + iw + D_I + H_I:].astype(BF16)
            w_out = w_out_even[e].astype(BF16)
            gb_norm = norm_b_out[e].reshape(1, bw)

            q, k, v, k16, vt16, qi, ki, ki3, wi, zb = _in_even(xp, sc1p, sh1p, gm, wa, wh, wl, wb,
                                                              keys_on_rows=True)
            oa = _dsa_prompt_t(q, qi, wi, k16, vt16, ki3, t5_prompt, ksel_p)
            ob, st = _hgrn(zb, lb_logits, gb_norm, None, e)
            outs_p["ak"].append(k.reshape(Bp, T, n_heads_a, DH))
            outs_p["av"].append(v.reshape(Bp, T, n_heads_a, DH))
            outs_p["ai"].append(ki)
            outs_p["bs"].append(st)
            xp = _mix_ffn(xp, g1p, sc2p, sh2p, g2p, gn, [oa, ob], [w_out[:aw], w_out[aw:]],
                          wg, wu, wd, gf)

            q, k, v, k16, v16, qi, ki, _, wi, zb = _in_even(xs, sc1s, sh1s, gm, wa, wh, wl, wb,
                                                            keys_on_rows=False)
            oa = _dsa_sample(q, qi, wi, _pad_rows(k16, npad), _pad_rows(v16, npad),
                             _pad_rows(ki, npad), cache_a_k[e].reshape(Bs, P, aw).astype(BF16),
                             cache_a_v[e].reshape(Bs, P, aw).astype(BF16), cache_a_kidx[e],
                             t5_sample,
                             Ts, ksel_s)
            ob, st = _hgrn(zb, lb_logits, gb_norm, state_b[e], e)
            outs_s["ak"].append(k.reshape(Bs, Ts, n_heads_a, DH))
            outs_s["av"].append(v.reshape(Bs, Ts, n_heads_a, DH))
            outs_s["ai"].append(ki)
            outs_s["bs"].append(st)
            xs = _mix_ffn(xs, g1s, sc2s, sh2s, g2s, gn, [oa, ob], [w_out[:aw], w_out[aw:]],
                          wg, wu, wd, gf)
        else:
            o = l // 2
            cw = n_heads_c * DH
            w_in = w_in_odd[o].astype(BF16)
            w_out = w_out_odd[o].astype(BF16)
            n_back = -(-(C_BACK * CHUNK) // tq)
            shift = CHUNK.bit_length() - 1

            n_band = n_heads_c * (n_back + 1)

            def band_mask(g, t, s, n_back=n_back, n_band=n_band):
                j = g % (n_back + 1)
                kc = (s + (j - n_back) * tq) >> shift
                tc = t >> shift
                return (kc <= tc) & (kc >= tc - C_BACK) & (g < n_band)

            band_p = _bias_tiles(
                rel_table_c[o], groups=n_band + 1, rows=tq, cols=tq, idx_fn=_clip_index,
                group_fn=lambda g, n_back=n_back: (jnp.minimum(g // (n_back + 1), n_heads_c - 1),
                                                   (g % (n_back + 1) - n_back) * tq),
                mask_fn=band_mask, keys_on_rows=True)
            band_s = _bias_tiles(
                rel_table_c[o], groups=n_heads_c, rows=Ts, cols=c_buf + npad, idx_fn=_clip_index,
                group_fn=lambda g: (g, -c_buf),
                mask_fn=lambda g, t, s: s < c_buf + Ts,
            ).reshape(n_heads_c * Ts, c_buf + npad)

            q, k, v, k16, vt16 = _in_odd(xp, sc1p, sh1p, gm, w_in, keys_on_rows=True)
            op = _band_prompt(q, k16, vt16, band_p)
            outs_p["ck"].append(k[:, T - c_keep:].reshape(Bp, c_keep, n_heads_c, DH))
            outs_p["cv"].append(v[:, T - c_keep:].reshape(Bp, c_keep, n_heads_c, DH))
            xp = _mix_ffn(xp, g1p, sc2p, sh2p, g2p, gn, [op], [w_out], wg, wu, wd, gf)

            q, k, v, k16, v16 = _in_odd(xs, sc1s, sh1s, gm, w_in, keys_on_rows=False)
            osm = _band_sample(q, _pad_rows(k16, npad), _pad_rows(v16, npad),
                               cache_c_k[o].reshape(Bs, c_buf, cw).astype(BF16),
                               cache_c_v[o].reshape(Bs, c_buf, cw).astype(BF16), band_s)
            outs_s["ck"].append(k.reshape(Bs, Ts, n_heads_c, DH))
            outs_s["cv"].append(v.reshape(Bs, Ts, n_heads_c, DH))
            xs = _mix_ffn(xs, g1s, sc2s, sh2s, g2s, gn, [osm], [w_out], wg, wu, wd, gf)

    st = lambda xs_: jnp.stack(xs_)
    return (xp, xs, st(outs_p["ak"]), st(outs_p["av"]), st(outs_p["ai"]), st(outs_p["bs"]),
            st(outs_p["ck"]), st(outs_p["cv"]), st(outs_s["ak"]), st(outs_s["av"]),
            st(outs_s["ai"]), st(outs_s["bs"]), st(outs_s["ck"]), st(outs_s["cv"]))
```

```python
import functools
import math

import jax
import jax.numpy as jnp
from jax import lax
from jax.experimental import pallas as pl
from jax.experimental.pallas import tpu as pltpu

F32, BF16, I32 = jnp.float32, jnp.bfloat16, jnp.int32

CHUNK = 64
EPS = 1e-6
DH = 64
H_I = 8
D_I = 64
K_TOP = 256
N_BUCKETS = 32
T5_MAX_DIST = 1024
DK_B = 128
DV_B = 128
C_BACK = 8
REL_CLIP = 128

V7X_LANES = 128
BF16_ROWS = 16
V7X_VMEM_LIMIT = 56 * 2**20

INT_MIN = -(2**31)
TIE_NONE = 2**30
HIGHEST = lax.Precision.HIGHEST


def _cparams(*sem):
    return pltpu.CompilerParams(dimension_semantics=sem, vmem_limit_bytes=V7X_VMEM_LIMIT)


def _dot(a, b):
    return jnp.dot(a, b, preferred_element_type=F32)


def _dot_nt(a, b):
    return lax.dot_general(a, b, (((1,), (1,)), ((), ())), preferred_element_type=F32)


def _split_bf16(x):
    hi = x.astype(BF16)
    lo = (x - hi.astype(F32)).astype(BF16)
    return hi, lo


def _silu(x):
    return x * jax.nn.sigmoid(x)


def _norm_mod(x, g, scale, shift):
    ms = jnp.mean(x * x, axis=-1, keepdims=True)
    return (x * lax.rsqrt(ms + EPS) * g) * (1.0 + scale) + shift


def _sortable(x):
    b = pltpu.bitcast(x + 0.0, I32)
    return b ^ ((b >> 31) & I32(0x7FFFFFFF))


LOG2E = math.log2(math.e)


def _q_scale(base2):
    return DH ** -0.5 * (LOG2E if base2 else 1.0)


def _row_tiles(B, T, rows=256):
    tt = min(T, rows)
    bb = max(1, min(B, rows // tt))
    assert T % tt == 0 and B % bb == 0
    return bb, tt


def _ada_body(c_ref, w_ref, b_ref, o_ref):
    s = _silu(c_ref[...])
    o_ref[0] = jnp.dot(s, w_ref[0], precision=HIGHEST, preferred_element_type=F32) + b_ref[0]


def _adaln_all(c_all, w_ada, b_ada):
    depth, D, E = w_ada.shape
    R = c_all.shape[0]
    tn = 1024
    return pl.pallas_call(
        _ada_body,
        grid=(depth, E // tn),
        in_specs=[pl.BlockSpec((R, D), lambda l, n: (0, 0)),
                  pl.BlockSpec((1, D, tn), lambda l, n: (l, 0, n)),
                  pl.BlockSpec((1, 1, tn), lambda l, n: (l, 0, n))],
        out_specs=pl.BlockSpec((1, R, tn), lambda l, n: (l, 0, n)),
        out_shape=jax.ShapeDtypeStruct((depth, R, E), F32),
        compiler_params=_cparams("arbitrary", "arbitrary"),
        name="adaln",
    )(c_all, w_ada, b_ada.reshape(depth, 1, E))


def _t5_bucket(rel):
    nb = N_BUCKETS // 2
    max_exact = nb // 2
    base = jnp.where(rel > 0, nb, 0)
    n = jnp.abs(rel)
    large = max_exact + (jnp.log(jnp.maximum(n, 1).astype(F32) / max_exact)
                         / math.log(T5_MAX_DIST / max_exact) * (nb - max_exact)).astype(I32)
    large = jnp.minimum(large, nb - 1)
    return base + jnp.where(n < max_exact, n, large)


def _clip_index(rel):
    return jnp.clip(-rel, -REL_CLIP, REL_CLIP) + REL_CLIP


def _bias_body(tab_ref, o_ref, *, rows, cols, width, n_tab, idx_fn, group_fn, mask_fn,
               keys_on_rows):
    g = pl.program_id(0)
    head, off = group_fn(g)
    i = lax.broadcasted_iota(I32, (1, width), 1)
    diff = jnp.where(i < width - rows, i, i - width)
    idx = idx_fn(off - diff if keys_on_rows else off + diff)
    val = jnp.zeros((1, width), F32)
    for b in range(n_tab):
        val = jnp.where(idx == b, tab_ref[head, b], val)
    x = jnp.broadcast_to(val, (rows, width))
    y = pltpu.roll(x, 0, 1, stride=1, stride_axis=0)
    tile = y[:, :cols]
    if keys_on_rows:
        tile = tile * LOG2E
    if mask_fn is not None:
        t = lax.broadcasted_iota(I32, (rows, cols), 1 if keys_on_rows else 0)
        s = lax.broadcasted_iota(I32, (rows, cols), 0 if keys_on_rows else 1)
        tile = jnp.where(mask_fn(g, t, s), tile, -jnp.inf)
    o_ref[0] = tile.astype(o_ref.dtype)


def _bias_tiles(table, *, groups, rows, cols, idx_fn, group_fn, mask_fn=None,
                keys_on_rows=False):
    n_tab = table.shape[0]
    width = -(-(rows + cols) // V7X_LANES) * V7X_LANES
    body = functools.partial(_bias_body, rows=rows, cols=cols, width=width, n_tab=n_tab,
                             idx_fn=idx_fn, group_fn=group_fn, mask_fn=mask_fn,
                             keys_on_rows=keys_on_rows)
    return pl.pallas_call(
        body,
        grid=(groups,),
        in_specs=[pl.BlockSpec(memory_space=pltpu.SMEM)],
        out_specs=pl.BlockSpec((1, rows, cols), lambda g: (g, 0, 0)),
        out_shape=jax.ShapeDtypeStruct((groups, rows, cols), BF16 if keys_on_rows else F32),
        compiler_params=_cparams("arbitrary"),
        name="bias_tiles",
    )(table.T)


def _in_even_body(x_ref, sc_ref, sh_ref, g_ref, wa_ref, wh_ref, wl_ref, wb_ref,
                  q_ref, k_ref, v_ref, kb_ref, vb_ref, qi_ref, ki_ref, ki3_ref, wi_ref, zb_ref,
                  *, aw, keys_on_rows):
    bb, tt, D = x_ref.shape
    M = bb * tt
    h = _norm_mod(x_ref[...], g_ref[...].reshape(1, 1, D), sc_ref[...], sh_ref[...]).reshape(M, D)
    h_hi, h_lo = _split_bf16(h)
    za = _dot(h_hi, wa_ref[...])
    q_ref[...] = (za[:, :aw] * _q_scale(keys_on_rows)).astype(BF16).reshape(bb, tt, aw)
    k = za[:, aw:2 * aw]
    v = za[:, 2 * aw:3 * aw]
    k_ref[...] = k.reshape(bb, tt, aw)
    v_ref[...] = v.reshape(bb, tt, aw)
    kb_ref[...] = k.astype(BF16).reshape(bb, tt, aw)
    if keys_on_rows:
        vb_ref[0, 0] = v.T.astype(BF16)
    else:
        vb_ref[...] = v.astype(BF16).reshape(bb, tt, aw)
    wh = wh_ref[...]
    zi = _dot(h_hi, wh) + _dot(h_hi, wl_ref[...]) + _dot(h_lo, wh)
    iw = H_I * D_I
    ki = zi[:, iw:iw + D_I]
    qi_ref[...] = zi[:, :iw].reshape(bb, tt, iw)
    ki_ref[...] = ki.reshape(bb, tt, D_I)
    ki_hi = ki.astype(BF16).astype(F32)
    ki3 = jnp.concatenate([ki_hi, ki - ki_hi, ki_hi, jnp.zeros_like(ki)], axis=1)
    ki3_ref[...] = ki3.astype(BF16).reshape(bb, tt, 4 * D_I)
    wi_ref[...] = zi[:, iw + D_I:iw + D_I + H_I].reshape(bb, tt, H_I)
    zb_ref[...] = _dot(h_hi, wb_ref[...]).reshape(bb, tt, zb_ref.shape[-1])


def _in_even(x, scale, shift, g, wa, wh, wl, wb, keys_on_rows):
    B, T, D = x.shape
    aw = wa.shape[1] // 3
    bw = wb.shape[1]
    iw = H_I * D_I
    bb, tt = _row_tiles(B, T)
    assert not keys_on_rows or bb == 1
    row = lambda n: pl.BlockSpec((bb, tt, n), lambda b, t: (b, t, 0))
    mod = pl.BlockSpec((bb, 1, D), lambda b, t: (b, 0, 0))
    full = lambda a: pl.BlockSpec(a.shape, lambda b, t: (0,) * a.ndim)
    outs = [(aw, BF16), (aw, F32), (aw, F32), (aw, BF16), (aw, BF16), (iw, F32), (D_I, F32),
            (4 * D_I, BF16), (H_I, F32), (bw, F32)]
    out_specs = [row(n) for n, _ in outs]
    out_shape = [jax.ShapeDtypeStruct((B, T, n), dt) for n, dt in outs]
    if keys_on_rows:
        out_specs[4] = pl.BlockSpec((1, 1, aw, tt), lambda b, t: (b, t, 0, 0))
        out_shape[4] = jax.ShapeDtypeStruct((B, T // tt, aw, tt), BF16)
    return pl.pallas_call(
        functools.partial(_in_even_body, aw=aw, keys_on_rows=keys_on_rows),
        grid=(B // bb, T // tt),
        in_specs=[row(D), mod, mod, full(g), full(wa), full(wh), full(wl), full(wb)],
        out_specs=out_specs,
        out_shape=out_shape,
        compiler_params=_cparams("arbitrary", "arbitrary"),
        name="in_even",
    )(x, scale, shift, g, wa, wh, wl, wb)


def _in_odd_body(x_ref, sc_ref, sh_ref, g_ref, w_ref, q_ref, k_ref, v_ref, kb_ref, vb_ref,
                 *, keys_on_rows):
    bb, tt, D = x_ref.shape
    M = bb * tt
    cw = q_ref.shape[-1]
    h = _norm_mod(x_ref[...], g_ref[...].reshape(1, 1, D), sc_ref[...], sh_ref[...]).reshape(M, D)
    z = _dot(h.astype(BF16), w_ref[...])
    q_ref[...] = (z[:, :cw] * _q_scale(keys_on_rows)).astype(BF16).reshape(bb, tt, cw)
    k = z[:, cw:2 * cw]
    v = z[:, 2 * cw:]
    k_ref[...] = k.reshape(bb, tt, cw)
    v_ref[...] = v.reshape(bb, tt, cw)
    kb_ref[...] = k.astype(BF16).reshape(bb, tt, cw)
    if keys_on_rows:
        vb_ref[0, 0] = v.T.astype(BF16)
    else:
        vb_ref[...] = v.astype(BF16).reshape(bb, tt, cw)


def _in_odd(x, scale, shift, g, w, keys_on_rows):
    B, T, D = x.shape
    cw = w.shape[1] // 3
    bb, tt = _row_tiles(B, T)
    assert not keys_on_rows or bb == 1
    row = lambda n: pl.BlockSpec((bb, tt, n), lambda b, t: (b, t, 0))
    mod = pl.BlockSpec((bb, 1, D), lambda b, t: (b, 0, 0))
    full = lambda a: pl.BlockSpec(a.shape, lambda b, t: (0,) * a.ndim)
    dts = [BF16, F32, F32, BF16, BF16]
    out_specs = [row(cw) for _ in dts]
    out_shape = [jax.ShapeDtypeStruct((B, T, cw), dt) for dt in dts]
    if keys_on_rows:
        out_specs[4] = pl.BlockSpec((1, 1, cw, tt), lambda b, t: (b, t, 0, 0))
        out_shape[4] = jax.ShapeDtypeStruct((B, T // tt, cw, tt), BF16)
    return pl.pallas_call(
        functools.partial(_in_odd_body, keys_on_rows=keys_on_rows),
        grid=(B // bb, T // tt),
        in_specs=[row(D), mod, mod, full(g), full(w)],
        out_specs=out_specs,
        out_shape=out_shape,
        compiler_params=_cparams("arbitrary", "arbitrary"),
        name="in_odd",
    )(x, scale, shift, g, w)


def _mix_ffn_body(*refs, n_mix, final):
    x_ref, g1_ref, sc2_ref, sh2_ref, g2_ref, gn_ref = refs[:6]
    mix_refs = refs[6:6 + n_mix]
    wmix_refs = refs[6 + n_mix:6 + 2 * n_mix]
    wg_ref, wu_ref, wd_ref = refs[6 + 2 * n_mix:9 + 2 * n_mix]
    gf_ref = refs[9 + 2 * n_mix] if final else None
    o_ref = refs[-1]
    bb, tt, D = x_ref.shape
    M = bb * tt
    m = None
    for mr, wr in zip(mix_refs, wmix_refs):
        part = _dot(mr[...].reshape(M, mr.shape[-1]), wr[...])
        m = part if m is None else m + part
    x1 = x_ref[...] + g1_ref[...] * m.reshape(bb, tt, D)
    h2 = _norm_mod(x1, gn_ref[...].reshape(1, 1, D), sc2_ref[...], sh2_ref[...])
    h2 = h2.reshape(M, D).astype(BF16)
    a = _dot(h2, wg_ref[...])
    b = _dot(h2, wu_ref[...])
    u = (_silu(a) * b).astype(BF16)
    f = _dot(u, wd_ref[...])
    x2 = x1 + g2_ref[...] * f.reshape(bb, tt, D)
    if final:
        ms = jnp.mean(x2 * x2, axis=-1, keepdims=True)
        x2 = x2 * lax.rsqrt(ms + EPS) * gf_ref[...].reshape(1, 1, D)
    o_ref[...] = x2


def _mix_ffn(x, g1, sc2, sh2, g2, gn, mixes, wmixes, wg, wu, wd, gf=None):
    B, T, D = x.shape
    bb, tt = _row_tiles(B, T)
    row = lambda n: pl.BlockSpec((bb, tt, n), lambda b, t: (b, t, 0))
    mod = pl.BlockSpec((bb, 1, D), lambda b, t: (b, 0, 0))
    full = lambda a: pl.BlockSpec(a.shape, lambda b, t: (0,) * a.ndim,
                                  pipeline_mode=pl.Buffered(1))
    final = gf is not None
    args = [x, g1, sc2, sh2, g2, gn, *mixes, *wmixes, wg, wu, wd] + ([gf] if final else [])
    specs = ([row(D), mod, mod, mod, mod, full(gn)] + [row(m.shape[-1]) for m in mixes]
             + [full(w) for w in wmixes] + [full(wg), full(wu), full(wd)]
             + ([full(gf)] if final else []))
    return pl.pallas_call(
        functools.partial(_mix_ffn_body, n_mix=len(mixes), final=final),
        grid=(B // bb, T // tt),
        in_specs=specs,
        out_specs=row(D),
        out_shape=jax.ShapeDtypeStruct((B, T, D), F32),
        compiler_params=_cparams("arbitrary", "arbitrary"),
        name="mix_ffn",
    )(*args)


def _hgrn_body(*refs, C, n_sub, e, has_s0):
    if has_s0:
        z_ref, lbl_ref, gn_ref, s0_ref, ob_ref, sout_ref, st_scr = refs
    else:
        z_ref, lbl_ref, gn_ref, ob_ref, sout_ref, st_scr = refs
    step = pl.program_id(1)
    n_heads = st_scr.shape[0]
    kw = n_heads * DK_B

    @pl.when(step == 0)
    def _():
        for h in range(n_heads):
            st_scr[h] = s0_ref[0, h].T if has_s0 else jnp.zeros((DV_B, DK_B), F32)

    ll = lbl_ref[...]
    ex = jnp.exp(ll - jnp.max(ll, axis=0, keepdims=True))
    sm = ex / jnp.sum(ex, axis=0, keepdims=True)
    lb = jnp.sum(sm[:e + 1], axis=0, keepdims=True)
    r = lax.broadcasted_iota(I32, (C, C), 0)
    c = lax.broadcasted_iota(I32, (C, C), 1)
    causal = r >= c
    tri = jnp.where(causal, 1.0, 0.0)
    gn = gn_ref[...]
    heads = [(slice(h * DK_B, (h + 1) * DK_B), slice(h * DV_B, (h + 1) * DV_B))
             for h in range(n_heads)]
    intra, q_sts, decs, updates, gates = [], [], [], [], []
    for sub in range(n_sub):
        z = z_ref[0, sub * C:(sub + 1) * C, :]
        qh = _silu(z[:, :kw])
        f = lb + (1.0 - lb) * jax.nn.sigmoid(z[:, kw:2 * kw])
        kh = 1.0 - f
        vh = z[:, 2 * kw:3 * kw]
        gates.append(_silu(z[:, 3 * kw:]))
        G = jnp.dot(tri, jnp.log(f), precision=HIGHEST, preferred_element_type=F32)
        g_end = G[C - 1:C, :]
        mid = 0.5 * g_end
        q_in = (qh * jnp.exp(G - mid)).astype(BF16)
        k_in = (kh * jnp.exp(mid - G)).astype(BF16)
        k_st = (kh * jnp.exp(g_end - G)).astype(BF16)
        q_sts.append((qh * jnp.exp(G)).astype(BF16))
        decs.append(jnp.exp(g_end))
        o_sub, u_sub = [], []
        for sk, sv in heads:
            att = jnp.where(causal, _dot_nt(q_in[:, sk], k_in[:, sk]), 0.0)
            v_h = vh[:, sv]
            o_sub.append(_dot(att.astype(BF16), v_h.astype(BF16)))
            u_sub.append(_dot(v_h.T.astype(BF16), k_st[:, sk]))
        intra.append(o_sub)
        updates.append(u_sub)
    states = [st_scr[h] for h in range(n_heads)]
    for sub in range(n_sub):
        outs = []
        for h, (sk, sv) in enumerate(heads):
            o = intra[sub][h] + _dot_nt(q_sts[sub][:, sk], states[h].astype(BF16))
            states[h] = states[h] * decs[sub][:, sk] + updates[sub][h]
            outs.append(o * lax.rsqrt(jnp.mean(o * o, axis=-1, keepdims=True) + EPS))
        ob = jnp.concatenate(outs, axis=1) * gn * gates[sub]
        ob_ref[0, sub * C:(sub + 1) * C, :] = ob.astype(BF16)
    for h in range(n_heads):
        st_scr[h] = states[h]

    @pl.when(step == pl.num_programs(1) - 1)
    def _():
        for h in range(n_heads):
            sout_ref[0, h] = st_scr[h].T


def _hgrn(zb, lb_logits, g_norm, s0, e):
    B, T, zw = zb.shape
    kw = zw // 4
    n_heads = kw // DK_B
    C = min(CHUNK, T)
    ct = min(T, 4 * C)
    has_s0 = s0 is not None
    full = lambda a: pl.BlockSpec(a.shape, lambda b, t: (0,) * a.ndim)
    state = pl.BlockSpec((1, n_heads, DK_B, DV_B), lambda b, t: (b, 0, 0, 0))
    args = [zb, lb_logits, g_norm] + ([s0] if has_s0 else [])
    specs = ([pl.BlockSpec((1, ct, zw), lambda b, t: (b, t, 0)), full(lb_logits), full(g_norm)]
             + ([state] if has_s0 else []))
    return pl.pallas_call(
        functools.partial(_hgrn_body, C=C, n_sub=ct // C, e=e, has_s0=has_s0),
        grid=(B, T // ct),
        in_specs=specs,
        out_specs=[pl.BlockSpec((1, ct, kw), lambda b, t: (b, t, 0)), state],
        out_shape=[jax.ShapeDtypeStruct((B, T, kw), BF16),
                   jax.ShapeDtypeStruct((B, n_heads, DK_B, DV_B), F32)],
        scratch_shapes=[pltpu.VMEM((n_heads, DV_B, DK_B), F32)],
        compiler_params=_cparams("arbitrary", "arbitrary"),
        name="hgrn2",
    )(*args)


def _topk_threshold(count_ge, count_gt, count_tie_below, write_ties, rows, ksel, idx_bits):
    def thr_step(i, thr):
        cand = thr + lax.shift_left(I32(1), 31 - i)
        return jnp.where(count_ge(cand) >= ksel, cand, thr)

    thr = lax.fori_loop(0, 32, thr_step, jnp.full((rows, V7X_LANES), INT_MIN, I32))
    no_thr = thr == INT_MIN
    write_ties(thr)
    excess = jnp.where(jnp.logical_or(no_thr, count_ge(thr) <= ksel), 0, 1)

    def resolve_ties():
        need = jnp.where(no_thr, 0, ksel - count_gt(thr))

        def tie_step(i, end):
            cand = end + lax.shift_left(I32(1), idx_bits - 1 - i)
            return jnp.where(count_tie_below(cand) <= need, cand, end)

        return lax.fori_loop(0, idx_bits, tie_step, jnp.zeros((rows, V7X_LANES), I32))

    tie_end = lax.cond(jnp.max(excess) > 0, resolve_ties,
                       lambda: jnp.where(no_thr, 0, TIE_NONE).astype(I32))
    return thr, tie_end


def _lane_total(cnt):
    return jnp.broadcast_to(jnp.sum(cnt, axis=1, keepdims=True), cnt.shape)


I16_MIN = -(2**15)


def _dsa_prompt_t_body(q_ref, qi_ref, wi_ref, k_ref, vt_ref, ki3_ref, bias_ref, o_ref,
                       key_scr, hi_scr, lo_scr, tix_scr, neg_scr, qi3_scr, qz_scr,
                       m_scr, l_scr, acc_scr, s_scr, mx_scr, p_scr, *, tq, ksel, idx_bits):
    L = V7X_LANES
    n_heads = qz_scr.shape[0]
    qt = pl.program_id(1)
    nkb = qt + 1
    shift = CHUNK.bit_length() - 1
    row_i = lax.broadcasted_iota(I32, (tq, tq), 0)
    col_i = lax.broadcasted_iota(I32, (tq, tq), 1)
    t_chunk = (qt * tq + col_i) >> shift

    lane = lax.broadcasted_iota(I32, (tq, L), 1)
    qi = qi_ref[0]
    qi_hi = qi.astype(BF16).astype(F32)
    qi_lo = qi - qi_hi
    zero = jnp.zeros((tq, D_I), F32)
    for h in range(H_I):
        sl = slice(h * D_I, (h + 1) * D_I)
        qi3_scr[h] = jnp.concatenate([qi_hi[:, sl], qi_hi[:, sl], qi_lo[:, sl], zero],
                                     axis=1).astype(BF16)
    w_t = (wi_ref[0] * (H_I ** -0.5 * D_I ** -0.5)).T

    def index_block(kb, carry):
        ks = pl.multiple_of(kb * tq, tq)
        ki3 = ki3_ref[0, pl.ds(ks, tq), :]
        score = None
        for h in range(H_I):
            term = w_t[h:h + 1, :] * jnp.maximum(_dot_nt(ki3, qi3_scr[h]), 0.0)
            score = term if score is None else score + term
        key = _sortable(score)
        s_chunk = (ks + row_i) >> shift
        key = jnp.where(s_chunk <= t_chunk, key, INT_MIN)
        key_scr[kb] = key
        hi_scr[kb] = (key >> 16).astype(jnp.int16)
        lo_scr[kb] = ((key & 0xFFFF) + I16_MIN).astype(jnp.int16)
        return carry

    lax.fori_loop(0, nkb, index_block, 0)

    def count(scr, pred, rows, dtype):
        n_acc = 4
        one, nil = jnp.ones((), dtype), jnp.zeros((), dtype)

        def body(kb, accs):
            accs = list(accs)
            for r in range(tq // rows):
                hit = pred(scr[kb, r * rows:(r + 1) * rows, :])
                accs[r % n_acc] = accs[r % n_acc] + jnp.where(hit, one, nil)
            return tuple(accs)
        accs = lax.fori_loop(0, nkb, body, tuple(jnp.zeros((rows, tq), dtype) for _ in range(n_acc)))
        tot = functools.reduce(lambda a, b: a + b, [a.astype(I32) for a in accs])
        return jnp.sum(tot, axis=0, keepdims=True)

    def count16(scr, pred):
        return count(scr, pred, 16, jnp.int16)

    def count32(scr, pred):
        return count(scr, pred, 8, I32)

    def search16(scr, target):
        def step(i, thr):
            cand = thr + lax.shift_left(I32(1), 15 - i)
            cand16 = jnp.broadcast_to(cand.astype(jnp.int16), (16, tq))
            return jnp.where(count16(scr, lambda x: x >= cand16) >= target, cand, thr)
        return lax.fori_loop(0, 16, step, jnp.full((1, tq), I16_MIN, I32))

    t_hi = search16(hi_scr, ksel)
    t_hi16 = jnp.broadcast_to(t_hi.astype(jnp.int16), (16, tq))
    above = count16(hi_scr, lambda x: x > t_hi16)

    def keep_bucket(kb, carry):
        for r in range(tq // 16):
            rows = slice(r * 16, (r + 1) * 16)
            lo_scr[kb, rows, :] = jnp.where(hi_scr[kb, rows, :] == t_hi16, lo_scr[kb, rows, :],
                                            jnp.int16(I16_MIN))
        return carry

    lax.fori_loop(0, nkb, keep_bucket, 0)
    t_lo = search16(lo_scr, ksel - above)
    thr = lax.shift_left(t_hi, 16) + (t_lo - I16_MIN)
    thr8 = jnp.broadcast_to(thr, (8, tq))
    n_ge = count32(key_scr, lambda x: x >= thr8)
    no_thr = thr == INT_MIN
    excess = jnp.where(jnp.logical_or(no_thr, n_ge <= ksel), 0, 1)

    def resolve_ties():
        need = jnp.where(no_thr, 0, ksel - count32(key_scr, lambda x: x > thr8))

        def write(kb, carry):
            tix_scr[kb] = jnp.where(key_scr[kb] == jnp.broadcast_to(thr, (tq, tq)),
                                    kb * tq + row_i, TIE_NONE)
            return carry

        lax.fori_loop(0, nkb, write, 0)

        def step(i, end):
            cand = end + lax.shift_left(I32(1), idx_bits - 1 - i)
            cand8 = jnp.broadcast_to(cand, (8, tq))
            return jnp.where(count32(tix_scr, lambda x: x < cand8) <= need, cand, end)

        return lax.fori_loop(0, idx_bits, step, jnp.zeros((1, tq), I32))

    tie_end = lax.cond(jnp.max(excess) > 0, resolve_ties,
                       lambda: jnp.where(no_thr, 0, TIE_NONE).astype(I32))

    def write_mask(kb, carry):
        key = key_scr[kb]
        thr_b = jnp.broadcast_to(thr, (tq, tq))
        tied = jnp.where(key == thr_b, kb * tq + row_i, TIE_NONE) < jnp.broadcast_to(tie_end, (tq, tq))
        neg_scr[kb] = jnp.where(key > thr_b, 0.0, jnp.where(tied, 0.0, -jnp.inf)).astype(BF16)
        return carry

    lax.fori_loop(0, nkb, write_mask, 0)

    q = q_ref[0].astype(F32)
    for h in range(n_heads):
        qp = q[:, (h // 2) * L:(h // 2 + 1) * L]
        keep = (lane < DH) if h % 2 == 0 else (lane >= DH)
        qz_scr[h] = jnp.where(keep, qp, 0.0).astype(BF16)
        m_scr[h] = jnp.full((1, tq), -1e30, F32)
        l_scr[h] = jnp.zeros((1, tq), F32)
        acc_scr[h] = jnp.zeros((DH, tq), F32)

    rc = 32
    n_rc = tq // rc
    ones = jnp.ones((BF16_ROWS, tq), BF16)

    def scores(kb, h, half):
        ks = pl.multiple_of(kb * tq, tq)
        kp = k_ref[0, pl.ds(ks, tq), (h // 2) * L:(h // 2 + 1) * L]
        x = _dot_nt(kp, qz_scr[h]).astype(BF16) + bias_ref[qt - kb, h] + neg_scr[kb]
        s_scr[half * n_heads + h] = x
        mx_scr[half * n_heads + h] = jnp.max(x.reshape(tq // BF16_ROWS, BF16_ROWS, tq), axis=0)

    for h in range(n_heads):
        scores(0, h, 0)

    def attend_half(kb, cur):
        kb_next = jnp.minimum(kb + 1, nkb - 1)
        for h in range(n_heads):
            scores(kb_next, h, 1 - cur)
            m_old = m_scr[h]
            m_new = jnp.maximum(m_old, jnp.max(mx_scr[cur * n_heads + h].astype(F32),
                                               axis=0, keepdims=True))
            alpha = jnp.exp2(m_old - m_new)
            m_scr[h] = m_new
            m_b = jnp.broadcast_to(m_new.astype(BF16), (rc, tq))
            for r in range(n_rc):
                rows = slice(r * rc, (r + 1) * rc)
                p_scr[cur * n_heads + h, rows, :] = jnp.exp2(
                    s_scr[cur * n_heads + h, rows, :] - m_b)
            vt = jnp.concatenate([vt_ref[0, kb, h * DH:(h + 1) * DH, :], ones], axis=0)
            pv = _dot(vt, p_scr[cur * n_heads + h])
            l_scr[h] = alpha * l_scr[h] + pv[DH:DH + 1, :]
            acc_scr[h] = acc_scr[h] * alpha + pv[:DH, :]

    def attend_block(kb, carry):
        lax.cond(lax.rem(kb, 2) == 0, lambda: attend_half(kb, 0), lambda: attend_half(kb, 1))
        return carry

    lax.fori_loop(0, nkb, attend_block, 0)
    for j in range(n_heads // 2):
        o_t = jnp.concatenate([acc_scr[2 * j] / l_scr[2 * j],
                               acc_scr[2 * j + 1] / l_scr[2 * j + 1]], axis=0)
        o_ref[0, :, j * L:(j + 1) * L] = o_t.T.astype(BF16)


def _dsa_prompt_t(q, qi, wi, kb16, vt16, ki3, bias_t, ksel):
    B, T, aw = q.shape
    n_heads = aw // DH
    tq = bias_t.shape[-1]
    nq = T // tq
    assert T < 2**15 and vt16.shape == (B, nq, aw, tq)
    row = lambda n: pl.BlockSpec((1, tq, n), lambda b, t: (b, t, 0))
    seq = lambda n: pl.BlockSpec((1, T, n), lambda b, t: (b, 0, 0))
    return pl.pallas_call(
        functools.partial(_dsa_prompt_t_body, tq=tq, ksel=ksel, idx_bits=T.bit_length()),
        grid=(B, nq),
        in_specs=[row(aw), row(H_I * D_I), row(H_I), seq(aw),
                  pl.BlockSpec((1, nq, aw, tq), lambda b, t: (b, 0, 0, 0)), seq(4 * D_I),
                  pl.BlockSpec(bias_t.shape, lambda b, t: (0, 0, 0, 0),
                               pipeline_mode=pl.Buffered(1))],
        out_specs=row(aw),
        out_shape=jax.ShapeDtypeStruct((B, T, aw), BF16),
        scratch_shapes=[pltpu.VMEM((nq, tq, tq), I32), pltpu.VMEM((nq, tq, tq), jnp.int16),
                        pltpu.VMEM((nq, tq, tq), jnp.int16), pltpu.VMEM((nq, tq, tq), I32),
                        pltpu.VMEM((nq, tq, tq), BF16), pltpu.VMEM((H_I, tq, 4 * D_I), BF16),
                        pltpu.VMEM((n_heads, tq, V7X_LANES), BF16),
                        pltpu.VMEM((n_heads, 1, tq), F32), pltpu.VMEM((n_heads, 1, tq), F32),
                        pltpu.VMEM((n_heads, DH, tq), F32),
                        pltpu.VMEM((2 * n_heads, tq, tq), BF16),
                        pltpu.VMEM((2 * n_heads, BF16_ROWS, tq), BF16),
                        pltpu.VMEM((2 * n_heads, tq, tq), BF16)],
        compiler_params=_cparams("arbitrary", "arbitrary"),
        name="dsa_prompt",
    )(q, qi, wi, kb16, vt16, ki3, bias_t)


def _pack_heads(q, n_heads):
    T, W = q.shape
    head_of_lane = lax.broadcasted_iota(I32, (T, W), 1) >> (DH.bit_length() - 1)
    qf = q.astype(F32)
    return jnp.concatenate([jnp.where(head_of_lane == h, qf, 0.0) for h in range(n_heads)],
                           axis=0).astype(BF16)


def _unpack_heads(acc, den, n_heads):
    T = acc.shape[0] // n_heads
    W = acc.shape[1]
    head_of_lane = lax.broadcasted_iota(I32, (T, W), 1) >> (DH.bit_length() - 1)
    out = jnp.zeros((T, W), F32)
    for h in range(n_heads):
        blk = acc[h * T:(h + 1) * T, :] / den[h * T:(h + 1) * T, :]
        out = jnp.where(head_of_lane == h, blk, out)
    return out


def _dsa_sample_body(q_ref, qi_ref, wi_ref, kn_ref, vn_ref, kin_ref, kc_ref, vc_ref, kic_ref,
                     bias_ref, biasn_ref, o_ref,
                     key_scr, tix_scr, keyn_scr, tixn_scr, sel_scr, qbd_scr, m_scr, l_scr, acc_scr,
                     *, ts, kc_len, n_new, ksel, idx_bits):
    L = V7X_LANES
    n_heads = qbd_scr.shape[0] // ts
    nc = key_scr.shape[0]
    step = pl.program_id(1)
    past = nc * kc_len
    lane = lax.broadcasted_iota(I32, (ts, L), 1)

    @pl.when(step == 0)
    def _():
        qi = qi_ref[0]
        qi_hm = jnp.concatenate([qi[:, h * D_I:(h + 1) * D_I] for h in range(H_I)], axis=0)
        qi_hi, qi_lo = _split_bf16(qi_hm)
        w = wi_ref[0] * (H_I ** -0.5 * D_I ** -0.5)

        def scores(ki):
            ki_hi, ki_lo = _split_bf16(ki)
            s = _dot_nt(qi_hi, ki_hi) + _dot_nt(qi_hi, ki_lo) + _dot_nt(qi_lo, ki_hi)
            acc = None
            for h in range(H_I):
                term = w[:, h:h + 1] * jnp.maximum(s[h * ts:(h + 1) * ts, :], 0.0)
                acc = term if acc is None else acc + term
            return _sortable(acc)

        for c in range(nc):
            key_scr[c] = scores(kic_ref[0, c * kc_len:(c + 1) * kc_len, :])
        keyn_scr[...] = jnp.where(lane < n_new, scores(kin_ref[0]), INT_MIN)

        def count(pred_key=None, pred_tie=None):
            cnt = jnp.zeros((ts, L), I32)
            for c in range(nc):
                for i in range(kc_len // L):
                    if pred_key is not None:
                        hit = pred_key(key_scr[c, :, i * L:(i + 1) * L])
                    else:
                        hit = pred_tie(tix_scr[c, :, i * L:(i + 1) * L])
                    cnt = cnt + jnp.where(hit, 1, 0)
            hit = pred_key(keyn_scr[...]) if pred_key is not None else pred_tie(tixn_scr[...])
            return _lane_total(cnt + jnp.where(hit, 1, 0))

        def write_ties(thr):
            for c in range(nc):
                for i in range(kc_len // L):
                    idx = c * kc_len + i * L + lane
                    tix_scr[c, :, i * L:(i + 1) * L] = jnp.where(
                        key_scr[c, :, i * L:(i + 1) * L] == thr, idx, TIE_NONE)
            tixn_scr[...] = jnp.where(keyn_scr[...] == thr, past + lane, TIE_NONE)

        thr, tie_end = _topk_threshold(
            lambda x: count(pred_key=lambda k: k >= x),
            lambda x: count(pred_key=lambda k: k > x),
            lambda x: count(pred_tie=lambda t: t < x),
            write_ties, ts, ksel, idx_bits)
        sel_scr[0] = thr
        sel_scr[1] = tie_end
        qbd_scr[...] = _pack_heads(q_ref[0], n_heads)
        m_scr[...] = jnp.full(m_scr.shape, -1e30, F32)
        l_scr[...] = jnp.zeros(l_scr.shape, F32)
        acc_scr[...] = jnp.zeros(acc_scr.shape, F32)

    thr = sel_scr[0]
    tie_end = sel_scr[1]

    def attend(keys, ties, k16, v16, bias):
        n = keys.shape[1]
        negs = []
        for i in range(n // L):
            sel = jnp.logical_or(keys[:, i * L:(i + 1) * L] > thr,
                                 ties[:, i * L:(i + 1) * L] < tie_end)
            negs.append(jnp.where(sel, 0.0, -jnp.inf))
        neg = jnp.concatenate(negs, axis=1) if len(negs) > 1 else negs[0]
        neg = jnp.concatenate([neg] * n_heads, axis=0)
        s = _dot_nt(qbd_scr[...], k16) + bias + neg
        m_old = m_scr[...]
        m_new = jnp.maximum(m_old, jnp.max(s, axis=1, keepdims=True))
        alpha = jnp.exp(m_old - m_new)
        p = jnp.exp(s - m_new)
        l_scr[...] = alpha * l_scr[...] + jnp.sum(p, axis=1, keepdims=True)
        m_scr[...] = m_new
        acc_scr[...] = acc_scr[...] * alpha + _dot(p.astype(BF16), v16)

    attend(key_scr[step], tix_scr[step], kc_ref[0].astype(BF16), vc_ref[0].astype(BF16),
           bias_ref[...])

    @pl.when(step == nc - 1)
    def _():
        attend(keyn_scr[...], tixn_scr[...], kn_ref[0], vn_ref[0], biasn_ref[...])
        o_ref[0] = _unpack_heads(acc_scr[...], l_scr[...], n_heads).astype(BF16)


def _dsa_sample(q, qi, wi, kn16, vn16, kin, cache_k, cache_v, cache_ki, bias, n_new, ksel):
    B, ts, aw = q.shape
    n_heads = aw // DH
    P = cache_k.shape[1]
    npad = kn16.shape[1]
    kc_len = min(P, 1024)
    nc = P // kc_len
    L = V7X_LANES
    per_b = lambda a: pl.BlockSpec((1,) + a.shape[1:], lambda b, c: (b, 0, 0))
    chunk = lambda n: pl.BlockSpec((1, kc_len, n), lambda b, c: (b, c, 0))
    rows = n_heads * ts
    return pl.pallas_call(
        functools.partial(_dsa_sample_body, ts=ts, kc_len=kc_len, n_new=n_new, ksel=ksel,
                          idx_bits=(P + npad).bit_length()),
        grid=(B, nc),
        in_specs=[per_b(q), per_b(qi), per_b(wi), per_b(kn16), per_b(vn16), per_b(kin),
                  chunk(aw), chunk(aw), per_b(cache_ki),
                  pl.BlockSpec((rows, kc_len), lambda b, c: (0, c)),
                  pl.BlockSpec((rows, npad), lambda b, c: (0, P // npad))],
        out_specs=pl.BlockSpec((1, ts, aw), lambda b, c: (b, 0, 0)),
        out_shape=jax.ShapeDtypeStruct((B, ts, aw), BF16),
        scratch_shapes=[pltpu.VMEM((nc, ts, kc_len), I32), pltpu.VMEM((nc, ts, kc_len), I32),
                        pltpu.VMEM((ts, npad), I32), pltpu.VMEM((ts, npad), I32),
                        pltpu.VMEM((2, ts, L), I32), pltpu.VMEM((rows, aw), BF16),
                        pltpu.VMEM((rows, 1), F32), pltpu.VMEM((rows, 1), F32),
                        pltpu.VMEM((rows, aw), F32)],
        compiler_params=_cparams("arbitrary", "arbitrary"),
        name="dsa_sample",
    )(q, qi, wi, kn16, vn16, kin, cache_k, cache_v, cache_ki, bias, bias)


def _band_prompt_body(q_ref, k_ref, vt_ref, bias_ref, o_ref, qz_scr, s_scr, p_scr, ot_scr,
                      *, tq, n_back):
    L = V7X_LANES
    n_heads = q_ref.shape[-1] // DH
    n_slots = s_scr.shape[0]
    n_p = p_scr.shape[0]
    nb = n_back + 1
    rc = 32
    n_rc = tq // rc
    qt = pl.program_id(1)
    lane = lax.broadcasted_iota(I32, (tq, L), 1)
    n_tiles = bias_ref.shape[0]
    blocks, starts, exists = [], [], []
    for j in range(nb):
        kb = qt - n_back + j
        blocks.append(jnp.maximum(kb, 0))
        starts.append(pl.multiple_of(jnp.maximum(kb, 0) * tq, tq))
        exists.append(kb >= 0)
    q = q_ref[0].astype(F32)
    for h in range(n_heads):
        qp = q[:, (h // 2) * L:(h // 2 + 1) * L]
        keep = (lane < DH) if h % 2 == 0 else (lane >= DH)
        qz_scr[h] = jnp.where(keep, qp, 0.0).astype(BF16)

    def scores(h):
        mx = None
        for j in range(nb):
            kp = k_ref[0, pl.ds(starts[j], tq), (h // 2) * L:(h // 2 + 1) * L]
            tile = jnp.where(exists[j], h * nb + j, n_tiles - 1)
            x = _dot_nt(kp, qz_scr[h]).astype(BF16) + bias_ref[tile]
            s_scr[h % n_slots, j] = x
            xm = jnp.max(x.reshape(tq // BF16_ROWS, BF16_ROWS, tq), axis=0)
            mx = xm if mx is None else jnp.maximum(mx, xm)
        return mx

    ones = jnp.ones((BF16_ROWS, tq), BF16)
    maxes = [scores(h) for h in range(n_slots - 1)]
    for h in range(n_heads):
        if h + n_slots - 1 < n_heads:
            maxes.append(scores(h + n_slots - 1))
        slot = h % n_slots
        m = jnp.max(maxes[h].astype(F32), axis=0, keepdims=True)
        m_b = jnp.broadcast_to(m.astype(BF16), (rc, tq))
        out = None
        for j in range(nb):
            for r in range(n_rc):
                rows = slice(r * rc, (r + 1) * rc)
                p_scr[h % n_p, j, rows, :] = jnp.exp2(s_scr[slot, j, rows, :] - m_b)
            vt = jnp.concatenate([vt_ref[0, blocks[j], h * DH:(h + 1) * DH, :], ones], axis=0)
            oj = _dot(vt, p_scr[h % n_p, j])
            out = oj if out is None else out + oj
        ot_scr[h * DH:(h + 1) * DH, :] = out[:DH, :] / out[DH:DH + 1, :]
    for pair in range(n_heads // 2):
        o_ref[0, :, pair * L:(pair + 1) * L] = ot_scr[pair * L:(pair + 1) * L, :].T.astype(BF16)


def _band_prompt(q, kb16, vt16, bias_t):
    B, T, cw = q.shape
    n_heads = cw // DH
    tq = bias_t.shape[-1]
    nq = T // tq
    n_back = (bias_t.shape[0] - 1) // n_heads - 1
    assert vt16.shape == (B, nq, cw, tq)
    row = pl.BlockSpec((1, tq, cw), lambda b, t: (b, t, 0))
    return pl.pallas_call(
        functools.partial(_band_prompt_body, tq=tq, n_back=n_back),
        grid=(B, nq),
        in_specs=[row, pl.BlockSpec((1, T, cw), lambda b, t: (b, 0, 0)),
                  pl.BlockSpec((1, nq, cw, tq), lambda b, t: (b, 0, 0, 0)),
                  pl.BlockSpec(bias_t.shape, lambda b, t: (0, 0, 0),
                               pipeline_mode=pl.Buffered(1))],
        out_specs=row,
        out_shape=jax.ShapeDtypeStruct((B, T, cw), BF16),
        scratch_shapes=[pltpu.VMEM((n_heads, tq, V7X_LANES), BF16),
                        pltpu.VMEM((6, n_back + 1, tq, tq), BF16),
                        pltpu.VMEM((6, n_back + 1, tq, tq), BF16),
                        pltpu.VMEM((cw, tq), F32)],
        compiler_params=_cparams("arbitrary", "arbitrary"),
        name="band_prompt",
    )(q, kb16, vt16, bias_t)


def _band_sample_body(q_ref, kn_ref, vn_ref, kc_ref, vc_ref, bias_ref, biasn_ref, o_ref):
    n_heads = q_ref.shape[-1] // DH
    qbd = _pack_heads(q_ref[0], n_heads)
    s_c = _dot_nt(qbd, kc_ref[0].astype(BF16)) + bias_ref[...]
    s_n = _dot_nt(qbd, kn_ref[0]) + biasn_ref[...]
    m = jnp.maximum(jnp.max(s_c, axis=1, keepdims=True), jnp.max(s_n, axis=1, keepdims=True))
    p_c = jnp.exp(s_c - m)
    p_n = jnp.exp(s_n - m)
    den = jnp.sum(p_c, axis=1, keepdims=True) + jnp.sum(p_n, axis=1, keepdims=True)
    acc = _dot(p_c.astype(BF16), vc_ref[0].astype(BF16)) + _dot(p_n.astype(BF16), vn_ref[0])
    o_ref[0] = _unpack_heads(acc, den, n_heads).astype(BF16)


def _band_sample(q, kn16, vn16, cache_k, cache_v, bias):
    B, ts, cw = q.shape
    n_heads = cw // DH
    cb = cache_k.shape[1]
    npad = kn16.shape[1]
    rows = n_heads * ts
    per_b = lambda a: pl.BlockSpec((1,) + a.shape[1:], lambda b: (b, 0, 0))
    return pl.pallas_call(
        _band_sample_body,
        grid=(B,),
        in_specs=[per_b(q), per_b(kn16), per_b(vn16), per_b(cache_k), per_b(cache_v),
                  pl.BlockSpec((rows, cb), lambda b: (0, 0)),
                  pl.BlockSpec((rows, npad), lambda b: (0, cb // npad))],
        out_specs=per_b(q),
        out_shape=jax.ShapeDtypeStruct((B, ts, cw), BF16),
        compiler_params=_cparams("arbitrary"),
        name="band_sample",
    )(q, kn16, vn16, cache_k, cache_v, bias, bias)


def _pad_rows(a, n):
    return jnp.pad(a, ((0, 0), (0, n - a.shape[1]), (0, 0)))


def kernel(x_prompt, x_sample, c_prompt, c_sample, cache_a_k, cache_a_v, cache_a_kidx, state_b,
           cache_c_k, cache_c_v, w_ada, b_ada, norm_mix, norm_ffn, norm_final, w_in_even,
           w_out_even, t5_table, lb_logits, norm_b_out, w_in_odd, w_out_odd, rel_table_c,
           w_ffn_gate, w_ffn_up, w_ffn_down):
    Bp, T, D = x_prompt.shape
    Bs, Ts, _ = x_sample.shape
    P = cache_a_k.shape[2]
    c_buf = cache_c_k.shape[2]
    depth = w_ada.shape[0]
    L = V7X_LANES
    n_heads_a = cache_a_k.shape[3]
    aw = n_heads_a * DH
    n_heads_c = cache_c_k.shape[3]
    bw = w_out_even.shape[1] - aw
    iw = H_I * D_I
    ksel_p = min(K_TOP, T // 4)
    ksel_s = min(K_TOP, (P + Ts) // 4)
    c_keep = min(C_BACK * CHUNK, T)
    tq = min(T, 256)
    npad = L
    assert Ts <= npad and P % npad == 0 and c_buf % npad == 0 and T % tq == 0

    mod = _adaln_all(jnp.concatenate([c_prompt, c_sample], axis=0), w_ada, b_ada)

    def mods(l):
        parts = jnp.split(mod[l], 6, axis=-1)
        return ([p[:Bp].reshape(Bp, 1, D) for p in parts],
                [p[Bp:].reshape(Bs, 1, D) for p in parts])

    nq = T // tq
    t5_prompt = _bias_tiles(
        t5_table, groups=nq * n_heads_a, rows=tq, cols=tq, idx_fn=_t5_bucket,
        group_fn=lambda g: (g % n_heads_a, -(g // n_heads_a) * tq), keys_on_rows=True,
    ).reshape(nq, n_heads_a, tq, tq)
    t5_sample = _bias_tiles(
        t5_table, groups=n_heads_a, rows=Ts, cols=P + npad, idx_fn=_t5_bucket,
        group_fn=lambda g: (g, -P),
    ).reshape(n_heads_a * Ts, P + npad)

    xp, xs = x_prompt, x_sample
    outs_p = {k: [] for k in ("ak", "av", "ai", "bs", "ck", "cv")}
    outs_s = {k: [] for k in ("ak", "av", "ai", "bs", "ck", "cv")}
    for l in range(depth):
        (sh1p, sc1p, g1p, sh2p, sc2p, g2p), (sh1s, sc1s, g1s, sh2s, sc2s, g2s) = mods(l)
        gm = norm_mix[l].reshape(1, D)
        gn = norm_ffn[l].reshape(1, D)
        wg = w_ffn_gate[l].astype(BF16)
        wu = w_ffn_up[l].astype(BF16)
        wd = w_ffn_down[l].astype(BF16)
        gf = norm_final.reshape(1, D) if l == depth - 1 else None
        if l % 2 == 0:
            e = l // 2
            w_in = w_in_even[e]
            wa = w_in[:, :3 * aw].astype(BF16)
            wi_cols = w_in[:, 3 * aw:3 * aw + iw + D_I + H_I]
            wi_cols = jnp.pad(wi_cols, ((0, 0), (0, -wi_cols.shape[1] % L)))
            wh = wi_cols.astype(BF16)
            wl = (wi_cols - wh.astype(F32)).astype(BF16)
            wb = w_in[:, 3 * aw + iw + D_I + H_I:].astype(BF16)
            w_out = w_out_even[e].astype(BF16)
            gb_norm = norm_b_out[e].reshape(1, bw)

            q, k, v, k16, vt16, qi, ki, ki3, wi, zb = _in_even(xp, sc1p, sh1p, gm, wa, wh, wl, wb,
                                                              keys_on_rows=True)
            oa = _dsa_prompt_t(q, qi, wi, k16, vt16, ki3, t5_prompt, ksel_p)
            ob, st = _hgrn(zb, lb_logits, gb_norm, None, e)
            outs_p["ak"].append(k.reshape(Bp, T, n_heads_a, DH))
            outs_p["av"].append(v.reshape(Bp, T, n_heads_a, DH))
            outs_p["ai"].append(ki)
            outs_p["bs"].append(st)
            xp = _mix_ffn(xp, g1p, sc2p, sh2p, g2p, gn, [oa, ob], [w_out[:aw], w_out[aw:]],
                          wg, wu, wd, gf)

            q, k, v, k16, v16, qi, ki, _, wi, zb = _in_even(xs, sc1s, sh1s, gm, wa, wh, wl, wb,
                                                            keys_on_rows=False)
            oa = _dsa_sample(q, qi, wi, _pad_rows(k16, npad), _pad_rows(v16, npad),
                             _pad_rows(ki, npad), cache_a_k[e].reshape(Bs, P, aw),
                             cache_a_v[e].reshape(Bs, P, aw), cache_a_kidx[e], t5_sample,
                             Ts, ksel_s)
            ob, st = _hgrn(zb, lb_logits, gb_norm, state_b[e], e)
            outs_s["ak"].append(k.reshape(Bs, Ts, n_heads_a, DH))
            outs_s["av"].append(v.reshape(Bs, Ts, n_heads_a, DH))
            outs_s["ai"].append(ki)
            outs_s["bs"].append(st)
            xs = _mix_ffn(xs, g1s, sc2s, sh2s, g2s, gn, [oa, ob], [w_out[:aw], w_out[aw:]],
                          wg, wu, wd, gf)
        else:
            o = l // 2
            cw = n_heads_c * DH
            w_in = w_in_odd[o].astype(BF16)
            w_out = w_out_odd[o].astype(BF16)
            n_back = -(-(C_BACK * CHUNK) // tq)
            shift = CHUNK.bit_length() - 1

            n_band = n_heads_c * (n_back + 1)

            def band_mask(g, t, s, n_back=n_back, n_band=n_band):
                j = g % (n_back + 1)
                kc = (s + (j - n_back) * tq) >> shift
                tc = t >> shift
                return (kc <= tc) & (kc >= tc - C_BACK) & (g < n_band)

            band_p = _bias_tiles(
                rel_table_c[o], groups=n_band + 1, rows=tq, cols=tq, idx_fn=_clip_index,
                group_fn=lambda g, n_back=n_back: (jnp.minimum(g // (n_back + 1), n_heads_c - 1),
                                                   (g % (n_back + 1) - n_back) * tq),
                mask_fn=band_mask, keys_on_rows=True)
            band_s = _bias_tiles(
                rel_table_c[o], groups=n_heads_c, rows=Ts, cols=c_buf + npad, idx_fn=_clip_index,
                group_fn=lambda g: (g, -c_buf),
                mask_fn=lambda g, t, s: s < c_buf + Ts,
            ).reshape(n_heads_c * Ts, c_buf + npad)

            q, k, v, k16, vt16 = _in_odd(xp, sc1p, sh1p, gm, w_in, keys_on_rows=True)
            op = _band_prompt(q, k16, vt16, band_p)
            outs_p["ck"].append(k[:, T - c_keep:].reshape(Bp, c_keep, n_heads_c, DH))
            outs_p["cv"].append(v[:, T - c_keep:].reshape(Bp, c_keep, n_heads_c, DH))
            xp = _mix_ffn(xp, g1p, sc2p, sh2p, g2p, gn, [op], [w_out], wg, wu, wd, gf)

            q, k, v, k16, v16 = _in_odd(xs, sc1s, sh1s, gm, w_in, keys_on_rows=False)
            osm = _band_sample(q, _pad_rows(k16, npad), _pad_rows(v16, npad),
                               cache_c_k[o].reshape(Bs, c_buf, cw),
                               cache_c_v[o].reshape(Bs, c_buf, cw), band_s)
            outs_s["ck"].append(k.reshape(Bs, Ts, n_heads_c, DH))
            outs_s["cv"].append(v.reshape(Bs, Ts, n_heads_c, DH))
            xs = _mix_ffn(xs, g1s, sc2s, sh2s, g2s, gn, [osm], [w_out], wg, wu, wd, gf)

    st = lambda xs_: jnp.stack(xs_)
    return (xp, xs, st(outs_p["ak"]), st(outs_p["av"]), st(outs_p["ai"]), st(outs_p["bs"]),
            st(outs_p["ck"]), st(outs_p["cv"]), st(outs_s["ak"]), st(outs_s["av"]),
            st(outs_s["ai"]), st(outs_s["bs"]), st(outs_s["ck"]), st(outs_s["cv"]))
```

```python
import functools
import math

import jax
import jax.numpy as jnp
from jax import lax
from jax.experimental import pallas as pl
from jax.experimental.pallas import tpu as pltpu

F32, BF16, I32 = jnp.float32, jnp.bfloat16, jnp.int32

CHUNK = 64
EPS = 1e-6
DH = 64
H_I = 8
D_I = 64
K_TOP = 256
N_BUCKETS = 32
T5_MAX_DIST = 1024
DK_B = 128
DV_B = 128
C_BACK = 8
REL_CLIP = 128

V7X_LANES = 128
BF16_ROWS = 16
V7X_VMEM_LIMIT = 56 * 2**20

INT_MIN = -(2**31)
TIE_NONE = 2**30
HIGHEST = lax.Precision.HIGHEST


def _cparams(*sem):
    return pltpu.CompilerParams(dimension_semantics=sem, vmem_limit_bytes=V7X_VMEM_LIMIT)


def _dot(a, b):
    return jnp.dot(a, b, preferred_element_type=F32)


def _dot_nt(a, b):
    return lax.dot_general(a, b, (((1,), (1,)), ((), ())), preferred_element_type=F32)


def _split_bf16(x):
    hi = x.astype(BF16)
    lo = (x - hi.astype(F32)).astype(BF16)
    return hi, lo


def _silu(x):
    return x * jax.nn.sigmoid(x)


def _norm_mod(x, g, scale, shift):
    ms = jnp.mean(x * x, axis=-1, keepdims=True)
    return (x * lax.rsqrt(ms + EPS) * g) * (1.0 + scale) + shift


def _sortable(x):
    b = pltpu.bitcast(x + 0.0, I32)
    return b ^ ((b >> 31) & I32(0x7FFFFFFF))


LOG2E = math.log2(math.e)


def _q_scale(base2):
    return DH ** -0.5 * (LOG2E if base2 else 1.0)


def _row_tiles(B, T, rows=256):
    tt = min(T, rows)
    bb = max(1, min(B, rows // tt))
    assert T % tt == 0 and B % bb == 0
    return bb, tt


def _ada_body(c_ref, w_ref, b_ref, o_ref):
    s = _silu(c_ref[...])
    o_ref[0] = jnp.dot(s, w_ref[0], precision=HIGHEST, preferred_element_type=F32) + b_ref[0]


def _adaln_all(c_all, w_ada, b_ada):
    depth, D, E = w_ada.shape
    R = c_all.shape[0]
    tn = 1024
    return pl.pallas_call(
        _ada_body,
        grid=(depth, E // tn),
        in_specs=[pl.BlockSpec((R, D), lambda l, n: (0, 0)),
                  pl.BlockSpec((1, D, tn), lambda l, n: (l, 0, n)),
                  pl.BlockSpec((1, 1, tn), lambda l, n: (l, 0, n))],
        out_specs=pl.BlockSpec((1, R, tn), lambda l, n: (l, 0, n)),
        out_shape=jax.ShapeDtypeStruct((depth, R, E), F32),
        compiler_params=_cparams("arbitrary", "arbitrary"),
        name="adaln",
    )(c_all, w_ada, b_ada.reshape(depth, 1, E))


def _t5_bucket(rel):
    nb = N_BUCKETS // 2
    max_exact = nb // 2
    base = jnp.where(rel > 0, nb, 0)
    n = jnp.abs(rel)
    large = max_exact + (jnp.log(jnp.maximum(n, 1).astype(F32) / max_exact)
                         / math.log(T5_MAX_DIST / max_exact) * (nb - max_exact)).astype(I32)
    large = jnp.minimum(large, nb - 1)
    return base + jnp.where(n < max_exact, n, large)


def _clip_index(rel):
    return jnp.clip(-rel, -REL_CLIP, REL_CLIP) + REL_CLIP


def _bias_body(tab_ref, o_ref, *, rows, cols, width, n_tab, idx_fn, group_fn, mask_fn,
               keys_on_rows):
    g = pl.program_id(0)
    head, off = group_fn(g)
    i = lax.broadcasted_iota(I32, (1, width), 1)
    diff = jnp.where(i < width - rows, i, i - width)
    idx = idx_fn(off - diff if keys_on_rows else off + diff)
    val = jnp.zeros((1, width), F32)
    for b in range(n_tab):
        val = jnp.where(idx == b, tab_ref[head, b], val)
    x = jnp.broadcast_to(val, (rows, width))
    y = pltpu.roll(x, 0, 1, stride=1, stride_axis=0)
    tile = y[:, :cols]
    if keys_on_rows:
        tile = tile * LOG2E
    if mask_fn is not None:
        t = lax.broadcasted_iota(I32, (rows, cols), 1 if keys_on_rows else 0)
        s = lax.broadcasted_iota(I32, (rows, cols), 0 if keys_on_rows else 1)
        tile = jnp.where(mask_fn(g, t, s), tile, -jnp.inf)
    o_ref[0] = tile.astype(o_ref.dtype)


def _bias_tiles(table, *, groups, rows, cols, idx_fn, group_fn, mask_fn=None,
                keys_on_rows=False):
    n_tab = table.shape[0]
    width = -(-(rows + cols) // V7X_LANES) * V7X_LANES
    body = functools.partial(_bias_body, rows=rows, cols=cols, width=width, n_tab=n_tab,
                             idx_fn=idx_fn, group_fn=group_fn, mask_fn=mask_fn,
                             keys_on_rows=keys_on_rows)
    return pl.pallas_call(
        body,
        grid=(groups,),
        in_specs=[pl.BlockSpec(memory_space=pltpu.SMEM)],
        out_specs=pl.BlockSpec((1, rows, cols), lambda g: (g, 0, 0)),
        out_shape=jax.ShapeDtypeStruct((groups, rows, cols), BF16 if keys_on_rows else F32),
        compiler_params=_cparams("arbitrary"),
        name="bias_tiles",
    )(table.T)


def _in_even_body(x_ref, sc_ref, sh_ref, g_ref, wa_ref, wh_ref, wl_ref, wb_ref,
                  q_ref, k_ref, v_ref, kb_ref, vb_ref, qi_ref, ki_ref, ki3_ref, wi_ref, zb_ref,
                  *, aw, keys_on_rows):
    bb, tt, D = x_ref.shape
    M = bb * tt
    h = _norm_mod(x_ref[...], g_ref[...].reshape(1, 1, D), sc_ref[...], sh_ref[...]).reshape(M, D)
    h_hi, h_lo = _split_bf16(h)
    za = _dot(h_hi, wa_ref[...])
    q_ref[...] = (za[:, :aw] * _q_scale(keys_on_rows)).astype(BF16).reshape(bb, tt, aw)
    k = za[:, aw:2 * aw]
    v = za[:, 2 * aw:3 * aw]
    k_ref[...] = k.reshape(bb, tt, aw)
    v_ref[...] = v.reshape(bb, tt, aw)
    kb_ref[...] = k.astype(BF16).reshape(bb, tt, aw)
    if keys_on_rows:
        vb_ref[0, 0] = v.T.astype(BF16)
    else:
        vb_ref[...] = v.astype(BF16).reshape(bb, tt, aw)
    wh = wh_ref[...]
    zi = _dot(h_hi, wh) + _dot(h_hi, wl_ref[...]) + _dot(h_lo, wh)
    iw = H_I * D_I
    ki = zi[:, iw:iw + D_I]
    qi_ref[...] = zi[:, :iw].reshape(bb, tt, iw)
    ki_ref[...] = ki.reshape(bb, tt, D_I)
    ki_hi = ki.astype(BF16).astype(F32)
    ki3 = jnp.concatenate([ki_hi, ki - ki_hi, ki_hi, jnp.zeros_like(ki)], axis=1)
    ki3_ref[...] = ki3.astype(BF16).reshape(bb, tt, 4 * D_I)
    wi_ref[...] = zi[:, iw + D_I:iw + D_I + H_I].reshape(bb, tt, H_I)
    zb_ref[...] = _dot(h_hi, wb_ref[...]).reshape(bb, tt, zb_ref.shape[-1])


def _in_even(x, scale, shift, g, wa, wh, wl, wb, keys_on_rows):
    B, T, D = x.shape
    aw = wa.shape[1] // 3
    bw = wb.shape[1]
    iw = H_I * D_I
    bb, tt = _row_tiles(B, T)
    assert not keys_on_rows or bb == 1
    row = lambda n: pl.BlockSpec((bb, tt, n), lambda b, t: (b, t, 0))
    mod = pl.BlockSpec((bb, 1, D), lambda b, t: (b, 0, 0))
    full = lambda a: pl.BlockSpec(a.shape, lambda b, t: (0,) * a.ndim)
    outs = [(aw, BF16), (aw, F32), (aw, F32), (aw, BF16), (aw, BF16), (iw, F32), (D_I, F32),
            (4 * D_I, BF16), (H_I, F32), (bw, F32)]
    out_specs = [row(n) for n, _ in outs]
    out_shape = [jax.ShapeDtypeStruct((B, T, n), dt) for n, dt in outs]
    if keys_on_rows:
        out_specs[4] = pl.BlockSpec((1, 1, aw, tt), lambda b, t: (b, t, 0, 0))
        out_shape[4] = jax.ShapeDtypeStruct((B, T // tt, aw, tt), BF16)
    return pl.pallas_call(
        functools.partial(_in_even_body, aw=aw, keys_on_rows=keys_on_rows),
        grid=(B // bb, T // tt),
        in_specs=[row(D), mod, mod, full(g), full(wa), full(wh), full(wl), full(wb)],
        out_specs=out_specs,
        out_shape=out_shape,
        compiler_params=_cparams("arbitrary", "arbitrary"),
        name="in_even",
    )(x, scale, shift, g, wa, wh, wl, wb)


def _in_odd_body(x_ref, sc_ref, sh_ref, g_ref, w_ref, q_ref, k_ref, v_ref, kb_ref, vb_ref,
                 *, keys_on_rows):
    bb, tt, D = x_ref.shape
    M = bb * tt
    cw = q_ref.shape[-1]
    h = _norm_mod(x_ref[...], g_ref[...].reshape(1, 1, D), sc_ref[...], sh_ref[...]).reshape(M, D)
    z = _dot(h.astype(BF16), w_ref[...])
    q_ref[...] = (z[:, :cw] * _q_scale(keys_on_rows)).astype(BF16).reshape(bb, tt, cw)
    k = z[:, cw:2 * cw]
    v = z[:, 2 * cw:]
    k_ref[...] = k.reshape(bb, tt, cw)
    v_ref[...] = v.reshape(bb, tt, cw)
    kb_ref[...] = k.astype(BF16).reshape(bb, tt, cw)
    if keys_on_rows:
        vb_ref[0, 0] = v.T.astype(BF16)
    else:
        vb_ref[...] = v.astype(BF16).reshape(bb, tt, cw)


def _in_odd(x, scale, shift, g, w, keys_on_rows):
    B, T, D = x.shape
    cw = w.shape[1] // 3
    bb, tt = _row_tiles(B, T)
    assert not keys_on_rows or bb == 1
    row = lambda n: pl.BlockSpec((bb, tt, n), lambda b, t: (b, t, 0))
    mod = pl.BlockSpec((bb, 1, D), lambda b, t: (b, 0, 0))
    full = lambda a: pl.BlockSpec(a.shape, lambda b, t: (0,) * a.ndim)
    dts = [BF16, F32, F32, BF16, BF16]
    out_specs = [row(cw) for _ in dts]
    out_shape = [jax.ShapeDtypeStruct((B, T, cw), dt) for dt in dts]
    if keys_on_rows:
        out_specs[4] = pl.BlockSpec((1, 1, cw, tt), lambda b, t: (b, t, 0, 0))
        out_shape[4] = jax.ShapeDtypeStruct((B, T // tt, cw, tt), BF16)
    return pl.pallas_call(
        functools.partial(_in_odd_body, keys_on_rows=keys_on_rows),
        grid=(B // bb, T // tt),
        in_specs=[row(D), mod, mod, full(g), full(w)],
        out_specs=out_specs,
        out_shape=out_shape,
        compiler_params=_cparams("arbitrary", "arbitrary"),
        name="in_odd",
    )(x, scale, shift, g, w)


def _mix_ffn_body(*refs, n_mix, final):
    x_ref, g1_ref, sc2_ref, sh2_ref, g2_ref, gn_ref = refs[:6]
    mix_refs = refs[6:6 + n_mix]
    wmix_refs = refs[6 + n_mix:6 + 2 * n_mix]
    wg_ref, wu_ref, wd_ref = refs[6 + 2 * n_mix:9 + 2 * n_mix]
    gf_ref = refs[9 + 2 * n_mix] if final else None
    o_ref = refs[-1]
    bb, tt, D = x_ref.shape
    M = bb * tt
    m = None
    for mr, wr in zip(mix_refs, wmix_refs):
        part = _dot(mr[...].reshape(M, mr.shape[-1]), wr[...])
        m = part if m is None else m + part
    x1 = x_ref[...] + g1_ref[...] * m.reshape(bb, tt, D)
    h2 = _norm_mod(x1, gn_ref[...].reshape(1, 1, D), sc2_ref[...], sh2_ref[...])
    h2 = h2.reshape(M, D).astype(BF16)
    a = _dot(h2, wg_ref[...])
    b = _dot(h2, wu_ref[...])
    u = (_silu(a) * b).astype(BF16)
    f = _dot(u, wd_ref[...])
    x2 = x1 + g2_ref[...] * f.reshape(bb, tt, D)
    if final:
        ms = jnp.mean(x2 * x2, axis=-1, keepdims=True)
        x2 = x2 * lax.rsqrt(ms + EPS) * gf_ref[...].reshape(1, 1, D)
    o_ref[...] = x2


def _mix_ffn(x, g1, sc2, sh2, g2, gn, mixes, wmixes, wg, wu, wd, gf=None):
    B, T, D = x.shape
    bb, tt = _row_tiles(B, T)
    row = lambda n: pl.BlockSpec((bb, tt, n), lambda b, t: (b, t, 0))
    mod = pl.BlockSpec((bb, 1, D), lambda b, t: (b, 0, 0))
    full = lambda a: pl.BlockSpec(a.shape, lambda b, t: (0,) * a.ndim,
                                  pipeline_mode=pl.Buffered(1))
    final = gf is not None
    args = [x, g1, sc2, sh2, g2, gn, *mixes, *wmixes, wg, wu, wd] + ([gf] if final else [])
    specs = ([row(D), mod, mod, mod, mod, full(gn)] + [row(m.shape[-1]) for m in mixes]
             + [full(w) for w in wmixes] + [full(wg), full(wu), full(wd)]
             + ([full(gf)] if final else []))
    return pl.pallas_call(
        functools.partial(_mix_ffn_body, n_mix=len(mixes), final=final),
        grid=(B // bb, T // tt),
        in_specs=specs,
        out_specs=row(D),
        out_shape=jax.ShapeDtypeStruct((B, T, D), F32),
        compiler_params=_cparams("arbitrary", "arbitrary"),
        name="mix_ffn",
    )(*args)


def _hgrn_body(*refs, C, n_sub, e, has_s0):
    if has_s0:
        z_ref, lbl_ref, gn_ref, s0_ref, ob_ref, sout_ref, st_scr = refs
    else:
        z_ref, lbl_ref, gn_ref, ob_ref, sout_ref, st_scr = refs
    step = pl.program_id(1)
    n_heads = st_scr.shape[0]
    kw = n_heads * DK_B

    @pl.when(step == 0)
    def _():
        for h in range(n_heads):
            st_scr[h] = s0_ref[0, h].T if has_s0 else jnp.zeros((DV_B, DK_B), F32)

    ll = lbl_ref[...]
    ex = jnp.exp(ll - jnp.max(ll, axis=0, keepdims=True))
    sm = ex / jnp.sum(ex, axis=0, keepdims=True)
    lb = jnp.sum(sm[:e + 1], axis=0, keepdims=True)
    r = lax.broadcasted_iota(I32, (C, C), 0)
    c = lax.broadcasted_iota(I32, (C, C), 1)
    causal = r >= c
    tri = jnp.where(causal, 1.0, 0.0)
    gn = gn_ref[...]
    heads = [(slice(h * DK_B, (h + 1) * DK_B), slice(h * DV_B, (h + 1) * DV_B))
             for h in range(n_heads)]
    intra, q_sts, decs, updates, gates = [], [], [], [], []
    for sub in range(n_sub):
        z = z_ref[0, sub * C:(sub + 1) * C, :]
        qh = _silu(z[:, :kw])
        f = lb + (1.0 - lb) * jax.nn.sigmoid(z[:, kw:2 * kw])
        kh = 1.0 - f
        vh = z[:, 2 * kw:3 * kw]
        gates.append(_silu(z[:, 3 * kw:]))
        G = jnp.dot(tri, jnp.log(f), precision=HIGHEST, preferred_element_type=F32)
        g_end = G[C - 1:C, :]
        mid = 0.5 * g_end
        q_in = (qh * jnp.exp(G - mid)).astype(BF16)
        k_in = (kh * jnp.exp(mid - G)).astype(BF16)
        k_st = (kh * jnp.exp(g_end - G)).astype(BF16)
        q_sts.append((qh * jnp.exp(G)).astype(BF16))
        decs.append(jnp.exp(g_end))
        o_sub, u_sub = [], []
        for sk, sv in heads:
            att = jnp.where(causal, _dot_nt(q_in[:, sk], k_in[:, sk]), 0.0)
            v_h = vh[:, sv]
            o_sub.append(_dot(att.astype(BF16), v_h.astype(BF16)))
            u_sub.append(_dot(v_h.T.astype(BF16), k_st[:, sk]))
        intra.append(o_sub)
        updates.append(u_sub)
    states = [st_scr[h] for h in range(n_heads)]
    for sub in range(n_sub):
        outs = []
        for h, (sk, sv) in enumerate(heads):
            o = intra[sub][h] + _dot_nt(q_sts[sub][:, sk], states[h].astype(BF16))
            states[h] = states[h] * decs[sub][:, sk] + updates[sub][h]
            outs.append(o * lax.rsqrt(jnp.mean(o * o, axis=-1, keepdims=True) + EPS))
        ob = jnp.concatenate(outs, axis=1) * gn * gates[sub]
        ob_ref[0, sub * C:(sub + 1) * C, :] = ob.astype(BF16)
    for h in range(n_heads):
        st_scr[h] = states[h]

    @pl.when(step == pl.num_programs(1) - 1)
    def _():
        for h in range(n_heads):
            sout_ref[0, h] = st_scr[h].T


def _hgrn(zb, lb_logits, g_norm, s0, e):
    B, T, zw = zb.shape
    kw = zw // 4
    n_heads = kw // DK_B
    C = min(CHUNK, T)
    ct = min(T, 4 * C)
    has_s0 = s0 is not None
    full = lambda a: pl.BlockSpec(a.shape, lambda b, t: (0,) * a.ndim)
    state = pl.BlockSpec((1, n_heads, DK_B, DV_B), lambda b, t: (b, 0, 0, 0))
    args = [zb, lb_logits, g_norm] + ([s0] if has_s0 else [])
    specs = ([pl.BlockSpec((1, ct, zw), lambda b, t: (b, t, 0)), full(lb_logits), full(g_norm)]
             + ([state] if has_s0 else []))
    return pl.pallas_call(
        functools.partial(_hgrn_body, C=C, n_sub=ct // C, e=e, has_s0=has_s0),
        grid=(B, T // ct),
        in_specs=specs,
        out_specs=[pl.BlockSpec((1, ct, kw), lambda b, t: (b, t, 0)), state],
        out_shape=[jax.ShapeDtypeStruct((B, T, kw), BF16),
                   jax.ShapeDtypeStruct((B, n_heads, DK_B, DV_B), F32)],
        scratch_shapes=[pltpu.VMEM((n_heads, DV_B, DK_B), F32)],
        compiler_params=_cparams("arbitrary", "arbitrary"),
        name="hgrn2",
    )(*args)


def _topk_threshold(count_ge, count_gt, count_tie_below, write_ties, rows, ksel, idx_bits):
    def thr_step(i, thr):
        cand = thr + lax.shift_left(I32(1), 31 - i)
        return jnp.where(count_ge(cand) >= ksel, cand, thr)

    thr = lax.fori_loop(0, 32, thr_step, jnp.full((rows, V7X_LANES), INT_MIN, I32))
    no_thr = thr == INT_MIN
    write_ties(thr)
    excess = jnp.where(jnp.logical_or(no_thr, count_ge(thr) <= ksel), 0, 1)

    def resolve_ties():
        need = jnp.where(no_thr, 0, ksel - count_gt(thr))

        def tie_step(i, end):
            cand = end + lax.shift_left(I32(1), idx_bits - 1 - i)
            return jnp.where(count_tie_below(cand) <= need, cand, end)

        return lax.fori_loop(0, idx_bits, tie_step, jnp.zeros((rows, V7X_LANES), I32))

    tie_end = lax.cond(jnp.max(excess) > 0, resolve_ties,
                       lambda: jnp.where(no_thr, 0, TIE_NONE).astype(I32))
    return thr, tie_end


def _lane_total(cnt):
    return jnp.broadcast_to(jnp.sum(cnt, axis=1, keepdims=True), cnt.shape)


I16_MIN = -(2**15)


def _dsa_prompt_t_body(q_ref, qi_ref, wi_ref, k_ref, vt_ref, ki3_ref, bias_ref, o_ref,
                       key_scr, hi_scr, lo_scr, tix_scr, neg_scr, qi3_scr, qz_scr,
                       m_scr, l_scr, acc_scr, s_scr, mx_scr, p_scr, *, tq, ksel, idx_bits):
    L = V7X_LANES
    n_heads = qz_scr.shape[0]
    qt = pl.program_id(1)
    nkb = qt + 1
    shift = CHUNK.bit_length() - 1
    row_i = lax.broadcasted_iota(I32, (tq, tq), 0)
    col_i = lax.broadcasted_iota(I32, (tq, tq), 1)
    t_chunk = (qt * tq + col_i) >> shift

    lane = lax.broadcasted_iota(I32, (tq, L), 1)
    qi = qi_ref[0]
    qi_hi = qi.astype(BF16).astype(F32)
    qi_lo = qi - qi_hi
    zero = jnp.zeros((tq, D_I), F32)
    for h in range(H_I):
        sl = slice(h * D_I, (h + 1) * D_I)
        qi3_scr[h] = jnp.concatenate([qi_hi[:, sl], qi_hi[:, sl], qi_lo[:, sl], zero],
                                     axis=1).astype(BF16)
    w_t = (wi_ref[0] * (H_I ** -0.5 * D_I ** -0.5)).T

    def index_block(kb, carry):
        ks = pl.multiple_of(kb * tq, tq)
        ki3 = ki3_ref[0, pl.ds(ks, tq), :]
        score = None
        for h in range(H_I):
            term = w_t[h:h + 1, :] * jnp.maximum(_dot_nt(ki3, qi3_scr[h]), 0.0)
            score = term if score is None else score + term
        key = _sortable(score)
        s_chunk = (ks + row_i) >> shift
        key = jnp.where(s_chunk <= t_chunk, key, INT_MIN)
        key_scr[kb] = key
        hi_scr[kb] = (key >> 16).astype(jnp.int16)
        lo_scr[kb] = ((key & 0xFFFF) + I16_MIN).astype(jnp.int16)
        return carry

    lax.fori_loop(0, nkb, index_block, 0)

    def count(scr, pred, rows, dtype):
        n_acc = 4
        one, nil = jnp.ones((), dtype), jnp.zeros((), dtype)

        def body(kb, accs):
            accs = list(accs)
            for r in range(tq // rows):
                hit = pred(scr[kb, r * rows:(r + 1) * rows, :])
                accs[r % n_acc] = accs[r % n_acc] + jnp.where(hit, one, nil)
            return tuple(accs)
        accs = lax.fori_loop(0, nkb, body, tuple(jnp.zeros((rows, tq), dtype) for _ in range(n_acc)))
        tot = functools.reduce(lambda a, b: a + b, [a.astype(I32) for a in accs])
        return jnp.sum(tot, axis=0, keepdims=True)

    def count16(scr, pred):
        return count(scr, pred, 16, jnp.int16)

    def count32(scr, pred):
        return count(scr, pred, 8, I32)

    def search16(scr, target):
        def step(i, thr):
            cand = thr + lax.shift_left(I32(1), 15 - i)
            cand16 = jnp.broadcast_to(cand.astype(jnp.int16), (16, tq))
            return jnp.where(count16(scr, lambda x: x >= cand16) >= target, cand, thr)
        return lax.fori_loop(0, 16, step, jnp.full((1, tq), I16_MIN, I32))

    t_hi = search16(hi_scr, ksel)
    t_hi16 = jnp.broadcast_to(t_hi.astype(jnp.int16), (16, tq))
    above = count16(hi_scr, lambda x: x > t_hi16)

    def keep_bucket(kb, carry):
        for r in range(tq // 16):
            rows = slice(r * 16, (r + 1) * 16)
            lo_scr[kb, rows, :] = jnp.where(hi_scr[kb, rows, :] == t_hi16, lo_scr[kb, rows, :],
                                            jnp.int16(I16_MIN))
        return carry

    lax.fori_loop(0, nkb, keep_bucket, 0)
    t_lo = search16(lo_scr, ksel - above)
    thr = lax.shift_left(t_hi, 16) + (t_lo - I16_MIN)
    thr8 = jnp.broadcast_to(thr, (8, tq))
    n_ge = count32(key_scr, lambda x: x >= thr8)
    no_thr = thr == INT_MIN
    excess = jnp.where(jnp.logical_or(no_thr, n_ge <= ksel), 0, 1)

    def resolve_ties():
        need = jnp.where(no_thr, 0, ksel - count32(key_scr, lambda x: x > thr8))

        def write(kb, carry):
            tix_scr[kb] = jnp.where(key_scr[kb] == jnp.broadcast_to(thr, (tq, tq)),
                                    kb * tq + row_i, TIE_NONE)
            return carry

        lax.fori_loop(0, nkb, write, 0)

        def step(i, end):
            cand = end + lax.shift_left(I32(1), idx_bits - 1 - i)
            cand8 = jnp.broadcast_to(cand, (8, tq))
            return jnp.where(count32(tix_scr, lambda x: x < cand8) <= need, cand, end)

        return lax.fori_loop(0, idx_bits, step, jnp.zeros((1, tq), I32))

    tie_end = lax.cond(jnp.max(excess) > 0, resolve_ties,
                       lambda: jnp.where(no_thr, 0, TIE_NONE).astype(I32))

    def write_mask(kb, carry):
        key = key_scr[kb]
        thr_b = jnp.broadcast_to(thr, (tq, tq))
        tied = jnp.where(key == thr_b, kb * tq + row_i, TIE_NONE) < jnp.broadcast_to(tie_end, (tq, tq))
        neg_scr[kb] = jnp.where(key > thr_b, 0.0, jnp.where(tied, 0.0, -jnp.inf)).astype(BF16)
        return carry

    lax.fori_loop(0, nkb, write_mask, 0)

    q = q_ref[0].astype(F32)
    for h in range(n_heads):
        qp = q[:, (h // 2) * L:(h // 2 + 1) * L]
        keep = (lane < DH) if h % 2 == 0 else (lane >= DH)
        qz_scr[h] = jnp.where(keep, qp, 0.0).astype(BF16)
        m_scr[h] = jnp.full((1, tq), -1e30, F32)
        l_scr[h] = jnp.zeros((1, tq), F32)
        acc_scr[h] = jnp.zeros((DH, tq), F32)

    rc = 32
    n_rc = tq // rc
    ones = jnp.ones((BF16_ROWS, tq), BF16)

    def scores(kb, h, half):
        ks = pl.multiple_of(kb * tq, tq)
        kp = k_ref[0, pl.ds(ks, tq), (h // 2) * L:(h // 2 + 1) * L]
        x = _dot_nt(kp, qz_scr[h]).astype(BF16) + bias_ref[qt - kb, h] + neg_scr[kb]
        s_scr[half * n_heads + h] = x
        mx_scr[half * n_heads + h] = jnp.max(x.reshape(tq // BF16_ROWS, BF16_ROWS, tq), axis=0)

    for h in range(n_heads):
        scores(0, h, 0)

    def attend_half(kb, cur):
        kb_next = jnp.minimum(kb + 1, nkb - 1)
        for h in range(n_heads):
            scores(kb_next, h, 1 - cur)
            m_old = m_scr[h]
            m_new = jnp.maximum(m_old, jnp.max(mx_scr[cur * n_heads + h].astype(F32),
                                               axis=0, keepdims=True))
            alpha = jnp.exp2(m_old - m_new)
            m_scr[h] = m_new
            m_b = jnp.broadcast_to(m_new.astype(BF16), (rc, tq))
            for r in range(n_rc):
                rows = slice(r * rc, (r + 1) * rc)
                p_scr[cur * n_heads + h, rows, :] = jnp.exp2(
                    s_scr[cur * n_heads + h, rows, :] - m_b)
            vt = jnp.concatenate([vt_ref[0, kb, h * DH:(h + 1) * DH, :], ones], axis=0)
            pv = _dot(vt, p_scr[cur * n_heads + h])
            l_scr[h] = alpha * l_scr[h] + pv[DH:DH + 1, :]
            acc_scr[h] = acc_scr[h] * alpha + pv[:DH, :]

    def attend_block(kb, carry):
        lax.cond(lax.rem(kb, 2) == 0, lambda: attend_half(kb, 0), lambda: attend_half(kb, 1))
        return carry

    lax.fori_loop(0, nkb, attend_block, 0)
    for j in range(n_heads // 2):
        o_t = jnp.concatenate([acc_scr[2 * j] / l_scr[2 * j],
                               acc_scr[2 * j + 1] / l_scr[2 * j + 1]], axis=0)
        o_ref[0, :, j * L:(j + 1) * L] = o_t.T.astype(BF16)


def _dsa_prompt_t(q, qi, wi, kb16, vt16, ki3, bias_t, ksel):
    B, T, aw = q.shape
    n_heads = aw // DH
    tq = bias_t.shape[-1]
    nq = T // tq
    assert T < 2**15 and vt16.shape == (B, nq, aw, tq)
    row = lambda n: pl.BlockSpec((1, tq, n), lambda b, t: (b, t, 0))
    seq = lambda n: pl.BlockSpec((1, T, n), lambda b, t: (b, 0, 0))
    return pl.pallas_call(
        functools.partial(_dsa_prompt_t_body, tq=tq, ksel=ksel, idx_bits=T.bit_length()),
        grid=(B, nq),
        in_specs=[row(aw), row(H_I * D_I), row(H_I), seq(aw),
                  pl.BlockSpec((1, nq, aw, tq), lambda b, t: (b, 0, 0, 0)), seq(4 * D_I),
                  pl.BlockSpec(bias_t.shape, lambda b, t: (0, 0, 0, 0),
                               pipeline_mode=pl.Buffered(1))],
        out_specs=row(aw),
        out_shape=jax.ShapeDtypeStruct((B, T, aw), BF16),
        scratch_shapes=[pltpu.VMEM((nq, tq, tq), I32), pltpu.VMEM((nq, tq, tq), jnp.int16),
                        pltpu.VMEM((nq, tq, tq), jnp.int16), pltpu.VMEM((nq, tq, tq), I32),
                        pltpu.VMEM((nq, tq, tq), BF16), pltpu.VMEM((H_I, tq, 4 * D_I), BF16),
                        pltpu.VMEM((n_heads, tq, V7X_LANES), BF16),
                        pltpu.VMEM((n_heads, 1, tq), F32), pltpu.VMEM((n_heads, 1, tq), F32),
                        pltpu.VMEM((n_heads, DH, tq), F32),
                        pltpu.VMEM((2 * n_heads, tq, tq), BF16),
                        pltpu.VMEM((2 * n_heads, BF16_ROWS, tq), BF16),
                        pltpu.VMEM((2 * n_heads, tq, tq), BF16)],
        compiler_params=_cparams("arbitrary", "arbitrary"),
        name="dsa_prompt",
    )(q, qi, wi, kb16, vt16, ki3, bias_t)


def _pack_heads(q, n_heads):
    T, W = q.shape
    head_of_lane = lax.broadcasted_iota(I32, (T, W), 1) >> (DH.bit_length() - 1)
    qf = q.astype(F32)
    return jnp.concatenate([jnp.where(head_of_lane == h, qf, 0.0) for h in range(n_heads)],
                           axis=0).astype(BF16)


def _unpack_heads(acc, den, n_heads):
    T = acc.shape[0] // n_heads
    W = acc.shape[1]
    head_of_lane = lax.broadcasted_iota(I32, (T, W), 1) >> (DH.bit_length() - 1)
    out = jnp.zeros((T, W), F32)
    for h in range(n_heads):
        blk = acc[h * T:(h + 1) * T, :] / den[h * T:(h + 1) * T, :]
        out = jnp.where(head_of_lane == h, blk, out)
    return out


def _gather_heads(ref, n_keys, n_heads):
    pairs = []
    for j in range(n_heads // 2):
        a = ref[0, pl.ds(2 * j, n_keys, stride=n_heads), :]
        b = ref[0, pl.ds(2 * j + 1, n_keys, stride=n_heads), :]
        pairs.append(jnp.concatenate([a, b], axis=1).astype(BF16))
    return jnp.concatenate(pairs, axis=1)


def _dsa_sample_body(q_ref, qi_ref, wi_ref, kn_ref, vn_ref, kin_ref, kc_ref, vc_ref, kic_ref,
                     bias_ref, biasn_ref, o_ref,
                     key_scr, tix_scr, keyn_scr, tixn_scr, sel_scr, qbd_scr, m_scr, l_scr, acc_scr,
                     *, ts, kc_len, n_new, ksel, idx_bits):
    L = V7X_LANES
    n_heads = qbd_scr.shape[0] // ts
    nc = key_scr.shape[0]
    step = pl.program_id(1)
    past = nc * kc_len
    lane = lax.broadcasted_iota(I32, (ts, L), 1)

    @pl.when(step == 0)
    def _():
        qi = qi_ref[0]
        qi_hm = jnp.concatenate([qi[:, h * D_I:(h + 1) * D_I] for h in range(H_I)], axis=0)
        qi_hi, qi_lo = _split_bf16(qi_hm)
        w = wi_ref[0] * (H_I ** -0.5 * D_I ** -0.5)

        def scores(ki):
            ki_hi, ki_lo = _split_bf16(ki)
            s = _dot_nt(qi_hi, ki_hi) + _dot_nt(qi_hi, ki_lo) + _dot_nt(qi_lo, ki_hi)
            acc = None
            for h in range(H_I):
                term = w[:, h:h + 1] * jnp.maximum(s[h * ts:(h + 1) * ts, :], 0.0)
                acc = term if acc is None else acc + term
            return _sortable(acc)

        for c in range(nc):
            key_scr[c] = scores(kic_ref[0, c * kc_len:(c + 1) * kc_len, :])
        keyn_scr[...] = jnp.where(lane < n_new, scores(kin_ref[0]), INT_MIN)

        def count(pred_key=None, pred_tie=None):
            cnt = jnp.zeros((ts, L), I32)
            for c in range(nc):
                for i in range(kc_len // L):
                    if pred_key is not None:
                        hit = pred_key(key_scr[c, :, i * L:(i + 1) * L])
                    else:
                        hit = pred_tie(tix_scr[c, :, i * L:(i + 1) * L])
                    cnt = cnt + jnp.where(hit, 1, 0)
            hit = pred_key(keyn_scr[...]) if pred_key is not None else pred_tie(tixn_scr[...])
            return _lane_total(cnt + jnp.where(hit, 1, 0))

        def write_ties(thr):
            for c in range(nc):
                for i in range(kc_len // L):
                    idx = c * kc_len + i * L + lane
                    tix_scr[c, :, i * L:(i + 1) * L] = jnp.where(
                        key_scr[c, :, i * L:(i + 1) * L] == thr, idx, TIE_NONE)
            tixn_scr[...] = jnp.where(keyn_scr[...] == thr, past + lane, TIE_NONE)

        thr, tie_end = _topk_threshold(
            lambda x: count(pred_key=lambda k: k >= x),
            lambda x: count(pred_key=lambda k: k > x),
            lambda x: count(pred_tie=lambda t: t < x),
            write_ties, ts, ksel, idx_bits)
        sel_scr[0] = thr
        sel_scr[1] = tie_end
        qbd_scr[...] = _pack_heads(q_ref[0], n_heads)
        m_scr[...] = jnp.full(m_scr.shape, -1e30, F32)
        l_scr[...] = jnp.zeros(l_scr.shape, F32)
        acc_scr[...] = jnp.zeros(acc_scr.shape, F32)

    thr = sel_scr[0]
    tie_end = sel_scr[1]

    def attend(keys, ties, k16, v16, bias):
        n = keys.shape[1]
        negs = []
        for i in range(n // L):
            sel = jnp.logical_or(keys[:, i * L:(i + 1) * L] > thr,
                                 ties[:, i * L:(i + 1) * L] < tie_end)
            negs.append(jnp.where(sel, 0.0, -jnp.inf))
        neg = jnp.concatenate(negs, axis=1) if len(negs) > 1 else negs[0]
        neg = jnp.concatenate([neg] * n_heads, axis=0)
        s = _dot_nt(qbd_scr[...], k16) + bias + neg
        m_old = m_scr[...]
        m_new = jnp.maximum(m_old, jnp.max(s, axis=1, keepdims=True))
        alpha = jnp.exp(m_old - m_new)
        p = jnp.exp(s - m_new)
        l_scr[...] = alpha * l_scr[...] + jnp.sum(p, axis=1, keepdims=True)
        m_scr[...] = m_new
        acc_scr[...] = acc_scr[...] * alpha + _dot(p.astype(BF16), v16)

    attend(key_scr[step], tix_scr[step], _gather_heads(kc_ref, kc_len, n_heads),
           _gather_heads(vc_ref, kc_len, n_heads),
           bias_ref[...])

    @pl.when(step == nc - 1)
    def _():
        attend(keyn_scr[...], tixn_scr[...], kn_ref[0], vn_ref[0], biasn_ref[...])
        o_ref[0] = _unpack_heads(acc_scr[...], l_scr[...], n_heads).astype(BF16)


def _dsa_sample(q, qi, wi, kn16, vn16, kin, cache_k, cache_v, cache_ki, bias, n_new, ksel):
    B, ts, aw = q.shape
    n_heads = aw // DH
    P = cache_k.shape[1] // n_heads
    npad = kn16.shape[1]
    kc_len = min(P, 1024)
    nc = P // kc_len
    L = V7X_LANES
    per_b = lambda a: pl.BlockSpec((1,) + a.shape[1:], lambda b, c: (b, 0, 0))
    chunk = pl.BlockSpec((1, kc_len * n_heads, DH), lambda b, c: (b, c, 0))
    rows = n_heads * ts
    return pl.pallas_call(
        functools.partial(_dsa_sample_body, ts=ts, kc_len=kc_len, n_new=n_new, ksel=ksel,
                          idx_bits=(P + npad).bit_length()),
        grid=(B, nc),
        in_specs=[per_b(q), per_b(qi), per_b(wi), per_b(kn16), per_b(vn16), per_b(kin),
                  chunk, chunk, per_b(cache_ki),
                  pl.BlockSpec((rows, kc_len), lambda b, c: (0, c)),
                  pl.BlockSpec((rows, npad), lambda b, c: (0, P // npad))],
        out_specs=pl.BlockSpec((1, ts, aw), lambda b, c: (b, 0, 0)),
        out_shape=jax.ShapeDtypeStruct((B, ts, aw), BF16),
        scratch_shapes=[pltpu.VMEM((nc, ts, kc_len), I32), pltpu.VMEM((nc, ts, kc_len), I32),
                        pltpu.VMEM((ts, npad), I32), pltpu.VMEM((ts, npad), I32),
                        pltpu.VMEM((2, ts, L), I32), pltpu.VMEM((rows, aw), BF16),
                        pltpu.VMEM((rows, 1), F32), pltpu.VMEM((rows, 1), F32),
                        pltpu.VMEM((rows, aw), F32)],
        compiler_params=_cparams("arbitrary", "arbitrary"),
        name="dsa_sample",
    )(q, qi, wi, kn16, vn16, kin, cache_k, cache_v, cache_ki, bias, bias)


def _band_prompt_body(q_ref, k_ref, vt_ref, bias_ref, o_ref, qz_scr, s_scr, p_scr, ot_scr,
                      *, tq, n_back):
    L = V7X_LANES
    n_heads = q_ref.shape[-1] // DH
    n_slots = s_scr.shape[0]
    n_p = p_scr.shape[0]
    nb = n_back + 1
    rc = 32
    n_rc = tq // rc
    qt = pl.program_id(1)
    lane = lax.broadcasted_iota(I32, (tq, L), 1)
    n_tiles = bias_ref.shape[0]
    blocks, starts, exists = [], [], []
    for j in range(nb):
        kb = qt - n_back + j
        blocks.append(jnp.maximum(kb, 0))
        starts.append(pl.multiple_of(jnp.maximum(kb, 0) * tq, tq))
        exists.append(kb >= 0)
    q = q_ref[0].astype(F32)
    for h in range(n_heads):
        qp = q[:, (h // 2) * L:(h // 2 + 1) * L]
        keep = (lane < DH) if h % 2 == 0 else (lane >= DH)
        qz_scr[h] = jnp.where(keep, qp, 0.0).astype(BF16)

    def scores(h):
        mx = None
        for j in range(nb):
            kp = k_ref[0, pl.ds(starts[j], tq), (h // 2) * L:(h // 2 + 1) * L]
            tile = jnp.where(exists[j], h * nb + j, n_tiles - 1)
            x = _dot_nt(kp, qz_scr[h]).astype(BF16) + bias_ref[tile]
            s_scr[h % n_slots, j] = x
            xm = jnp.max(x.reshape(tq // BF16_ROWS, BF16_ROWS, tq), axis=0)
            mx = xm if mx is None else jnp.maximum(mx, xm)
        return mx

    ones = jnp.ones((BF16_ROWS, tq), BF16)
    maxes = [scores(h) for h in range(n_slots - 1)]
    for h in range(n_heads):
        if h + n_slots - 1 < n_heads:
            maxes.append(scores(h + n_slots - 1))
        slot = h % n_slots
        m = jnp.max(maxes[h].astype(F32), axis=0, keepdims=True)
        m_b = jnp.broadcast_to(m.astype(BF16), (rc, tq))
        out = None
        for j in range(nb):
            for r in range(n_rc):
                rows = slice(r * rc, (r + 1) * rc)
                p_scr[h % n_p, j, rows, :] = jnp.exp2(s_scr[slot, j, rows, :] - m_b)
            vt = jnp.concatenate([vt_ref[0, blocks[j], h * DH:(h + 1) * DH, :], ones], axis=0)
            oj = _dot(vt, p_scr[h % n_p, j])
            out = oj if out is None else out + oj
        ot_scr[h * DH:(h + 1) * DH, :] = out[:DH, :] / out[DH:DH + 1, :]
    for pair in range(n_heads // 2):
        o_ref[0, :, pair * L:(pair + 1) * L] = ot_scr[pair * L:(pair + 1) * L, :].T.astype(BF16)


def _band_prompt(q, kb16, vt16, bias_t):
    B, T, cw = q.shape
    n_heads = cw // DH
    tq = bias_t.shape[-1]
    nq = T // tq
    n_back = (bias_t.shape[0] - 1) // n_heads - 1
    assert vt16.shape == (B, nq, cw, tq)
    row = pl.BlockSpec((1, tq, cw), lambda b, t: (b, t, 0))
    return pl.pallas_call(
        functools.partial(_band_prompt_body, tq=tq, n_back=n_back),
        grid=(B, nq),
        in_specs=[row, pl.BlockSpec((1, T, cw), lambda b, t: (b, 0, 0)),
                  pl.BlockSpec((1, nq, cw, tq), lambda b, t: (b, 0, 0, 0)),
                  pl.BlockSpec(bias_t.shape, lambda b, t: (0, 0, 0),
                               pipeline_mode=pl.Buffered(1))],
        out_specs=row,
        out_shape=jax.ShapeDtypeStruct((B, T, cw), BF16),
        scratch_shapes=[pltpu.VMEM((n_heads, tq, V7X_LANES), BF16),
                        pltpu.VMEM((6, n_back + 1, tq, tq), BF16),
                        pltpu.VMEM((6, n_back + 1, tq, tq), BF16),
                        pltpu.VMEM((cw, tq), F32)],
        compiler_params=_cparams("arbitrary", "arbitrary"),
        name="band_prompt",
    )(q, kb16, vt16, bias_t)


def _band_sample_body(q_ref, kn_ref, vn_ref, kc_ref, vc_ref, bias_ref, biasn_ref, o_ref):
    n_heads = q_ref.shape[-1] // DH
    cb = bias_ref.shape[1]
    qbd = _pack_heads(q_ref[0], n_heads)
    s_c = _dot_nt(qbd, _gather_heads(kc_ref, cb, n_heads)) + bias_ref[...]
    s_n = _dot_nt(qbd, kn_ref[0]) + biasn_ref[...]
    m = jnp.maximum(jnp.max(s_c, axis=1, keepdims=True), jnp.max(s_n, axis=1, keepdims=True))
    p_c = jnp.exp(s_c - m)
    p_n = jnp.exp(s_n - m)
    den = jnp.sum(p_c, axis=1, keepdims=True) + jnp.sum(p_n, axis=1, keepdims=True)
    acc = (_dot(p_c.astype(BF16), _gather_heads(vc_ref, cb, n_heads))
           + _dot(p_n.astype(BF16), vn_ref[0]))
    o_ref[0] = _unpack_heads(acc, den, n_heads).astype(BF16)


def _band_sample(q, kn16, vn16, cache_k, cache_v, bias):
    B, ts, cw = q.shape
    n_heads = cw // DH
    cb = cache_k.shape[1] // n_heads
    npad = kn16.shape[1]
    rows = n_heads * ts
    per_b = lambda a: pl.BlockSpec((1,) + a.shape[1:], lambda b: (b, 0, 0))
    return pl.pallas_call(
        _band_sample_body,
        grid=(B,),
        in_specs=[per_b(q), per_b(kn16), per_b(vn16), per_b(cache_k), per_b(cache_v),
                  pl.BlockSpec((rows, cb), lambda b: (0, 0)),
                  pl.BlockSpec((rows, npad), lambda b: (0, cb // npad))],
        out_specs=per_b(q),
        out_shape=jax.ShapeDtypeStruct((B, ts, cw), BF16),
        compiler_params=_cparams("arbitrary"),
        name="band_sample",
    )(q, kn16, vn16, cache_k, cache_v, bias, bias)


def _pad_rows(a, n):
    return jnp.pad(a, ((0, 0), (0, n - a.shape[1]), (0, 0)))


def kernel(x_prompt, x_sample, c_prompt, c_sample, cache_a_k, cache_a_v, cache_a_kidx, state_b,
           cache_c_k, cache_c_v, w_ada, b_ada, norm_mix, norm_ffn, norm_final, w_in_even,
           w_out_even, t5_table, lb_logits, norm_b_out, w_in_odd, w_out_odd, rel_table_c,
           w_ffn_gate, w_ffn_up, w_ffn_down):
    Bp, T, D = x_prompt.shape
    Bs, Ts, _ = x_sample.shape
    P = cache_a_k.shape[2]
    c_buf = cache_c_k.shape[2]
    depth = w_ada.shape[0]
    L = V7X_LANES
    n_heads_a = cache_a_k.shape[3]
    aw = n_heads_a * DH
    n_heads_c = cache_c_k.shape[3]
    bw = w_out_even.shape[1] - aw
    iw = H_I * D_I
    ksel_p = min(K_TOP, T // 4)
    ksel_s = min(K_TOP, (P + Ts) // 4)
    c_keep = min(C_BACK * CHUNK, T)
    tq = min(T, 256)
    npad = L
    assert Ts <= npad and P % npad == 0 and c_buf % npad == 0 and T % tq == 0

    mod = _adaln_all(jnp.concatenate([c_prompt, c_sample], axis=0), w_ada, b_ada)

    def mods(l):
        parts = jnp.split(mod[l], 6, axis=-1)
        return ([p[:Bp].reshape(Bp, 1, D) for p in parts],
                [p[Bp:].reshape(Bs, 1, D) for p in parts])

    nq = T // tq
    t5_prompt = _bias_tiles(
        t5_table, groups=nq * n_heads_a, rows=tq, cols=tq, idx_fn=_t5_bucket,
        group_fn=lambda g: (g % n_heads_a, -(g // n_heads_a) * tq), keys_on_rows=True,
    ).reshape(nq, n_heads_a, tq, tq)
    t5_sample = _bias_tiles(
        t5_table, groups=n_heads_a, rows=Ts, cols=P + npad, idx_fn=_t5_bucket,
        group_fn=lambda g: (g, -P),
    ).reshape(n_heads_a * Ts, P + npad)

    xp, xs = x_prompt, x_sample
    outs_p = {k: [] for k in ("ak", "av", "ai", "bs", "ck", "cv")}
    outs_s = {k: [] for k in ("ak", "av", "ai", "bs", "ck", "cv")}
    for l in range(depth):
        (sh1p, sc1p, g1p, sh2p, sc2p, g2p), (sh1s, sc1s, g1s, sh2s, sc2s, g2s) = mods(l)
        gm = norm_mix[l].reshape(1, D)
        gn = norm_ffn[l].reshape(1, D)
        wg = w_ffn_gate[l].astype(BF16)
        wu = w_ffn_up[l].astype(BF16)
        wd = w_ffn_down[l].astype(BF16)
        gf = norm_final.reshape(1, D) if l == depth - 1 else None
        if l % 2 == 0:
            e = l // 2
            w_in = w_in_even[e]
            wa = w_in[:, :3 * aw].astype(BF16)
            wi_cols = w_in[:, 3 * aw:3 * aw + iw + D_I + H_I]
            wi_cols = jnp.pad(wi_cols, ((0, 0), (0, -wi_cols.shape[1] % L)))
            wh = wi_cols.astype(BF16)
            wl = (wi_cols - wh.astype(F32)).astype(BF16)
            wb = w_in[:, 3 * aw + iw + D_I + H_I:].astype(BF16)
            w_out = w_out_even[e].astype(BF16)
            gb_norm = norm_b_out[e].reshape(1, bw)

            q, k, v, k16, vt16, qi, ki, ki3, wi, zb = _in_even(xp, sc1p, sh1p, gm, wa, wh, wl, wb,
                                                              keys_on_rows=True)
            oa = _dsa_prompt_t(q, qi, wi, k16, vt16, ki3, t5_prompt, ksel_p)
            ob, st = _hgrn(zb, lb_logits, gb_norm, None, e)
            outs_p["ak"].append(k.reshape(Bp, T, n_heads_a, DH))
            outs_p["av"].append(v.reshape(Bp, T, n_heads_a, DH))
            outs_p["ai"].append(ki)
            outs_p["bs"].append(st)
            xp = _mix_ffn(xp, g1p, sc2p, sh2p, g2p, gn, [oa, ob], [w_out[:aw], w_out[aw:]],
                          wg, wu, wd, gf)

            q, k, v, k16, v16, qi, ki, _, wi, zb = _in_even(xs, sc1s, sh1s, gm, wa, wh, wl, wb,
                                                            keys_on_rows=False)
            oa = _dsa_sample(q, qi, wi, _pad_rows(k16, npad), _pad_rows(v16, npad),
                             _pad_rows(ki, npad), cache_a_k[e].reshape(Bs, P * n_heads_a, DH),
                             cache_a_v[e].reshape(Bs, P * n_heads_a, DH), cache_a_kidx[e],
                             t5_sample,
                             Ts, ksel_s)
            ob, st = _hgrn(zb, lb_logits, gb_norm, state_b[e], e)
            outs_s["ak"].append(k.reshape(Bs, Ts, n_heads_a, DH))
            outs_s["av"].append(v.reshape(Bs, Ts, n_heads_a, DH))
            outs_s["ai"].append(ki)
            outs_s["bs"].append(st)
            xs = _mix_ffn(xs, g1s, sc2s, sh2s, g2s, gn, [oa, ob], [w_out[:aw], w_out[aw:]],
                          wg, wu, wd, gf)
        else:
            o = l // 2
            cw = n_heads_c * DH
            w_in = w_in_odd[o].astype(BF16)
            w_out = w_out_odd[o].astype(BF16)
            n_back = -(-(C_BACK * CHUNK) // tq)
            shift = CHUNK.bit_length() - 1

            n_band = n_heads_c * (n_back + 1)

            def band_mask(g, t, s, n_back=n_back, n_band=n_band):
                j = g % (n_back + 1)
                kc = (s + (j - n_back) * tq) >> shift
                tc = t >> shift
                return (kc <= tc) & (kc >= tc - C_BACK) & (g < n_band)

            band_p = _bias_tiles(
                rel_table_c[o], groups=n_band + 1, rows=tq, cols=tq, idx_fn=_clip_index,
                group_fn=lambda g, n_back=n_back: (jnp.minimum(g // (n_back + 1), n_heads_c - 1),
                                                   (g % (n_back + 1) - n_back) * tq),
                mask_fn=band_mask, keys_on_rows=True)
            band_s = _bias_tiles(
                rel_table_c[o], groups=n_heads_c, rows=Ts, cols=c_buf + npad, idx_fn=_clip_index,
                group_fn=lambda g: (g, -c_buf),
                mask_fn=lambda g, t, s: s < c_buf + Ts,
            ).reshape(n_heads_c * Ts, c_buf + npad)

            q, k, v, k16, vt16 = _in_odd(xp, sc1p, sh1p, gm, w_in, keys_on_rows=True)
            op = _band_prompt(q, k16, vt16, band_p)
            outs_p["ck"].append(k[:, T - c_keep:].reshape(Bp, c_keep, n_heads_c, DH))
            outs_p["cv"].append(v[:, T - c_keep:].reshape(Bp, c_keep, n_heads_c, DH))
            xp = _mix_ffn(xp, g1p, sc2p, sh2p, g2p, gn, [op], [w_out], wg, wu, wd, gf)

            q, k, v, k16, v16 = _in_odd(xs, sc1s, sh1s, gm, w_in, keys_on_rows=False)
            osm = _band_sample(q, _pad_rows(k16, npad), _pad_rows(v16, npad),
                               cache_c_k[o].reshape(Bs, c_buf * n_heads_c, DH),
                               cache_c_v[o].reshape(Bs, c_buf * n_heads_c, DH), band_s)
            outs_s["ck"].append(k.reshape(Bs, Ts, n_heads_c, DH))
            outs_s["cv"].append(v.reshape(Bs, Ts, n_heads_c, DH))
            xs = _mix_ffn(xs, g1s, sc2s, sh2s, g2s, gn, [osm], [w_out], wg, wu, wd, gf)

    st = lambda xs_: jnp.stack(xs_)
    return (xp, xs, st(outs_p["ak"]), st(outs_p["av"]), st(outs_p["ai"]), st(outs_p["bs"]),
            st(outs_p["ck"]), st(outs_p["cv"]), st(outs_s["ak"]), st(outs_s["av"]),
            st(outs_s["ai"]), st(outs_s["bs"]), st(outs_s["ck"]), st(outs_s["cv"]))
```

```python
import functools
import math

import jax
import jax.numpy as jnp
from jax import lax
from jax.experimental import pallas as pl
from jax.experimental.pallas import tpu as pltpu

F32, BF16, I32 = jnp.float32, jnp.bfloat16, jnp.int32

CHUNK = 64
EPS = 1e-6
DH = 64
H_I = 8
D_I = 64
K_TOP = 256
N_BUCKETS = 32
T5_MAX_DIST = 1024
DK_B = 128
DV_B = 128
C_BACK = 8
REL_CLIP = 128

V7X_LANES = 128
BF16_ROWS = 16
V7X_VMEM_LIMIT = 56 * 2**20

INT_MIN = -(2**31)
TIE_NONE = 2**30
HIGHEST = lax.Precision.HIGHEST


def _cparams(*sem):
    return pltpu.CompilerParams(dimension_semantics=sem, vmem_limit_bytes=V7X_VMEM_LIMIT)


def _dot(a, b):
    return jnp.dot(a, b, preferred_element_type=F32)


def _dot_nt(a, b):
    return lax.dot_general(a, b, (((1,), (1,)), ((), ())), preferred_element_type=F32)


def _split_bf16(x):
    hi = x.astype(BF16)
    lo = (x - hi.astype(F32)).astype(BF16)
    return hi, lo


def _silu(x):
    return x * jax.nn.sigmoid(x)


def _norm_mod(x, g, scale, shift):
    ms = jnp.mean(x * x, axis=-1, keepdims=True)
    return (x * lax.rsqrt(ms + EPS) * g) * (1.0 + scale) + shift


def _sortable(x):
    b = pltpu.bitcast(x + 0.0, I32)
    return b ^ ((b >> 31) & I32(0x7FFFFFFF))


LOG2E = math.log2(math.e)


def _q_scale(base2):
    return DH ** -0.5 * (LOG2E if base2 else 1.0)


def _row_tiles(B, T, rows=256):
    tt = min(T, rows)
    bb = max(1, min(B, rows // tt))
    assert T % tt == 0 and B % bb == 0
    return bb, tt


def _ada_body(c_ref, w_ref, b_ref, o_ref):
    s = _silu(c_ref[...])
    o_ref[0] = jnp.dot(s, w_ref[0], precision=HIGHEST, preferred_element_type=F32) + b_ref[0]


def _adaln_all(c_all, w_ada, b_ada):
    depth, D, E = w_ada.shape
    R = c_all.shape[0]
    tn = 1024
    return pl.pallas_call(
        _ada_body,
        grid=(depth, E // tn),
        in_specs=[pl.BlockSpec((R, D), lambda l, n: (0, 0)),
                  pl.BlockSpec((1, D, tn), lambda l, n: (l, 0, n)),
                  pl.BlockSpec((1, 1, tn), lambda l, n: (l, 0, n))],
        out_specs=pl.BlockSpec((1, R, tn), lambda l, n: (l, 0, n)),
        out_shape=jax.ShapeDtypeStruct((depth, R, E), F32),
        compiler_params=_cparams("arbitrary", "arbitrary"),
        name="adaln",
    )(c_all, w_ada, b_ada.reshape(depth, 1, E))


def _t5_bucket(rel):
    nb = N_BUCKETS // 2
    max_exact = nb // 2
    base = jnp.where(rel > 0, nb, 0)
    n = jnp.abs(rel)
    large = max_exact + (jnp.log(jnp.maximum(n, 1).astype(F32) / max_exact)
                         / math.log(T5_MAX_DIST / max_exact) * (nb - max_exact)).astype(I32)
    large = jnp.minimum(large, nb - 1)
    return base + jnp.where(n < max_exact, n, large)


def _clip_index(rel):
    return jnp.clip(-rel, -REL_CLIP, REL_CLIP) + REL_CLIP


def _bias_body(tab_ref, o_ref, *, rows, cols, width, n_tab, idx_fn, group_fn, mask_fn,
               keys_on_rows):
    g = pl.program_id(0)
    head, off = group_fn(g)
    i = lax.broadcasted_iota(I32, (1, width), 1)
    diff = jnp.where(i < width - rows, i, i - width)
    idx = idx_fn(off - diff if keys_on_rows else off + diff)
    val = jnp.zeros((1, width), F32)
    for b in range(n_tab):
        val = jnp.where(idx == b, tab_ref[head, b], val)
    x = jnp.broadcast_to(val, (rows, width))
    y = pltpu.roll(x, 0, 1, stride=1, stride_axis=0)
    tile = y[:, :cols]
    if keys_on_rows:
        tile = tile * LOG2E
    if mask_fn is not None:
        t = lax.broadcasted_iota(I32, (rows, cols), 1 if keys_on_rows else 0)
        s = lax.broadcasted_iota(I32, (rows, cols), 0 if keys_on_rows else 1)
        tile = jnp.where(mask_fn(g, t, s), tile, -jnp.inf)
    o_ref[0] = tile.astype(o_ref.dtype)


def _bias_tiles(table, *, groups, rows, cols, idx_fn, group_fn, mask_fn=None,
                keys_on_rows=False):
    n_tab = table.shape[0]
    width = -(-(rows + cols) // V7X_LANES) * V7X_LANES
    body = functools.partial(_bias_body, rows=rows, cols=cols, width=width, n_tab=n_tab,
                             idx_fn=idx_fn, group_fn=group_fn, mask_fn=mask_fn,
                             keys_on_rows=keys_on_rows)
    return pl.pallas_call(
        body,
        grid=(groups,),
        in_specs=[pl.BlockSpec(memory_space=pltpu.SMEM)],
        out_specs=pl.BlockSpec((1, rows, cols), lambda g: (g, 0, 0)),
        out_shape=jax.ShapeDtypeStruct((groups, rows, cols), BF16 if keys_on_rows else F32),
        compiler_params=_cparams("arbitrary"),
        name="bias_tiles",
    )(table.T)


def _in_even_body(x_ref, sc_ref, sh_ref, g_ref, wa_ref, wh_ref, wl_ref, wb_ref,
                  q_ref, k_ref, v_ref, kb_ref, vb_ref, qi_ref, ki_ref, ki3_ref, wi_ref, zb_ref,
                  *, aw, keys_on_rows):
    bb, tt, D = x_ref.shape
    M = bb * tt
    h = _norm_mod(x_ref[...], g_ref[...].reshape(1, 1, D), sc_ref[...], sh_ref[...]).reshape(M, D)
    h_hi, h_lo = _split_bf16(h)
    za = _dot(h_hi, wa_ref[...])
    q_ref[...] = (za[:, :aw] * _q_scale(keys_on_rows)).astype(BF16).reshape(bb, tt, aw)
    k = za[:, aw:2 * aw]
    v = za[:, 2 * aw:3 * aw]
    k_ref[...] = k.reshape(bb, tt, aw)
    v_ref[...] = v.reshape(bb, tt, aw)
    kb_ref[...] = k.astype(BF16).reshape(bb, tt, aw)
    if keys_on_rows:
        vb_ref[0, 0] = v.T.astype(BF16)
    else:
        vb_ref[...] = v.astype(BF16).reshape(bb, tt, aw)
    wh = wh_ref[...]
    zi = _dot(h_hi, wh) + _dot(h_hi, wl_ref[...]) + _dot(h_lo, wh)
    iw = H_I * D_I
    ki = zi[:, iw:iw + D_I]
    qi_ref[...] = zi[:, :iw].reshape(bb, tt, iw)
    ki_ref[...] = ki.reshape(bb, tt, D_I)
    ki_hi = ki.astype(BF16).astype(F32)
    ki3 = jnp.concatenate([ki_hi, ki - ki_hi, ki_hi, jnp.zeros_like(ki)], axis=1)
    ki3_ref[...] = ki3.astype(BF16).reshape(bb, tt, 4 * D_I)
    wi_ref[...] = zi[:, iw + D_I:iw + D_I + H_I].reshape(bb, tt, H_I)
    zb_ref[...] = _dot(h_hi, wb_ref[...]).reshape(bb, tt, zb_ref.shape[-1])


def _in_even(x, scale, shift, g, wa, wh, wl, wb, keys_on_rows):
    B, T, D = x.shape
    aw = wa.shape[1] // 3
    bw = wb.shape[1]
    iw = H_I * D_I
    bb, tt = _row_tiles(B, T)
    assert not keys_on_rows or bb == 1
    row = lambda n: pl.BlockSpec((bb, tt, n), lambda b, t: (b, t, 0))
    mod = pl.BlockSpec((bb, 1, D), lambda b, t: (b, 0, 0))
    full = lambda a: pl.BlockSpec(a.shape, lambda b, t: (0,) * a.ndim)
    outs = [(aw, BF16), (aw, F32), (aw, F32), (aw, BF16), (aw, BF16), (iw, F32), (D_I, F32),
            (4 * D_I, BF16), (H_I, F32), (bw, F32)]
    out_specs = [row(n) for n, _ in outs]
    out_shape = [jax.ShapeDtypeStruct((B, T, n), dt) for n, dt in outs]
    if keys_on_rows:
        out_specs[4] = pl.BlockSpec((1, 1, aw, tt), lambda b, t: (b, t, 0, 0))
        out_shape[4] = jax.ShapeDtypeStruct((B, T // tt, aw, tt), BF16)
    return pl.pallas_call(
        functools.partial(_in_even_body, aw=aw, keys_on_rows=keys_on_rows),
        grid=(B // bb, T // tt),
        in_specs=[row(D), mod, mod, full(g), full(wa), full(wh), full(wl), full(wb)],
        out_specs=out_specs,
        out_shape=out_shape,
        compiler_params=_cparams("arbitrary", "arbitrary"),
        name="in_even",
    )(x, scale, shift, g, wa, wh, wl, wb)


def _in_odd_body(x_ref, sc_ref, sh_ref, g_ref, w_ref, q_ref, k_ref, v_ref, kb_ref, vb_ref,
                 *, keys_on_rows):
    bb, tt, D = x_ref.shape
    M = bb * tt
    cw = q_ref.shape[-1]
    h = _norm_mod(x_ref[...], g_ref[...].reshape(1, 1, D), sc_ref[...], sh_ref[...]).reshape(M, D)
    z = _dot(h.astype(BF16), w_ref[...])
    q_ref[...] = (z[:, :cw] * _q_scale(keys_on_rows)).astype(BF16).reshape(bb, tt, cw)
    k = z[:, cw:2 * cw]
    v = z[:, 2 * cw:]
    k_ref[...] = k.reshape(bb, tt, cw)
    v_ref[...] = v.reshape(bb, tt, cw)
    kb_ref[...] = k.astype(BF16).reshape(bb, tt, cw)
    if keys_on_rows:
        vb_ref[0, 0] = v.T.astype(BF16)
    else:
        vb_ref[...] = v.astype(BF16).reshape(bb, tt, cw)


def _in_odd(x, scale, shift, g, w, keys_on_rows):
    B, T, D = x.shape
    cw = w.shape[1] // 3
    bb, tt = _row_tiles(B, T)
    assert not keys_on_rows or bb == 1
    row = lambda n: pl.BlockSpec((bb, tt, n), lambda b, t: (b, t, 0))
    mod = pl.BlockSpec((bb, 1, D), lambda b, t: (b, 0, 0))
    full = lambda a: pl.BlockSpec(a.shape, lambda b, t: (0,) * a.ndim)
    dts = [BF16, F32, F32, BF16, BF16]
    out_specs = [row(cw) for _ in dts]
    out_shape = [jax.ShapeDtypeStruct((B, T, cw), dt) for dt in dts]
    if keys_on_rows:
        out_specs[4] = pl.BlockSpec((1, 1, cw, tt), lambda b, t: (b, t, 0, 0))
        out_shape[4] = jax.ShapeDtypeStruct((B, T // tt, cw, tt), BF16)
    return pl.pallas_call(
        functools.partial(_in_odd_body, keys_on_rows=keys_on_rows),
        grid=(B // bb, T // tt),
        in_specs=[row(D), mod, mod, full(g), full(w)],
        out_specs=out_specs,
        out_shape=out_shape,
        compiler_params=_cparams("arbitrary", "arbitrary"),
        name="in_odd",
    )(x, scale, shift, g, w)


def _mix_ffn_body(*refs, n_mix, final):
    x_ref, g1_ref, sc2_ref, sh2_ref, g2_ref, gn_ref = refs[:6]
    mix_refs = refs[6:6 + n_mix]
    wmix_refs = refs[6 + n_mix:6 + 2 * n_mix]
    wg_ref, wu_ref, wd_ref = refs[6 + 2 * n_mix:9 + 2 * n_mix]
    gf_ref = refs[9 + 2 * n_mix] if final else None
    o_ref = refs[-1]
    bb, tt, D = x_ref.shape
    M = bb * tt
    m = None
    for mr, wr in zip(mix_refs, wmix_refs):
        part = _dot(mr[...].reshape(M, mr.shape[-1]), wr[...])
        m = part if m is None else m + part
    x1 = x_ref[...] + g1_ref[...] * m.reshape(bb, tt, D)
    h2 = _norm_mod(x1, gn_ref[...].reshape(1, 1, D), sc2_ref[...], sh2_ref[...])
    h2 = h2.reshape(M, D).astype(BF16)
    a = _dot(h2, wg_ref[...])
    b = _dot(h2, wu_ref[...])
    u = (_silu(a) * b).astype(BF16)
    f = _dot(u, wd_ref[...])
    x2 = x1 + g2_ref[...] * f.reshape(bb, tt, D)
    if final:
        ms = jnp.mean(x2 * x2, axis=-1, keepdims=True)
        x2 = x2 * lax.rsqrt(ms + EPS) * gf_ref[...].reshape(1, 1, D)
    o_ref[...] = x2


def _mix_ffn(x, g1, sc2, sh2, g2, gn, mixes, wmixes, wg, wu, wd, gf=None):
    B, T, D = x.shape
    bb, tt = _row_tiles(B, T)
    row = lambda n: pl.BlockSpec((bb, tt, n), lambda b, t: (b, t, 0))
    mod = pl.BlockSpec((bb, 1, D), lambda b, t: (b, 0, 0))
    full = lambda a: pl.BlockSpec(a.shape, lambda b, t: (0,) * a.ndim,
                                  pipeline_mode=pl.Buffered(1))
    final = gf is not None
    args = [x, g1, sc2, sh2, g2, gn, *mixes, *wmixes, wg, wu, wd] + ([gf] if final else [])
    specs = ([row(D), mod, mod, mod, mod, full(gn)] + [row(m.shape[-1]) for m in mixes]
             + [full(w) for w in wmixes] + [full(wg), full(wu), full(wd)]
             + ([full(gf)] if final else []))
    return pl.pallas_call(
        functools.partial(_mix_ffn_body, n_mix=len(mixes), final=final),
        grid=(B // bb, T // tt),
        in_specs=specs,
        out_specs=row(D),
        out_shape=jax.ShapeDtypeStruct((B, T, D), F32),
        compiler_params=_cparams("arbitrary", "arbitrary"),
        name="mix_ffn",
    )(*args)


def _hgrn_body(*refs, C, n_sub, e, has_s0):
    if has_s0:
        z_ref, lbl_ref, gn_ref, s0_ref, ob_ref, sout_ref, st_scr = refs
    else:
        z_ref, lbl_ref, gn_ref, ob_ref, sout_ref, st_scr = refs
    step = pl.program_id(1)
    n_heads = st_scr.shape[0]
    kw = n_heads * DK_B

    @pl.when(step == 0)
    def _():
        for h in range(n_heads):
            st_scr[h] = s0_ref[0, h].T if has_s0 else jnp.zeros((DV_B, DK_B), F32)

    ll = lbl_ref[...]
    ex = jnp.exp(ll - jnp.max(ll, axis=0, keepdims=True))
    sm = ex / jnp.sum(ex, axis=0, keepdims=True)
    lb = jnp.sum(sm[:e + 1], axis=0, keepdims=True)
    r = lax.broadcasted_iota(I32, (C, C), 0)
    c = lax.broadcasted_iota(I32, (C, C), 1)
    causal = r >= c
    tri = jnp.where(causal, 1.0, 0.0)
    gn = gn_ref[...]
    heads = [(slice(h * DK_B, (h + 1) * DK_B), slice(h * DV_B, (h + 1) * DV_B))
             for h in range(n_heads)]
    intra, q_sts, decs, updates, gates = [], [], [], [], []
    for sub in range(n_sub):
        z = z_ref[0, sub * C:(sub + 1) * C, :]
        qh = _silu(z[:, :kw])
        f = lb + (1.0 - lb) * jax.nn.sigmoid(z[:, kw:2 * kw])
        kh = 1.0 - f
        vh = z[:, 2 * kw:3 * kw]
        gates.append(_silu(z[:, 3 * kw:]))
        G = jnp.dot(tri, jnp.log(f), precision=HIGHEST, preferred_element_type=F32)
        g_end = G[C - 1:C, :]
        mid = 0.5 * g_end
        q_in = (qh * jnp.exp(G - mid)).astype(BF16)
        k_in = (kh * jnp.exp(mid - G)).astype(BF16)
        k_st = (kh * jnp.exp(g_end - G)).astype(BF16)
        q_sts.append((qh * jnp.exp(G)).astype(BF16))
        decs.append(jnp.exp(g_end))
        o_sub, u_sub = [], []
        for sk, sv in heads:
            att = jnp.where(causal, _dot_nt(q_in[:, sk], k_in[:, sk]), 0.0)
            v_h = vh[:, sv]
            o_sub.append(_dot(att.astype(BF16), v_h.astype(BF16)))
            u_sub.append(_dot(v_h.T.astype(BF16), k_st[:, sk]))
        intra.append(o_sub)
        updates.append(u_sub)
    states = [st_scr[h] for h in range(n_heads)]
    for sub in range(n_sub):
        outs = []
        for h, (sk, sv) in enumerate(heads):
            o = intra[sub][h] + _dot_nt(q_sts[sub][:, sk], states[h].astype(BF16))
            states[h] = states[h] * decs[sub][:, sk] + updates[sub][h]
            outs.append(o * lax.rsqrt(jnp.mean(o * o, axis=-1, keepdims=True) + EPS))
        ob = jnp.concatenate(outs, axis=1) * gn * gates[sub]
        ob_ref[0, sub * C:(sub + 1) * C, :] = ob.astype(BF16)
    for h in range(n_heads):
        st_scr[h] = states[h]

    @pl.when(step == pl.num_programs(1) - 1)
    def _():
        for h in range(n_heads):
            sout_ref[0, h] = st_scr[h].T


def _hgrn(zb, lb_logits, g_norm, s0, e):
    B, T, zw = zb.shape
    kw = zw // 4
    n_heads = kw // DK_B
    C = min(CHUNK, T)
    ct = min(T, 4 * C)
    has_s0 = s0 is not None
    full = lambda a: pl.BlockSpec(a.shape, lambda b, t: (0,) * a.ndim)
    state = pl.BlockSpec((1, n_heads, DK_B, DV_B), lambda b, t: (b, 0, 0, 0))
    args = [zb, lb_logits, g_norm] + ([s0] if has_s0 else [])
    specs = ([pl.BlockSpec((1, ct, zw), lambda b, t: (b, t, 0)), full(lb_logits), full(g_norm)]
             + ([state] if has_s0 else []))
    return pl.pallas_call(
        functools.partial(_hgrn_body, C=C, n_sub=ct // C, e=e, has_s0=has_s0),
        grid=(B, T // ct),
        in_specs=specs,
        out_specs=[pl.BlockSpec((1, ct, kw), lambda b, t: (b, t, 0)), state],
        out_shape=[jax.ShapeDtypeStruct((B, T, kw), BF16),
                   jax.ShapeDtypeStruct((B, n_heads, DK_B, DV_B), F32)],
        scratch_shapes=[pltpu.VMEM((n_heads, DV_B, DK_B), F32)],
        compiler_params=_cparams("arbitrary", "arbitrary"),
        name="hgrn2",
    )(*args)


def _topk_threshold(count_ge, count_gt, count_tie_below, write_ties, rows, ksel, idx_bits):
    def thr_step(i, thr):
        cand = thr + lax.shift_left(I32(1), 31 - i)
        return jnp.where(count_ge(cand) >= ksel, cand, thr)

    thr = lax.fori_loop(0, 32, thr_step, jnp.full((rows, V7X_LANES), INT_MIN, I32))
    no_thr = thr == INT_MIN
    write_ties(thr)
    excess = jnp.where(jnp.logical_or(no_thr, count_ge(thr) <= ksel), 0, 1)

    def resolve_ties():
        need = jnp.where(no_thr, 0, ksel - count_gt(thr))

        def tie_step(i, end):
            cand = end + lax.shift_left(I32(1), idx_bits - 1 - i)
            return jnp.where(count_tie_below(cand) <= need, cand, end)

        return lax.fori_loop(0, idx_bits, tie_step, jnp.zeros((rows, V7X_LANES), I32))

    tie_end = lax.cond(jnp.max(excess) > 0, resolve_ties,
                       lambda: jnp.where(no_thr, 0, TIE_NONE).astype(I32))
    return thr, tie_end


def _lane_total(cnt):
    return jnp.broadcast_to(jnp.sum(cnt, axis=1, keepdims=True), cnt.shape)


I16_MIN = -(2**15)


def _dsa_prompt_t_body(q_ref, qi_ref, wi_ref, k_ref, vt_ref, ki3_ref, bias_ref, o_ref,
                       key_scr, hi_scr, lo_scr, tix_scr, neg_scr, qi3_scr, qz_scr,
                       m_scr, l_scr, acc_scr, s_scr, mx_scr, p_scr, *, tq, ksel, idx_bits):
    L = V7X_LANES
    n_heads = qz_scr.shape[0]
    qt = pl.program_id(1)
    nkb = qt + 1
    shift = CHUNK.bit_length() - 1
    row_i = lax.broadcasted_iota(I32, (tq, tq), 0)
    col_i = lax.broadcasted_iota(I32, (tq, tq), 1)
    t_chunk = (qt * tq + col_i) >> shift

    lane = lax.broadcasted_iota(I32, (tq, L), 1)
    qi = qi_ref[0]
    qi_hi = qi.astype(BF16).astype(F32)
    qi_lo = qi - qi_hi
    zero = jnp.zeros((tq, D_I), F32)
    for h in range(H_I):
        sl = slice(h * D_I, (h + 1) * D_I)
        qi3_scr[h] = jnp.concatenate([qi_hi[:, sl], qi_hi[:, sl], qi_lo[:, sl], zero],
                                     axis=1).astype(BF16)
    w_t = (wi_ref[0] * (H_I ** -0.5 * D_I ** -0.5)).T

    def index_block(kb, carry):
        ks = pl.multiple_of(kb * tq, tq)
        ki3 = ki3_ref[0, pl.ds(ks, tq), :]
        score = None
        for h in range(H_I):
            term = w_t[h:h + 1, :] * jnp.maximum(_dot_nt(ki3, qi3_scr[h]), 0.0)
            score = term if score is None else score + term
        key = _sortable(score)
        s_chunk = (ks + row_i) >> shift
        key = jnp.where(s_chunk <= t_chunk, key, INT_MIN)
        key_scr[kb] = key
        hi_scr[kb] = (key >> 16).astype(jnp.int16)
        lo_scr[kb] = ((key & 0xFFFF) + I16_MIN).astype(jnp.int16)
        return carry

    lax.fori_loop(0, nkb, index_block, 0)

    def count(scr, pred, rows, dtype):
        n_acc = 4
        one, nil = jnp.ones((), dtype), jnp.zeros((), dtype)

        def body(kb, accs):
            accs = list(accs)
            for r in range(tq // rows):
                hit = pred(scr[kb, r * rows:(r + 1) * rows, :])
                accs[r % n_acc] = accs[r % n_acc] + jnp.where(hit, one, nil)
            return tuple(accs)
        accs = lax.fori_loop(0, nkb, body, tuple(jnp.zeros((rows, tq), dtype) for _ in range(n_acc)))
        tot = functools.reduce(lambda a, b: a + b, [a.astype(I32) for a in accs])
        return jnp.sum(tot, axis=0, keepdims=True)

    def count16(scr, pred):
        return count(scr, pred, 16, jnp.int16)

    def count32(scr, pred):
        return count(scr, pred, 8, I32)

    def search16(scr, target):
        def step(i, thr):
            cand = thr + lax.shift_left(I32(1), 15 - i)
            cand16 = jnp.broadcast_to(cand.astype(jnp.int16), (16, tq))
            return jnp.where(count16(scr, lambda x: x >= cand16) >= target, cand, thr)
        return lax.fori_loop(0, 16, step, jnp.full((1, tq), I16_MIN, I32))

    t_hi = search16(hi_scr, ksel)
    t_hi16 = jnp.broadcast_to(t_hi.astype(jnp.int16), (16, tq))
    above = count16(hi_scr, lambda x: x > t_hi16)

    def keep_bucket(kb, carry):
        for r in range(tq // 16):
            rows = slice(r * 16, (r + 1) * 16)
            lo_scr[kb, rows, :] = jnp.where(hi_scr[kb, rows, :] == t_hi16, lo_scr[kb, rows, :],
                                            jnp.int16(I16_MIN))
        return carry

    lax.fori_loop(0, nkb, keep_bucket, 0)
    t_lo = search16(lo_scr, ksel - above)
    thr = lax.shift_left(t_hi, 16) + (t_lo - I16_MIN)
    thr8 = jnp.broadcast_to(thr, (8, tq))
    n_ge = count32(key_scr, lambda x: x >= thr8)
    no_thr = thr == INT_MIN
    excess = jnp.where(jnp.logical_or(no_thr, n_ge <= ksel), 0, 1)

    def resolve_ties():
        need = jnp.where(no_thr, 0, ksel - count32(key_scr, lambda x: x > thr8))

        def write(kb, carry):
            tix_scr[kb] = jnp.where(key_scr[kb] == jnp.broadcast_to(thr, (tq, tq)),
                                    kb * tq + row_i, TIE_NONE)
            return carry

        lax.fori_loop(0, nkb, write, 0)

        def step(i, end):
            cand = end + lax.shift_left(I32(1), idx_bits - 1 - i)
            cand8 = jnp.broadcast_to(cand, (8, tq))
            return jnp.where(count32(tix_scr, lambda x: x < cand8) <= need, cand, end)

        return lax.fori_loop(0, idx_bits, step, jnp.zeros((1, tq), I32))

    tie_end = lax.cond(jnp.max(excess) > 0, resolve_ties,
                       lambda: jnp.where(no_thr, 0, TIE_NONE).astype(I32))

    def write_mask(kb, carry):
        key = key_scr[kb]
        thr_b = jnp.broadcast_to(thr, (tq, tq))
        tied = jnp.where(key == thr_b, kb * tq + row_i, TIE_NONE) < jnp.broadcast_to(tie_end, (tq, tq))
        neg_scr[kb] = jnp.where(key > thr_b, 0.0, jnp.where(tied, 0.0, -jnp.inf)).astype(BF16)
        return carry

    lax.fori_loop(0, nkb, write_mask, 0)

    q = q_ref[0].astype(F32)
    for h in range(n_heads):
        qp = q[:, (h // 2) * L:(h // 2 + 1) * L]
        keep = (lane < DH) if h % 2 == 0 else (lane >= DH)
        qz_scr[h] = jnp.where(keep, qp, 0.0).astype(BF16)
        m_scr[h] = jnp.full((1, tq), -1e30, F32)
        l_scr[h] = jnp.zeros((1, tq), F32)
        acc_scr[h] = jnp.zeros((DH, tq), F32)

    rc = 32
    n_rc = tq // rc
    ones = jnp.ones((BF16_ROWS, tq), BF16)

    def scores(kb, h, half):
        ks = pl.multiple_of(kb * tq, tq)
        kp = k_ref[0, pl.ds(ks, tq), (h // 2) * L:(h // 2 + 1) * L]
        x = _dot_nt(kp, qz_scr[h]).astype(BF16) + bias_ref[qt - kb, h] + neg_scr[kb]
        s_scr[half * n_heads + h] = x
        mx_scr[half * n_heads + h] = jnp.max(x.reshape(tq // BF16_ROWS, BF16_ROWS, tq), axis=0)

    for h in range(n_heads):
        scores(0, h, 0)

    def attend_half(kb, cur):
        kb_next = jnp.minimum(kb + 1, nkb - 1)
        for h in range(n_heads):
            scores(kb_next, h, 1 - cur)
            m_old = m_scr[h]
            m_new = jnp.maximum(m_old, jnp.max(mx_scr[cur * n_heads + h].astype(F32),
                                               axis=0, keepdims=True))
            alpha = jnp.exp2(m_old - m_new)
            m_scr[h] = m_new
            m_b = jnp.broadcast_to(m_new.astype(BF16), (rc, tq))
            for r in range(n_rc):
                rows = slice(r * rc, (r + 1) * rc)
                p_scr[cur * n_heads + h, rows, :] = jnp.exp2(
                    s_scr[cur * n_heads + h, rows, :] - m_b)
            vt = jnp.concatenate([vt_ref[0, kb, h * DH:(h + 1) * DH, :], ones], axis=0)
            pv = _dot(vt, p_scr[cur * n_heads + h])
            l_scr[h] = alpha * l_scr[h] + pv[DH:DH + 1, :]
            acc_scr[h] = acc_scr[h] * alpha + pv[:DH, :]

    def attend_block(kb, carry):
        lax.cond(lax.rem(kb, 2) == 0, lambda: attend_half(kb, 0), lambda: attend_half(kb, 1))
        return carry

    lax.fori_loop(0, nkb, attend_block, 0)
    for j in range(n_heads // 2):
        o_t = jnp.concatenate([acc_scr[2 * j] / l_scr[2 * j],
                               acc_scr[2 * j + 1] / l_scr[2 * j + 1]], axis=0)
        o_ref[0, :, j * L:(j + 1) * L] = o_t.T.astype(BF16)


def _dsa_prompt_t(q, qi, wi, kb16, vt16, ki3, bias_t, ksel):
    B, T, aw = q.shape
    n_heads = aw // DH
    tq = bias_t.shape[-1]
    nq = T // tq
    assert T < 2**15 and vt16.shape == (B, nq, aw, tq)
    row = lambda n: pl.BlockSpec((1, tq, n), lambda b, t: (b, t, 0))
    seq = lambda n: pl.BlockSpec((1, T, n), lambda b, t: (b, 0, 0))
    return pl.pallas_call(
        functools.partial(_dsa_prompt_t_body, tq=tq, ksel=ksel, idx_bits=T.bit_length()),
        grid=(B, nq),
        in_specs=[row(aw), row(H_I * D_I), row(H_I), seq(aw),
                  pl.BlockSpec((1, nq, aw, tq), lambda b, t: (b, 0, 0, 0)), seq(4 * D_I),
                  pl.BlockSpec(bias_t.shape, lambda b, t: (0, 0, 0, 0),
                               pipeline_mode=pl.Buffered(1))],
        out_specs=row(aw),
        out_shape=jax.ShapeDtypeStruct((B, T, aw), BF16),
        scratch_shapes=[pltpu.VMEM((nq, tq, tq), I32), pltpu.VMEM((nq, tq, tq), jnp.int16),
                        pltpu.VMEM((nq, tq, tq), jnp.int16), pltpu.VMEM((nq, tq, tq), I32),
                        pltpu.VMEM((nq, tq, tq), BF16), pltpu.VMEM((H_I, tq, 4 * D_I), BF16),
                        pltpu.VMEM((n_heads, tq, V7X_LANES), BF16),
                        pltpu.VMEM((n_heads, 1, tq), F32), pltpu.VMEM((n_heads, 1, tq), F32),
                        pltpu.VMEM((n_heads, DH, tq), F32),
                        pltpu.VMEM((2 * n_heads, tq, tq), BF16),
                        pltpu.VMEM((2 * n_heads, BF16_ROWS, tq), BF16),
                        pltpu.VMEM((2 * n_heads, tq, tq), BF16)],
        compiler_params=_cparams("arbitrary", "arbitrary"),
        name="dsa_prompt",
    )(q, qi, wi, kb16, vt16, ki3, bias_t)


def _pack_heads(q, n_heads):
    T, W = q.shape
    head_of_lane = lax.broadcasted_iota(I32, (T, W), 1) >> (DH.bit_length() - 1)
    qf = q.astype(F32)
    return jnp.concatenate([jnp.where(head_of_lane == h, qf, 0.0) for h in range(n_heads)],
                           axis=0).astype(BF16)


def _unpack_heads(acc, den, n_heads):
    T = acc.shape[0] // n_heads
    W = acc.shape[1]
    head_of_lane = lax.broadcasted_iota(I32, (T, W), 1) >> (DH.bit_length() - 1)
    out = jnp.zeros((T, W), F32)
    for h in range(n_heads):
        blk = acc[h * T:(h + 1) * T, :] / den[h * T:(h + 1) * T, :]
        out = jnp.where(head_of_lane == h, blk, out)
    return out


def _gather_heads(ref):
    n_keys, n_heads = ref.shape[2], ref.shape[3]
    rows = ref.reshape(n_keys * n_heads, DH)
    pairs = []
    for j in range(n_heads // 2):
        a = rows[pl.ds(2 * j, n_keys, stride=n_heads), :]
        b = rows[pl.ds(2 * j + 1, n_keys, stride=n_heads), :]
        pairs.append(jnp.concatenate([a, b], axis=1).astype(BF16))
    return jnp.concatenate(pairs, axis=1)


def _dsa_sample_body(q_ref, qi_ref, wi_ref, kn_ref, vn_ref, kin_ref, kc_ref, vc_ref, kic_ref,
                     bias_ref, biasn_ref, o_ref,
                     key_scr, tix_scr, keyn_scr, tixn_scr, sel_scr, qbd_scr, m_scr, l_scr, acc_scr,
                     *, ts, kc_len, n_new, ksel, idx_bits):
    L = V7X_LANES
    n_heads = qbd_scr.shape[0] // ts
    nc = key_scr.shape[0]
    step = pl.program_id(1)
    past = nc * kc_len
    lane = lax.broadcasted_iota(I32, (ts, L), 1)

    @pl.when(step == 0)
    def _():
        qi = qi_ref[0]
        qi_hm = jnp.concatenate([qi[:, h * D_I:(h + 1) * D_I] for h in range(H_I)], axis=0)
        qi_hi, qi_lo = _split_bf16(qi_hm)
        w = wi_ref[0] * (H_I ** -0.5 * D_I ** -0.5)

        def scores(ki):
            ki_hi, ki_lo = _split_bf16(ki)
            s = _dot_nt(qi_hi, ki_hi) + _dot_nt(qi_hi, ki_lo) + _dot_nt(qi_lo, ki_hi)
            acc = None
            for h in range(H_I):
                term = w[:, h:h + 1] * jnp.maximum(s[h * ts:(h + 1) * ts, :], 0.0)
                acc = term if acc is None else acc + term
            return _sortable(acc)

        for c in range(nc):
            key_scr[c] = scores(kic_ref[0, c * kc_len:(c + 1) * kc_len, :])
        keyn_scr[...] = jnp.where(lane < n_new, scores(kin_ref[0]), INT_MIN)

        def count(pred_key=None, pred_tie=None):
            cnt = jnp.zeros((ts, L), I32)
            for c in range(nc):
                for i in range(kc_len // L):
                    if pred_key is not None:
                        hit = pred_key(key_scr[c, :, i * L:(i + 1) * L])
                    else:
                        hit = pred_tie(tix_scr[c, :, i * L:(i + 1) * L])
                    cnt = cnt + jnp.where(hit, 1, 0)
            hit = pred_key(keyn_scr[...]) if pred_key is not None else pred_tie(tixn_scr[...])
            return _lane_total(cnt + jnp.where(hit, 1, 0))

        def write_ties(thr):
            for c in range(nc):
                for i in range(kc_len // L):
                    idx = c * kc_len + i * L + lane
                    tix_scr[c, :, i * L:(i + 1) * L] = jnp.where(
                        key_scr[c, :, i * L:(i + 1) * L] == thr, idx, TIE_NONE)
            tixn_scr[...] = jnp.where(keyn_scr[...] == thr, past + lane, TIE_NONE)

        thr, tie_end = _topk_threshold(
            lambda x: count(pred_key=lambda k: k >= x),
            lambda x: count(pred_key=lambda k: k > x),
            lambda x: count(pred_tie=lambda t: t < x),
            write_ties, ts, ksel, idx_bits)
        sel_scr[0] = thr
        sel_scr[1] = tie_end
        qbd_scr[...] = _pack_heads(q_ref[0], n_heads)
        m_scr[...] = jnp.full(m_scr.shape, -1e30, F32)
        l_scr[...] = jnp.zeros(l_scr.shape, F32)
        acc_scr[...] = jnp.zeros(acc_scr.shape, F32)

    thr = sel_scr[0]
    tie_end = sel_scr[1]

    def attend(keys, ties, k16, v16, bias):
        n = keys.shape[1]
        negs = []
        for i in range(n // L):
            sel = jnp.logical_or(keys[:, i * L:(i + 1) * L] > thr,
                                 ties[:, i * L:(i + 1) * L] < tie_end)
            negs.append(jnp.where(sel, 0.0, -jnp.inf))
        neg = jnp.concatenate(negs, axis=1) if len(negs) > 1 else negs[0]
        neg = jnp.concatenate([neg] * n_heads, axis=0)
        s = _dot_nt(qbd_scr[...], k16) + bias + neg
        m_old = m_scr[...]
        m_new = jnp.maximum(m_old, jnp.max(s, axis=1, keepdims=True))
        alpha = jnp.exp(m_old - m_new)
        p = jnp.exp(s - m_new)
        l_scr[...] = alpha * l_scr[...] + jnp.sum(p, axis=1, keepdims=True)
        m_scr[...] = m_new
        acc_scr[...] = acc_scr[...] * alpha + _dot(p.astype(BF16), v16)

    attend(key_scr[step], tix_scr[step], _gather_heads(kc_ref), _gather_heads(vc_ref),
           bias_ref[...])

    @pl.when(step == nc - 1)
    def _():
        attend(keyn_scr[...], tixn_scr[...], kn_ref[0], vn_ref[0], biasn_ref[...])
        o_ref[0] = _unpack_heads(acc_scr[...], l_scr[...], n_heads).astype(BF16)


def _dsa_sample(q, qi, wi, kn16, vn16, kin, cache_k, cache_v, e, cache_ki, bias, n_new, ksel):
    B, ts, aw = q.shape
    n_heads = aw // DH
    P = cache_k.shape[2]
    npad = kn16.shape[1]
    kc_len = min(P, 1024)
    nc = P // kc_len
    L = V7X_LANES
    per_b = lambda a: pl.BlockSpec((1,) + a.shape[1:], lambda b, c: (b, 0, 0))
    chunk = pl.BlockSpec((1, 1, kc_len, n_heads, DH), lambda b, c: (e, b, c, 0, 0))
    rows = n_heads * ts
    return pl.pallas_call(
        functools.partial(_dsa_sample_body, ts=ts, kc_len=kc_len, n_new=n_new, ksel=ksel,
                          idx_bits=(P + npad).bit_length()),
        grid=(B, nc),
        in_specs=[per_b(q), per_b(qi), per_b(wi), per_b(kn16), per_b(vn16), per_b(kin),
                  chunk, chunk, per_b(cache_ki),
                  pl.BlockSpec((rows, kc_len), lambda b, c: (0, c)),
                  pl.BlockSpec((rows, npad), lambda b, c: (0, P // npad))],
        out_specs=pl.BlockSpec((1, ts, aw), lambda b, c: (b, 0, 0)),
        out_shape=jax.ShapeDtypeStruct((B, ts, aw), BF16),
        scratch_shapes=[pltpu.VMEM((nc, ts, kc_len), I32), pltpu.VMEM((nc, ts, kc_len), I32),
                        pltpu.VMEM((ts, npad), I32), pltpu.VMEM((ts, npad), I32),
                        pltpu.VMEM((2, ts, L), I32), pltpu.VMEM((rows, aw), BF16),
                        pltpu.VMEM((rows, 1), F32), pltpu.VMEM((rows, 1), F32),
                        pltpu.VMEM((rows, aw), F32)],
        compiler_params=_cparams("arbitrary", "arbitrary"),
        name="dsa_sample",
    )(q, qi, wi, kn16, vn16, kin, cache_k, cache_v, cache_ki, bias, bias)


def _band_prompt_body(q_ref, k_ref, vt_ref, bias_ref, o_ref, qz_scr, s_scr, p_scr, ot_scr,
                      *, tq, n_back):
    L = V7X_LANES
    n_heads = q_ref.shape[-1] // DH
    n_slots = s_scr.shape[0]
    n_p = p_scr.shape[0]
    nb = n_back + 1
    rc = 32
    n_rc = tq // rc
    qt = pl.program_id(1)
    lane = lax.broadcasted_iota(I32, (tq, L), 1)
    n_tiles = bias_ref.shape[0]
    blocks, starts, exists = [], [], []
    for j in range(nb):
        kb = qt - n_back + j
        blocks.append(jnp.maximum(kb, 0))
        starts.append(pl.multiple_of(jnp.maximum(kb, 0) * tq, tq))
        exists.append(kb >= 0)
    q = q_ref[0].astype(F32)
    for h in range(n_heads):
        qp = q[:, (h // 2) * L:(h // 2 + 1) * L]
        keep = (lane < DH) if h % 2 == 0 else (lane >= DH)
        qz_scr[h] = jnp.where(keep, qp, 0.0).astype(BF16)

    def scores(h):
        mx = None
        for j in range(nb):
            kp = k_ref[0, pl.ds(starts[j], tq), (h // 2) * L:(h // 2 + 1) * L]
            tile = jnp.where(exists[j], h * nb + j, n_tiles - 1)
            x = _dot_nt(kp, qz_scr[h]).astype(BF16) + bias_ref[tile]
            s_scr[h % n_slots, j] = x
            xm = jnp.max(x.reshape(tq // BF16_ROWS, BF16_ROWS, tq), axis=0)
            mx = xm if mx is None else jnp.maximum(mx, xm)
        return mx

    ones = jnp.ones((BF16_ROWS, tq), BF16)
    maxes = [scores(h) for h in range(n_slots - 1)]
    for h in range(n_heads):
        if h + n_slots - 1 < n_heads:
            maxes.append(scores(h + n_slots - 1))
        slot = h % n_slots
        m = jnp.max(maxes[h].astype(F32), axis=0, keepdims=True)
        m_b = jnp.broadcast_to(m.astype(BF16), (rc, tq))
        out = None
        for j in range(nb):
            for r in range(n_rc):
                rows = slice(r * rc, (r + 1) * rc)
                p_scr[h % n_p, j, rows, :] = jnp.exp2(s_scr[slot, j, rows, :] - m_b)
            vt = jnp.concatenate([vt_ref[0, blocks[j], h * DH:(h + 1) * DH, :], ones], axis=0)
            oj = _dot(vt, p_scr[h % n_p, j])
            out = oj if out is None else out + oj
        ot_scr[h * DH:(h + 1) * DH, :] = out[:DH, :] / out[DH:DH + 1, :]
    for pair in range(n_heads // 2):
        o_ref[0, :, pair * L:(pair + 1) * L] = ot_scr[pair * L:(pair + 1) * L, :].T.astype(BF16)


def _band_prompt(q, kb16, vt16, bias_t):
    B, T, cw = q.shape
    n_heads = cw // DH
    tq = bias_t.shape[-1]
    nq = T // tq
    n_back = (bias_t.shape[0] - 1) // n_heads - 1
    assert vt16.shape == (B, nq, cw, tq)
    row = pl.BlockSpec((1, tq, cw), lambda b, t: (b, t, 0))
    return pl.pallas_call(
        functools.partial(_band_prompt_body, tq=tq, n_back=n_back),
        grid=(B, nq),
        in_specs=[row, pl.BlockSpec((1, T, cw), lambda b, t: (b, 0, 0)),
                  pl.BlockSpec((1, nq, cw, tq), lambda b, t: (b, 0, 0, 0)),
                  pl.BlockSpec(bias_t.shape, lambda b, t: (0, 0, 0),
                               pipeline_mode=pl.Buffered(1))],
        out_specs=row,
        out_shape=jax.ShapeDtypeStruct((B, T, cw), BF16),
        scratch_shapes=[pltpu.VMEM((n_heads, tq, V7X_LANES), BF16),
                        pltpu.VMEM((6, n_back + 1, tq, tq), BF16),
                        pltpu.VMEM((6, n_back + 1, tq, tq), BF16),
                        pltpu.VMEM((cw, tq), F32)],
        compiler_params=_cparams("arbitrary", "arbitrary"),
        name="band_prompt",
    )(q, kb16, vt16, bias_t)


def _band_sample_body(q_ref, kn_ref, vn_ref, kc_ref, vc_ref, bias_ref, biasn_ref, o_ref):
    n_heads = q_ref.shape[-1] // DH
    qbd = _pack_heads(q_ref[0], n_heads)
    s_c = _dot_nt(qbd, _gather_heads(kc_ref)) + bias_ref[...]
    s_n = _dot_nt(qbd, kn_ref[0]) + biasn_ref[...]
    m = jnp.maximum(jnp.max(s_c, axis=1, keepdims=True), jnp.max(s_n, axis=1, keepdims=True))
    p_c = jnp.exp(s_c - m)
    p_n = jnp.exp(s_n - m)
    den = jnp.sum(p_c, axis=1, keepdims=True) + jnp.sum(p_n, axis=1, keepdims=True)
    acc = (_dot(p_c.astype(BF16), _gather_heads(vc_ref))
           + _dot(p_n.astype(BF16), vn_ref[0]))
    o_ref[0] = _unpack_heads(acc, den, n_heads).astype(BF16)


def _band_sample(q, kn16, vn16, cache_k, cache_v, o, bias):
    B, ts, cw = q.shape
    n_heads = cw // DH
    cb = cache_k.shape[2]
    npad = kn16.shape[1]
    rows = n_heads * ts
    per_b = lambda a: pl.BlockSpec((1,) + a.shape[1:], lambda b: (b, 0, 0))
    cache = pl.BlockSpec((1, 1, cb, n_heads, DH), lambda b: (o, b, 0, 0, 0))
    return pl.pallas_call(
        _band_sample_body,
        grid=(B,),
        in_specs=[per_b(q), per_b(kn16), per_b(vn16), cache, cache,
                  pl.BlockSpec((rows, cb), lambda b: (0, 0)),
                  pl.BlockSpec((rows, npad), lambda b: (0, cb // npad))],
        out_specs=per_b(q),
        out_shape=jax.ShapeDtypeStruct((B, ts, cw), BF16),
        compiler_params=_cparams("arbitrary"),
        name="band_sample",
    )(q, kn16, vn16, cache_k, cache_v, bias, bias)


def _pad_rows(a, n):
    return jnp.pad(a, ((0, 0), (0, n - a.shape[1]), (0, 0)))


def kernel(x_prompt, x_sample, c_prompt, c_sample, cache_a_k, cache_a_v, cache_a_kidx, state_b,
           cache_c_k, cache_c_v, w_ada, b_ada, norm_mix, norm_ffn, norm_final, w_in_even,
           w_out_even, t5_table, lb_logits, norm_b_out, w_in_odd, w_out_odd, rel_table_c,
           w_ffn_gate, w_ffn_up, w_ffn_down):
    Bp, T, D = x_prompt.shape
    Bs, Ts, _ = x_sample.shape
    P = cache_a_k.shape[2]
    c_buf = cache_c_k.shape[2]
    depth = w_ada.shape[0]
    L = V7X_LANES
    n_heads_a = cache_a_k.shape[3]
    aw = n_heads_a * DH
    n_heads_c = cache_c_k.shape[3]
    bw = w_out_even.shape[1] - aw
    iw = H_I * D_I
    ksel_p = min(K_TOP, T // 4)
    ksel_s = min(K_TOP, (P + Ts) // 4)
    c_keep = min(C_BACK * CHUNK, T)
    tq = min(T, 256)
    npad = L
    assert Ts <= npad and P % npad == 0 and c_buf % npad == 0 and T % tq == 0

    mod = _adaln_all(jnp.concatenate([c_prompt, c_sample], axis=0), w_ada, b_ada)

    def mods(l):
        parts = jnp.split(mod[l], 6, axis=-1)
        return ([p[:Bp].reshape(Bp, 1, D) for p in parts],
                [p[Bp:].reshape(Bs, 1, D) for p in parts])

    nq = T // tq
    t5_prompt = _bias_tiles(
        t5_table, groups=nq * n_heads_a, rows=tq, cols=tq, idx_fn=_t5_bucket,
        group_fn=lambda g: (g % n_heads_a, -(g // n_heads_a) * tq), keys_on_rows=True,
    ).reshape(nq, n_heads_a, tq, tq)
    t5_sample = _bias_tiles(
        t5_table, groups=n_heads_a, rows=Ts, cols=P + npad, idx_fn=_t5_bucket,
        group_fn=lambda g: (g, -P),
    ).reshape(n_heads_a * Ts, P + npad)

    xp, xs = x_prompt, x_sample
    outs_p = {k: [] for k in ("ak", "av", "ai", "bs", "ck", "cv")}
    outs_s = {k: [] for k in ("ak", "av", "ai", "bs", "ck", "cv")}
    for l in range(depth):
        (sh1p, sc1p, g1p, sh2p, sc2p, g2p), (sh1s, sc1s, g1s, sh2s, sc2s, g2s) = mods(l)
        gm = norm_mix[l].reshape(1, D)
        gn = norm_ffn[l].reshape(1, D)
        wg = w_ffn_gate[l].astype(BF16)
        wu = w_ffn_up[l].astype(BF16)
        wd = w_ffn_down[l].astype(BF16)
        gf = norm_final.reshape(1, D) if l == depth - 1 else None
        if l % 2 == 0:
            e = l // 2
            w_in = w_in_even[e]
            wa = w_in[:, :3 * aw].astype(BF16)
            wi_cols = w_in[:, 3 * aw:3 * aw + iw + D_I + H_I]
            wi_cols = jnp.pad(wi_cols, ((0, 0), (0, -wi_cols.shape[1] % L)))
            wh = wi_cols.astype(BF16)
            wl = (wi_cols - wh.astype(F32)).astype(BF16)
            wb = w_in[:, 3 * aw + iw + D_I + H_I:].astype(BF16)
            w_out = w_out_even[e].astype(BF16)
            gb_norm = norm_b_out[e].reshape(1, bw)

            q, k, v, k16, vt16, qi, ki, ki3, wi, zb = _in_even(xp, sc1p, sh1p, gm, wa, wh, wl, wb,
                                                              keys_on_rows=True)
            oa = _dsa_prompt_t(q, qi, wi, k16, vt16, ki3, t5_prompt, ksel_p)
            ob, st = _hgrn(zb, lb_logits, gb_norm, None, e)
            outs_p["ak"].append(k.reshape(Bp, T, n_heads_a, DH))
            outs_p["av"].append(v.reshape(Bp, T, n_heads_a, DH))
            outs_p["ai"].append(ki)
            outs_p["bs"].append(st)
            xp = _mix_ffn(xp, g1p, sc2p, sh2p, g2p, gn, [oa, ob], [w_out[:aw], w_out[aw:]],
                          wg, wu, wd, gf)

            q, k, v, k16, v16, qi, ki, _, wi, zb = _in_even(xs, sc1s, sh1s, gm, wa, wh, wl, wb,
                                                            keys_on_rows=False)
            oa = _dsa_sample(q, qi, wi, _pad_rows(k16, npad), _pad_rows(v16, npad),
                             _pad_rows(ki, npad), cache_a_k, cache_a_v, e, cache_a_kidx[e],
                             t5_sample,
                             Ts, ksel_s)
            ob, st = _hgrn(zb, lb_logits, gb_norm, state_b[e], e)
            outs_s["ak"].append(k.reshape(Bs, Ts, n_heads_a, DH))
            outs_s["av"].append(v.reshape(Bs, Ts, n_heads_a, DH))
            outs_s["ai"].append(ki)
            outs_s["bs"].append(st)
            xs = _mix_ffn(xs, g1s, sc2s, sh2s, g2s, gn, [oa, ob], [w_out[:aw], w_out[aw:]],
                          wg, wu, wd, gf)
        else:
            o = l // 2
            cw = n_heads_c * DH
            w_in = w_in_odd[o].astype(BF16)
            w_out = w_out_odd[o].astype(BF16)
            n_back = -(-(C_BACK * CHUNK) // tq)
            shift = CHUNK.bit_length() - 1

            n_band = n_heads_c * (n_back + 1)

            def band_mask(g, t, s, n_back=n_back, n_band=n_band):
                j = g % (n_back + 1)
                kc = (s + (j - n_back) * tq) >> shift
                tc = t >> shift
                return (kc <= tc) & (kc >= tc - C_BACK) & (g < n_band)

            band_p = _bias_tiles(
                rel_table_c[o], groups=n_band + 1, rows=tq, cols=tq, idx_fn=_clip_index,
                group_fn=lambda g, n_back=n_back: (jnp.minimum(g // (n_back + 1), n_heads_c - 1),
                                                   (g % (n_back + 1) - n_back) * tq),
                mask_fn=band_mask, keys_on_rows=True)
            band_s = _bias_tiles(
                rel_table_c[o], groups=n_heads_c, rows=Ts, cols=c_buf + npad, idx_fn=_clip_index,
                group_fn=lambda g: (g, -c_buf),
                mask_fn=lambda g, t, s: s < c_buf + Ts,
            ).reshape(n_heads_c * Ts, c_buf + npad)

            q, k, v, k16, vt16 = _in_odd(xp, sc1p, sh1p, gm, w_in, keys_on_rows=True)
            op = _band_prompt(q, k16, vt16, band_p)
            outs_p["ck"].append(k[:, T - c_keep:].reshape(Bp, c_keep, n_heads_c, DH))
            outs_p["cv"].append(v[:, T - c_keep:].reshape(Bp, c_keep, n_heads_c, DH))
            xp = _mix_ffn(xp, g1p, sc2p, sh2p, g2p, gn, [op], [w_out], wg, wu, wd, gf)

            q, k, v, k16, v16 = _in_odd(xs, sc1s, sh1s, gm, w_in, keys_on_rows=False)
            osm = _band_sample(q, _pad_rows(k16, npad), _pad_rows(v16, npad),
                               cache_c_k, cache_c_v, o, band_s)
            outs_s["ck"].append(k.reshape(Bs, Ts, n_heads_c, DH))
            outs_s["cv"].append(v.reshape(Bs, Ts, n_heads_c, DH))
            xs = _mix_ffn(xs, g1s, sc2s, sh2s, g2s, gn, [osm], [w_out], wg, wu, wd, gf)

    st = lambda xs_: jnp.stack(xs_)
    return (xp, xs, st(outs_p["ak"]), st(outs_p["av"]), st(outs_p["ai"]), st(outs_p["bs"]),
            st(outs_p["ck"]), st(outs_p["cv"]), st(outs_s["ak"]), st(outs_s["av"]),
            st(outs_s["ai"]), st(outs_s["bs"]), st(outs_s["ck"]), st(outs_s["cv"]))
```

```python
import functools
import math

import jax
import jax.numpy as jnp
from jax import lax
from jax.experimental import pallas as pl
from jax.experimental.pallas import tpu as pltpu

F32, BF16, I32 = jnp.float32, jnp.bfloat16, jnp.int32

CHUNK = 64
EPS = 1e-6
DH = 64
H_I = 8
D_I = 64
K_TOP = 256
N_BUCKETS = 32
T5_MAX_DIST = 1024
DK_B = 128
DV_B = 128
C_BACK = 8
REL_CLIP = 128

V7X_LANES = 128
BF16_ROWS = 16
V7X_VMEM_LIMIT = 56 * 2**20

INT_MIN = -(2**31)
TIE_NONE = 2**30
HIGHEST = lax.Precision.HIGHEST


def _cparams(*sem):
    return pltpu.CompilerParams(dimension_semantics=sem, vmem_limit_bytes=V7X_VMEM_LIMIT)


def _dot(a, b):
    return jnp.dot(a, b, preferred_element_type=F32)


def _dot_nt(a, b):
    return lax.dot_general(a, b, (((1,), (1,)), ((), ())), preferred_element_type=F32)


def _silu(x):
    return x * jax.nn.sigmoid(x)


def _norm_mod(x, g, scale, shift):
    ms = jnp.mean(x * x, axis=-1, keepdims=True)
    return (x * lax.rsqrt(ms + EPS) * g) * (1.0 + scale) + shift


def _sortable(x):
    b = pltpu.bitcast(x + 0.0, I32)
    return b ^ ((b >> 31) & I32(0x7FFFFFFF))


LOG2E = math.log2(math.e)


def _q_scale(base2):
    return DH ** -0.5 * (LOG2E if base2 else 1.0)


def _row_tiles(B, T, rows=256):
    tt = min(T, rows)
    bb = max(1, min(B, rows // tt))
    assert T % tt == 0 and B % bb == 0
    return bb, tt


def _ada_body(c_ref, w_ref, b_ref, o_ref):
    s = _silu(c_ref[...])
    o_ref[0] = jnp.dot(s, w_ref[0], precision=HIGHEST, preferred_element_type=F32) + b_ref[0]


def _adaln_all(c_all, w_ada, b_ada):
    depth, D, E = w_ada.shape
    R = c_all.shape[0]
    tn = 1024
    return pl.pallas_call(
        _ada_body,
        grid=(depth, E // tn),
        in_specs=[pl.BlockSpec((R, D), lambda l, n: (0, 0)),
                  pl.BlockSpec((1, D, tn), lambda l, n: (l, 0, n)),
                  pl.BlockSpec((1, 1, tn), lambda l, n: (l, 0, n))],
        out_specs=pl.BlockSpec((1, R, tn), lambda l, n: (l, 0, n)),
        out_shape=jax.ShapeDtypeStruct((depth, R, E), F32),
        compiler_params=_cparams("arbitrary", "arbitrary"),
        name="adaln",
    )(c_all, w_ada, b_ada.reshape(depth, 1, E))


def _t5_bucket(rel):
    nb = N_BUCKETS // 2
    max_exact = nb // 2
    base = jnp.where(rel > 0, nb, 0)
    n = jnp.abs(rel)
    large = max_exact + (jnp.log(jnp.maximum(n, 1).astype(F32) / max_exact)
                         / math.log(T5_MAX_DIST / max_exact) * (nb - max_exact)).astype(I32)
    large = jnp.minimum(large, nb - 1)
    return base + jnp.where(n < max_exact, n, large)


def _clip_index(rel):
    return jnp.clip(-rel, -REL_CLIP, REL_CLIP) + REL_CLIP


def _bias_body(tab_ref, o_ref, *, rows, cols, width, n_tab, idx_fn, group_fn, mask_fn,
               keys_on_rows):
    g = pl.program_id(0)
    head, off = group_fn(g)
    i = lax.broadcasted_iota(I32, (1, width), 1)
    diff = jnp.where(i < width - rows, i, i - width)
    idx = idx_fn(off - diff if keys_on_rows else off + diff)
    val = jnp.zeros((1, width), F32)
    for b in range(n_tab):
        val = jnp.where(idx == b, tab_ref[head, b], val)
    x = jnp.broadcast_to(val, (rows, width))
    y = pltpu.roll(x, 0, 1, stride=1, stride_axis=0)
    tile = y[:, :cols]
    if keys_on_rows:
        tile = tile * LOG2E
    if mask_fn is not None:
        t = lax.broadcasted_iota(I32, (rows, cols), 1 if keys_on_rows else 0)
        s = lax.broadcasted_iota(I32, (rows, cols), 0 if keys_on_rows else 1)
        tile = jnp.where(mask_fn(g, t, s), tile, -jnp.inf)
    o_ref[0] = tile.astype(o_ref.dtype)


def _bias_tiles(table, *, groups, rows, cols, idx_fn, group_fn, mask_fn=None,
                keys_on_rows=False):
    n_tab = table.shape[0]
    width = -(-(rows + cols) // V7X_LANES) * V7X_LANES
    body = functools.partial(_bias_body, rows=rows, cols=cols, width=width, n_tab=n_tab,
                             idx_fn=idx_fn, group_fn=group_fn, mask_fn=mask_fn,
                             keys_on_rows=keys_on_rows)
    return pl.pallas_call(
        body,
        grid=(groups,),
        in_specs=[pl.BlockSpec(memory_space=pltpu.SMEM)],
        out_specs=pl.BlockSpec((1, rows, cols), lambda g: (g, 0, 0)),
        out_shape=jax.ShapeDtypeStruct((groups, rows, cols), BF16 if keys_on_rows else F32),
        compiler_params=_cparams("arbitrary"),
        name="bias_tiles",
    )(table.T)


def _in_even_body(x_ref, sc_ref, sh_ref, g_ref, wa_ref, wb_ref,
                  q_ref, k_ref, v_ref, kb_ref, vb_ref, qi_ref, ki_ref, ki2_ref, wi_ref, zb_ref,
                  *, aw, keys_on_rows):
    bb, tt, D = x_ref.shape
    M = bb * tt
    h = _norm_mod(x_ref[...], g_ref[...].reshape(1, 1, D), sc_ref[...], sh_ref[...]).reshape(M, D)
    h = h.astype(BF16)
    za = _dot(h, wa_ref[...])
    q_ref[...] = (za[:, :aw] * _q_scale(keys_on_rows)).astype(BF16).reshape(bb, tt, aw)
    k = za[:, aw:2 * aw]
    v = za[:, 2 * aw:3 * aw]
    k_ref[...] = k.reshape(bb, tt, aw)
    v_ref[...] = v.reshape(bb, tt, aw)
    kb_ref[...] = k.astype(BF16).reshape(bb, tt, aw)
    if keys_on_rows:
        vb_ref[0, 0] = v.T.astype(BF16)
    else:
        vb_ref[...] = v.astype(BF16).reshape(bb, tt, aw)
    iw = H_I * D_I
    zi = za[:, 3 * aw:]
    ki = zi[:, iw:iw + D_I]
    qi_ref[...] = zi[:, :iw].astype(BF16).reshape(bb, tt, iw)
    ki_ref[...] = ki.reshape(bb, tt, D_I)
    ki2_ref[...] = jnp.concatenate([ki, ki], axis=1).astype(BF16).reshape(bb, tt, 2 * D_I)
    wi_ref[...] = zi[:, iw + D_I:iw + D_I + H_I].reshape(bb, tt, H_I)
    zb_ref[...] = _dot(h, wb_ref[...]).reshape(bb, tt, zb_ref.shape[-1])


def _in_even(x, scale, shift, g, wa, wb, aw, keys_on_rows):
    B, T, D = x.shape
    bw = wb.shape[1]
    iw = H_I * D_I
    bb, tt = _row_tiles(B, T)
    assert not keys_on_rows or bb == 1
    row = lambda n: pl.BlockSpec((bb, tt, n), lambda b, t: (b, t, 0))
    mod = pl.BlockSpec((bb, 1, D), lambda b, t: (b, 0, 0))
    full = lambda a: pl.BlockSpec(a.shape, lambda b, t: (0,) * a.ndim)
    outs = [(aw, BF16), (aw, F32), (aw, F32), (aw, BF16), (aw, BF16), (iw, BF16), (D_I, F32),
            (2 * D_I, BF16), (H_I, F32), (bw, F32)]
    out_specs = [row(n) for n, _ in outs]
    out_shape = [jax.ShapeDtypeStruct((B, T, n), dt) for n, dt in outs]
    if keys_on_rows:
        out_specs[4] = pl.BlockSpec((1, 1, aw, tt), lambda b, t: (b, t, 0, 0))
        out_shape[4] = jax.ShapeDtypeStruct((B, T // tt, aw, tt), BF16)
    return pl.pallas_call(
        functools.partial(_in_even_body, aw=aw, keys_on_rows=keys_on_rows),
        grid=(B // bb, T // tt),
        in_specs=[row(D), mod, mod, full(g), full(wa), full(wb)],
        out_specs=out_specs,
        out_shape=out_shape,
        compiler_params=_cparams("arbitrary", "arbitrary"),
        name="in_even",
    )(x, scale, shift, g, wa, wb)


def _in_odd_body(x_ref, sc_ref, sh_ref, g_ref, w_ref, q_ref, k_ref, v_ref, kb_ref, vb_ref,
                 *, keys_on_rows):
    bb, tt, D = x_ref.shape
    M = bb * tt
    cw = q_ref.shape[-1]
    h = _norm_mod(x_ref[...], g_ref[...].reshape(1, 1, D), sc_ref[...], sh_ref[...]).reshape(M, D)
    z = _dot(h.astype(BF16), w_ref[...])
    q_ref[...] = (z[:, :cw] * _q_scale(keys_on_rows)).astype(BF16).reshape(bb, tt, cw)
    k = z[:, cw:2 * cw]
    v = z[:, 2 * cw:]
    k_ref[...] = k.reshape(bb, tt, cw)
    v_ref[...] = v.reshape(bb, tt, cw)
    kb_ref[...] = k.astype(BF16).reshape(bb, tt, cw)
    if keys_on_rows:
        vb_ref[0, 0] = v.T.astype(BF16)
    else:
        vb_ref[...] = v.astype(BF16).reshape(bb, tt, cw)


def _in_odd(x, scale, shift, g, w, keys_on_rows):
    B, T, D = x.shape
    cw = w.shape[1] // 3
    bb, tt = _row_tiles(B, T)
    assert not keys_on_rows or bb == 1
    row = lambda n: pl.BlockSpec((bb, tt, n), lambda b, t: (b, t, 0))
    mod = pl.BlockSpec((bb, 1, D), lambda b, t: (b, 0, 0))
    full = lambda a: pl.BlockSpec(a.shape, lambda b, t: (0,) * a.ndim)
    dts = [BF16, F32, F32, BF16, BF16]
    out_specs = [row(cw) for _ in dts]
    out_shape = [jax.ShapeDtypeStruct((B, T, cw), dt) for dt in dts]
    if keys_on_rows:
        out_specs[4] = pl.BlockSpec((1, 1, cw, tt), lambda b, t: (b, t, 0, 0))
        out_shape[4] = jax.ShapeDtypeStruct((B, T // tt, cw, tt), BF16)
    return pl.pallas_call(
        functools.partial(_in_odd_body, keys_on_rows=keys_on_rows),
        grid=(B // bb, T // tt),
        in_specs=[row(D), mod, mod, full(g), full(w)],
        out_specs=out_specs,
        out_shape=out_shape,
        compiler_params=_cparams("arbitrary", "arbitrary"),
        name="in_odd",
    )(x, scale, shift, g, w)


def _mix_ffn_body(*refs, n_mix, final):
    x_ref, g1_ref, sc2_ref, sh2_ref, g2_ref, gn_ref = refs[:6]
    mix_refs = refs[6:6 + n_mix]
    wmix_refs = refs[6 + n_mix:6 + 2 * n_mix]
    wg_ref, wu_ref, wd_ref = refs[6 + 2 * n_mix:9 + 2 * n_mix]
    gf_ref = refs[9 + 2 * n_mix] if final else None
    o_ref = refs[-1]
    bb, tt, D = x_ref.shape
    M = bb * tt
    m = None
    for mr, wr in zip(mix_refs, wmix_refs):
        part = _dot(mr[...].reshape(M, mr.shape[-1]), wr[...])
        m = part if m is None else m + part
    x1 = x_ref[...] + g1_ref[...] * m.reshape(bb, tt, D)
    h2 = _norm_mod(x1, gn_ref[...].reshape(1, 1, D), sc2_ref[...], sh2_ref[...])
    h2 = h2.reshape(M, D).astype(BF16)
    a = _dot(h2, wg_ref[...])
    b = _dot(h2, wu_ref[...])
    u = (_silu(a) * b).astype(BF16)
    f = _dot(u, wd_ref[...])
    x2 = x1 + g2_ref[...] * f.reshape(bb, tt, D)
    if final:
        ms = jnp.mean(x2 * x2, axis=-1, keepdims=True)
        x2 = x2 * lax.rsqrt(ms + EPS) * gf_ref[...].reshape(1, 1, D)
    o_ref[...] = x2


def _mix_ffn(x, g1, sc2, sh2, g2, gn, mixes, wmixes, wg, wu, wd, gf=None):
    B, T, D = x.shape
    bb, tt = _row_tiles(B, T)
    row = lambda n: pl.BlockSpec((bb, tt, n), lambda b, t: (b, t, 0))
    mod = pl.BlockSpec((bb, 1, D), lambda b, t: (b, 0, 0))
    full = lambda a: pl.BlockSpec(a.shape, lambda b, t: (0,) * a.ndim,
                                  pipeline_mode=pl.Buffered(1))
    final = gf is not None
    args = [x, g1, sc2, sh2, g2, gn, *mixes, *wmixes, wg, wu, wd] + ([gf] if final else [])
    specs = ([row(D), mod, mod, mod, mod, full(gn)] + [row(m.shape[-1]) for m in mixes]
             + [full(w) for w in wmixes] + [full(wg), full(wu), full(wd)]
             + ([full(gf)] if final else []))
    return pl.pallas_call(
        functools.partial(_mix_ffn_body, n_mix=len(mixes), final=final),
        grid=(B // bb, T // tt),
        in_specs=specs,
        out_specs=row(D),
        out_shape=jax.ShapeDtypeStruct((B, T, D), F32),
        compiler_params=_cparams("arbitrary", "arbitrary"),
        name="mix_ffn",
    )(*args)


def _hgrn_body(*refs, C, n_sub, e, has_s0):
    if has_s0:
        z_ref, lbl_ref, gn_ref, s0_ref, ob_ref, sout_ref, st_scr = refs
    else:
        z_ref, lbl_ref, gn_ref, ob_ref, sout_ref, st_scr = refs
    step = pl.program_id(1)
    n_heads = st_scr.shape[0]
    kw = n_heads * DK_B

    @pl.when(step == 0)
    def _():
        for h in range(n_heads):
            st_scr[h] = s0_ref[0, h].T if has_s0 else jnp.zeros((DV_B, DK_B), F32)

    ll = lbl_ref[...]
    ex = jnp.exp(ll - jnp.max(ll, axis=0, keepdims=True))
    sm = ex / jnp.sum(ex, axis=0, keepdims=True)
    lb = jnp.sum(sm[:e + 1], axis=0, keepdims=True)
    r = lax.broadcasted_iota(I32, (C, C), 0)
    c = lax.broadcasted_iota(I32, (C, C), 1)
    causal = r >= c
    tri = jnp.where(causal, 1.0, 0.0)
    gn = gn_ref[...]
    heads = [(slice(h * DK_B, (h + 1) * DK_B), slice(h * DV_B, (h + 1) * DV_B))
             for h in range(n_heads)]
    intra, q_sts, decs, updates, gates = [], [], [], [], []
    for sub in range(n_sub):
        z = z_ref[0, sub * C:(sub + 1) * C, :]
        qh = _silu(z[:, :kw])
        f = lb + (1.0 - lb) * jax.nn.sigmoid(z[:, kw:2 * kw])
        kh = 1.0 - f
        vh = z[:, 2 * kw:3 * kw]
        gates.append(_silu(z[:, 3 * kw:]))
        G = jnp.dot(tri, jnp.log(f), precision=HIGHEST, preferred_element_type=F32)
        g_end = G[C - 1:C, :]
        mid = 0.5 * g_end
        e_mid = jnp.exp(mid)
        q_mid = qh * jnp.exp(G - mid)
        k_mid = kh * jnp.exp(mid - G)
        q_in = q_mid.astype(BF16)
        k_in = k_mid.astype(BF16)
        k_st = (k_mid * e_mid).astype(BF16)
        q_sts.append((q_mid * e_mid).astype(BF16))
        decs.append(e_mid * e_mid)
        o_sub, u_sub = [], []
        for sk, sv in heads:
            att = jnp.where(causal, _dot_nt(q_in[:, sk], k_in[:, sk]), 0.0)
            v_h = vh[:, sv]
            o_sub.append(_dot(att.astype(BF16), v_h.astype(BF16)))
            u_sub.append(_dot(v_h.T.astype(BF16), k_st[:, sk]))
        intra.append(o_sub)
        updates.append(u_sub)
    states = [st_scr[h] for h in range(n_heads)]
    for sub in range(n_sub):
        outs = []
        for h, (sk, sv) in enumerate(heads):
            o = intra[sub][h] + _dot_nt(q_sts[sub][:, sk], states[h].astype(BF16))
            states[h] = states[h] * decs[sub][:, sk] + updates[sub][h]
            outs.append(o * lax.rsqrt(jnp.mean(o * o, axis=-1, keepdims=True) + EPS))
        ob = jnp.concatenate(outs, axis=1) * gn * gates[sub]
        ob_ref[0, sub * C:(sub + 1) * C, :] = ob.astype(BF16)
    for h in range(n_heads):
        st_scr[h] = states[h]

    @pl.when(step == pl.num_programs(1) - 1)
    def _():
        for h in range(n_heads):
            sout_ref[0, h] = st_scr[h].T


def _hgrn(zb, lb_logits, g_norm, s0, e):
    B, T, zw = zb.shape
    kw = zw // 4
    n_heads = kw // DK_B
    C = min(CHUNK, T)
    ct = min(T, 4 * C)
    has_s0 = s0 is not None
    full = lambda a: pl.BlockSpec(a.shape, lambda b, t: (0,) * a.ndim)
    state = pl.BlockSpec((1, n_heads, DK_B, DV_B), lambda b, t: (b, 0, 0, 0))
    args = [zb, lb_logits, g_norm] + ([s0] if has_s0 else [])
    specs = ([pl.BlockSpec((1, ct, zw), lambda b, t: (b, t, 0)), full(lb_logits), full(g_norm)]
             + ([state] if has_s0 else []))
    return pl.pallas_call(
        functools.partial(_hgrn_body, C=C, n_sub=ct // C, e=e, has_s0=has_s0),
        grid=(B, T // ct),
        in_specs=specs,
        out_specs=[pl.BlockSpec((1, ct, kw), lambda b, t: (b, t, 0)), state],
        out_shape=[jax.ShapeDtypeStruct((B, T, kw), BF16),
                   jax.ShapeDtypeStruct((B, n_heads, DK_B, DV_B), F32)],
        scratch_shapes=[pltpu.VMEM((n_heads, DV_B, DK_B), F32)],
        compiler_params=_cparams("arbitrary", "arbitrary"),
        name="hgrn2",
    )(*args)


def _topk_threshold(count_ge, count_gt, count_tie_below, write_ties, rows, ksel, idx_bits):
    def thr_step(i, thr):
        cand = thr + lax.shift_left(I32(1), 31 - i)
        return jnp.where(count_ge(cand) >= ksel, cand, thr)

    thr = lax.fori_loop(0, 32, thr_step, jnp.full((rows, V7X_LANES), INT_MIN, I32))
    no_thr = thr == INT_MIN
    write_ties(thr)
    excess = jnp.where(jnp.logical_or(no_thr, count_ge(thr) <= ksel), 0, 1)

    def resolve_ties():
        need = jnp.where(no_thr, 0, ksel - count_gt(thr))

        def tie_step(i, end):
            cand = end + lax.shift_left(I32(1), idx_bits - 1 - i)
            return jnp.where(count_tie_below(cand) <= need, cand, end)

        return lax.fori_loop(0, idx_bits, tie_step, jnp.zeros((rows, V7X_LANES), I32))

    tie_end = lax.cond(jnp.max(excess) > 0, resolve_ties,
                       lambda: jnp.where(no_thr, 0, TIE_NONE).astype(I32))
    return thr, tie_end


def _lane_total(cnt):
    return jnp.broadcast_to(jnp.sum(cnt, axis=1, keepdims=True), cnt.shape)


I16_MIN = -(2**15)


def _dsa_prompt_t_body(q_ref, qi_ref, wi_ref, k_ref, vt_ref, ki2_ref, bias_ref, o_ref,
                       key_scr, hi_scr, lo_scr, tix_scr, neg_scr, qiz_scr, qz_scr,
                       m_scr, l_scr, acc_scr, s_scr, mx_scr, p_scr, *, tq, ksel, idx_bits):
    L = V7X_LANES
    n_heads = qz_scr.shape[0]
    qt = pl.program_id(1)
    nkb = qt + 1
    shift = CHUNK.bit_length() - 1
    row_i = lax.broadcasted_iota(I32, (tq, tq), 0)
    col_i = lax.broadcasted_iota(I32, (tq, tq), 1)
    t_chunk = (qt * tq + col_i) >> shift

    lane = lax.broadcasted_iota(I32, (tq, L), 1)
    qi = qi_ref[0].astype(F32)
    for h in range(H_I):
        keep = (lane < D_I) if h % 2 == 0 else (lane >= D_I)
        qiz_scr[h] = jnp.where(keep, qi[:, (h // 2) * L:(h // 2 + 1) * L], 0.0).astype(BF16)
    w_t = (wi_ref[0] * (H_I ** -0.5 * D_I ** -0.5)).T

    def index_block(kb, carry):
        ks = pl.multiple_of(kb * tq, tq)
        ki2 = ki2_ref[0, pl.ds(ks, tq), :]
        score = None
        for h in range(H_I):
            term = w_t[h:h + 1, :] * jnp.maximum(_dot_nt(ki2, qiz_scr[h]), 0.0)
            score = term if score is None else score + term
        key = _sortable(score)
        s_chunk = (ks + row_i) >> shift
        key = jnp.where(s_chunk <= t_chunk, key, INT_MIN)
        key_scr[kb] = key
        hi_scr[kb] = (key >> 16).astype(jnp.int16)
        lo_scr[kb] = ((key & 0xFFFF) + I16_MIN).astype(jnp.int16)
        return carry

    lax.fori_loop(0, nkb, index_block, 0)

    def count(scr, pred, rows, dtype):
        n_acc = 4
        one, nil = jnp.ones((), dtype), jnp.zeros((), dtype)

        def body(kb, accs):
            accs = list(accs)
            for r in range(tq // rows):
                hit = pred(scr[kb, r * rows:(r + 1) * rows, :])
                accs[r % n_acc] = accs[r % n_acc] + jnp.where(hit, one, nil)
            return tuple(accs)
        accs = lax.fori_loop(0, nkb, body, tuple(jnp.zeros((rows, tq), dtype) for _ in range(n_acc)))
        tot = functools.reduce(lambda a, b: a + b, [a.astype(I32) for a in accs])
        return jnp.sum(tot, axis=0, keepdims=True)

    def count16(scr, pred):
        return count(scr, pred, 16, jnp.int16)

    def count32(scr, pred):
        return count(scr, pred, 8, I32)

    def search16(scr, target):
        def step(i, thr):
            cand = thr + lax.shift_left(I32(1), 15 - i)
            cand16 = jnp.broadcast_to(cand.astype(jnp.int16), (16, tq))
            return jnp.where(count16(scr, lambda x: x >= cand16) >= target, cand, thr)
        return lax.fori_loop(0, 16, step, jnp.full((1, tq), I16_MIN, I32))

    t_hi = search16(hi_scr, ksel)
    t_hi16 = jnp.broadcast_to(t_hi.astype(jnp.int16), (16, tq))
    above = count16(hi_scr, lambda x: x > t_hi16)

    def keep_bucket(kb, carry):
        for r in range(tq // 16):
            rows = slice(r * 16, (r + 1) * 16)
            lo_scr[kb, rows, :] = jnp.where(hi_scr[kb, rows, :] == t_hi16, lo_scr[kb, rows, :],
                                            jnp.int16(I16_MIN))
        return carry

    lax.fori_loop(0, nkb, keep_bucket, 0)
    t_lo = search16(lo_scr, ksel - above)
    thr = lax.shift_left(t_hi, 16) + (t_lo - I16_MIN)
    thr8 = jnp.broadcast_to(thr, (8, tq))
    n_ge = count32(key_scr, lambda x: x >= thr8)
    no_thr = thr == INT_MIN
    excess = jnp.where(jnp.logical_or(no_thr, n_ge <= ksel), 0, 1)

    def resolve_ties():
        need = jnp.where(no_thr, 0, ksel - count32(key_scr, lambda x: x > thr8))

        def write(kb, carry):
            tix_scr[kb] = jnp.where(key_scr[kb] == jnp.broadcast_to(thr, (tq, tq)),
                                    kb * tq + row_i, TIE_NONE)
            return carry

        lax.fori_loop(0, nkb, write, 0)

        def step(i, end):
            cand = end + lax.shift_left(I32(1), idx_bits - 1 - i)
            cand8 = jnp.broadcast_to(cand, (8, tq))
            return jnp.where(count32(tix_scr, lambda x: x < cand8) <= need, cand, end)

        return lax.fori_loop(0, idx_bits, step, jnp.zeros((1, tq), I32))

    tie_end = lax.cond(jnp.max(excess) > 0, resolve_ties,
                       lambda: jnp.where(no_thr, 0, TIE_NONE).astype(I32))

    def write_mask(kb, carry):
        key = key_scr[kb]
        thr_b = jnp.broadcast_to(thr, (tq, tq))
        tied = jnp.where(key == thr_b, kb * tq + row_i, TIE_NONE) < jnp.broadcast_to(tie_end, (tq, tq))
        neg_scr[kb] = jnp.where(key > thr_b, 0.0, jnp.where(tied, 0.0, -jnp.inf)).astype(BF16)
        return carry

    lax.fori_loop(0, nkb, write_mask, 0)

    q = q_ref[0].astype(F32)
    for h in range(n_heads):
        qp = q[:, (h // 2) * L:(h // 2 + 1) * L]
        keep = (lane < DH) if h % 2 == 0 else (lane >= DH)
        qz_scr[h] = jnp.where(keep, qp, 0.0).astype(BF16)
        m_scr[h] = jnp.full((1, tq), -1e30, F32)
        l_scr[h] = jnp.zeros((1, tq), F32)
        acc_scr[h] = jnp.zeros((DH, tq), F32)

    rc = 32
    n_rc = tq // rc
    ones = jnp.ones((BF16_ROWS, tq), BF16)

    def scores(kb, h, half):
        ks = pl.multiple_of(kb * tq, tq)
        kp = k_ref[0, pl.ds(ks, tq), (h // 2) * L:(h // 2 + 1) * L]
        x = _dot_nt(kp, qz_scr[h]).astype(BF16) + bias_ref[qt - kb, h] + neg_scr[kb]
        s_scr[half * n_heads + h] = x
        mx_scr[half * n_heads + h] = jnp.max(x.reshape(tq // BF16_ROWS, BF16_ROWS, tq), axis=0)

    for h in range(n_heads):
        scores(0, h, 0)

    def attend_half(kb, cur):
        kb_next = jnp.minimum(kb + 1, nkb - 1)
        for h in range(n_heads):
            scores(kb_next, h, 1 - cur)
            m_old = m_scr[h]
            m_new = jnp.maximum(m_old, jnp.max(mx_scr[cur * n_heads + h].astype(F32),
                                               axis=0, keepdims=True))
            alpha = jnp.exp2(m_old - m_new)
            m_scr[h] = m_new
            m_b = jnp.broadcast_to(m_new.astype(BF16), (rc, tq))
            for r in range(n_rc):
                rows = slice(r * rc, (r + 1) * rc)
                p_scr[cur * n_heads + h, rows, :] = jnp.exp2(
                    s_scr[cur * n_heads + h, rows, :] - m_b)
            vt = jnp.concatenate([vt_ref[0, kb, h * DH:(h + 1) * DH, :], ones], axis=0)
            pv = _dot(vt, p_scr[cur * n_heads + h])
            l_scr[h] = alpha * l_scr[h] + pv[DH:DH + 1, :]
            acc_scr[h] = acc_scr[h] * alpha + pv[:DH, :]

    def attend_block(kb, carry):
        lax.cond(lax.rem(kb, 2) == 0, lambda: attend_half(kb, 0), lambda: attend_half(kb, 1))
        return carry

    lax.fori_loop(0, nkb, attend_block, 0)
    for j in range(n_heads // 2):
        o_t = jnp.concatenate([acc_scr[2 * j] / l_scr[2 * j],
                               acc_scr[2 * j + 1] / l_scr[2 * j + 1]], axis=0)
        o_ref[0, :, j * L:(j + 1) * L] = o_t.T.astype(BF16)


def _dsa_prompt_t(q, qi, wi, kb16, vt16, ki2, bias_t, ksel):
    B, T, aw = q.shape
    n_heads = aw // DH
    tq = bias_t.shape[-1]
    nq = T // tq
    assert T < 2**15 and vt16.shape == (B, nq, aw, tq)
    row = lambda n: pl.BlockSpec((1, tq, n), lambda b, t: (b, t, 0))
    seq = lambda n: pl.BlockSpec((1, T, n), lambda b, t: (b, 0, 0))
    return pl.pallas_call(
        functools.partial(_dsa_prompt_t_body, tq=tq, ksel=ksel, idx_bits=T.bit_length()),
        grid=(B, nq),
        in_specs=[row(aw), row(H_I * D_I), row(H_I), seq(aw),
                  pl.BlockSpec((1, nq, aw, tq), lambda b, t: (b, 0, 0, 0)), seq(2 * D_I),
                  pl.BlockSpec(bias_t.shape, lambda b, t: (0, 0, 0, 0),
                               pipeline_mode=pl.Buffered(1))],
        out_specs=row(aw),
        out_shape=jax.ShapeDtypeStruct((B, T, aw), BF16),
        scratch_shapes=[pltpu.VMEM((nq, tq, tq), I32), pltpu.VMEM((nq, tq, tq), jnp.int16),
                        pltpu.VMEM((nq, tq, tq), jnp.int16), pltpu.VMEM((nq, tq, tq), I32),
                        pltpu.VMEM((nq, tq, tq), BF16), pltpu.VMEM((H_I, tq, V7X_LANES), BF16),
                        pltpu.VMEM((n_heads, tq, V7X_LANES), BF16),
                        pltpu.VMEM((n_heads, 1, tq), F32), pltpu.VMEM((n_heads, 1, tq), F32),
                        pltpu.VMEM((n_heads, DH, tq), F32),
                        pltpu.VMEM((2 * n_heads, tq, tq), BF16),
                        pltpu.VMEM((2 * n_heads, BF16_ROWS, tq), BF16),
                        pltpu.VMEM((2 * n_heads, tq, tq), BF16)],
        compiler_params=_cparams("arbitrary", "arbitrary"),
        name="dsa_prompt",
    )(q, qi, wi, kb16, vt16, ki2, bias_t)


def _pack_heads(q, n_heads):
    T, W = q.shape
    head_of_lane = lax.broadcasted_iota(I32, (T, W), 1) >> (DH.bit_length() - 1)
    qf = q.astype(F32)
    return jnp.concatenate([jnp.where(head_of_lane == h, qf, 0.0) for h in range(n_heads)],
                           axis=0).astype(BF16)


def _unpack_heads(acc, den, n_heads):
    T = acc.shape[0] // n_heads
    W = acc.shape[1]
    head_of_lane = lax.broadcasted_iota(I32, (T, W), 1) >> (DH.bit_length() - 1)
    out = jnp.zeros((T, W), F32)
    for h in range(n_heads):
        blk = acc[h * T:(h + 1) * T, :] / den[h * T:(h + 1) * T, :]
        out = jnp.where(head_of_lane == h, blk, out)
    return out


def _dsa_sample_body(q_ref, qi_ref, wi_ref, kn_ref, vn_ref, kin_ref, kc_ref, vc_ref, kic_ref,
                     bias_ref, biasn_ref, o_ref,
                     key_scr, tix_scr, keyn_scr, tixn_scr, sel_scr, qbd_scr, m_scr, l_scr, acc_scr,
                     *, ts, kc_len, n_new, ksel, idx_bits):
    L = V7X_LANES
    n_heads = qbd_scr.shape[0] // ts
    nc = key_scr.shape[0]
    step = pl.program_id(1)
    past = nc * kc_len
    lane = lax.broadcasted_iota(I32, (ts, L), 1)

    @pl.when(step == 0)
    def _():
        qi = qi_ref[0].astype(F32)
        qi_hm = jnp.concatenate([qi[:, h * D_I:(h + 1) * D_I] for h in range(H_I)],
                                axis=0).astype(BF16)
        w = wi_ref[0] * (H_I ** -0.5 * D_I ** -0.5)

        def scores(ki):
            s = _dot_nt(qi_hm, ki.astype(BF16))
            acc = None
            for h in range(H_I):
                term = w[:, h:h + 1] * jnp.maximum(s[h * ts:(h + 1) * ts, :], 0.0)
                acc = term if acc is None else acc + term
            return _sortable(acc)

        for c in range(nc):
            key_scr[c] = scores(kic_ref[0, c * kc_len:(c + 1) * kc_len, :])
        keyn_scr[...] = jnp.where(lane < n_new, scores(kin_ref[0]), INT_MIN)

        def count(pred_key=None, pred_tie=None):
            cnt = jnp.zeros((ts, L), I32)
            for c in range(nc):
                for i in range(kc_len // L):
                    if pred_key is not None:
                        hit = pred_key(key_scr[c, :, i * L:(i + 1) * L])
                    else:
                        hit = pred_tie(tix_scr[c, :, i * L:(i + 1) * L])
                    cnt = cnt + jnp.where(hit, 1, 0)
            hit = pred_key(keyn_scr[...]) if pred_key is not None else pred_tie(tixn_scr[...])
            return _lane_total(cnt + jnp.where(hit, 1, 0))

        def write_ties(thr):
            for c in range(nc):
                for i in range(kc_len // L):
                    idx = c * kc_len + i * L + lane
                    tix_scr[c, :, i * L:(i + 1) * L] = jnp.where(
                        key_scr[c, :, i * L:(i + 1) * L] == thr, idx, TIE_NONE)
            tixn_scr[...] = jnp.where(keyn_scr[...] == thr, past + lane, TIE_NONE)

        thr, tie_end = _topk_threshold(
            lambda x: count(pred_key=lambda k: k >= x),
            lambda x: count(pred_key=lambda k: k > x),
            lambda x: count(pred_tie=lambda t: t < x),
            write_ties, ts, ksel, idx_bits)
        sel_scr[0] = thr
        sel_scr[1] = tie_end
        qbd_scr[...] = _pack_heads(q_ref[0], n_heads)
        m_scr[...] = jnp.full(m_scr.shape, -1e30, F32)
        l_scr[...] = jnp.zeros(l_scr.shape, F32)
        acc_scr[...] = jnp.zeros(acc_scr.shape, F32)

    thr = sel_scr[0]
    tie_end = sel_scr[1]

    def attend(keys, ties, k16, v16, bias):
        n = keys.shape[1]
        negs = []
        for i in range(n // L):
            sel = jnp.logical_or(keys[:, i * L:(i + 1) * L] > thr,
                                 ties[:, i * L:(i + 1) * L] < tie_end)
            negs.append(jnp.where(sel, 0.0, -jnp.inf))
        neg = jnp.concatenate(negs, axis=1) if len(negs) > 1 else negs[0]
        neg = jnp.concatenate([neg] * n_heads, axis=0)
        s = _dot_nt(qbd_scr[...], k16) + bias + neg
        m_old = m_scr[...]
        m_new = jnp.maximum(m_old, jnp.max(s, axis=1, keepdims=True))
        alpha = jnp.exp(m_old - m_new)
        p = jnp.exp(s - m_new)
        l_scr[...] = alpha * l_scr[...] + jnp.sum(p, axis=1, keepdims=True)
        m_scr[...] = m_new
        acc_scr[...] = acc_scr[...] * alpha + _dot(p.astype(BF16), v16)

    attend(key_scr[step], tix_scr[step], kc_ref[0].astype(BF16), vc_ref[0].astype(BF16),
           bias_ref[...])

    @pl.when(step == nc - 1)
    def _():
        attend(keyn_scr[...], tixn_scr[...], kn_ref[0], vn_ref[0], biasn_ref[...])
        o_ref[0] = _unpack_heads(acc_scr[...], l_scr[...], n_heads).astype(BF16)


def _dsa_sample(q, qi, wi, kn16, vn16, kin, cache_k, cache_v, cache_ki, bias, n_new, ksel):
    B, ts, aw = q.shape
    n_heads = aw // DH
    P = cache_k.shape[1]
    npad = kn16.shape[1]
    kc_len = min(P, 1024)
    nc = P // kc_len
    L = V7X_LANES
    per_b = lambda a: pl.BlockSpec((1,) + a.shape[1:], lambda b, c: (b, 0, 0))
    chunk = lambda n: pl.BlockSpec((1, kc_len, n), lambda b, c: (b, c, 0))
    rows = n_heads * ts
    return pl.pallas_call(
        functools.partial(_dsa_sample_body, ts=ts, kc_len=kc_len, n_new=n_new, ksel=ksel,
                          idx_bits=(P + npad).bit_length()),
        grid=(B, nc),
        in_specs=[per_b(q), per_b(qi), per_b(wi), per_b(kn16), per_b(vn16), per_b(kin),
                  chunk(aw), chunk(aw), per_b(cache_ki),
                  pl.BlockSpec((rows, kc_len), lambda b, c: (0, c)),
                  pl.BlockSpec((rows, npad), lambda b, c: (0, P // npad))],
        out_specs=pl.BlockSpec((1, ts, aw), lambda b, c: (b, 0, 0)),
        out_shape=jax.ShapeDtypeStruct((B, ts, aw), BF16),
        scratch_shapes=[pltpu.VMEM((nc, ts, kc_len), I32), pltpu.VMEM((nc, ts, kc_len), I32),
                        pltpu.VMEM((ts, npad), I32), pltpu.VMEM((ts, npad), I32),
                        pltpu.VMEM((2, ts, L), I32), pltpu.VMEM((rows, aw), BF16),
                        pltpu.VMEM((rows, 1), F32), pltpu.VMEM((rows, 1), F32),
                        pltpu.VMEM((rows, aw), F32)],
        compiler_params=_cparams("arbitrary", "arbitrary"),
        name="dsa_sample",
    )(q, qi, wi, kn16, vn16, kin, cache_k, cache_v, cache_ki, bias, bias)


def _band_prompt_body(q_ref, k_ref, vt_ref, bias_ref, o_ref, qz_scr, s_scr, p_scr, ot_scr,
                      *, tq, n_back):
    L = V7X_LANES
    n_heads = q_ref.shape[-1] // DH
    n_slots = s_scr.shape[0]
    n_p = p_scr.shape[0]
    nb = n_back + 1
    rc = 32
    n_rc = tq // rc
    qt = pl.program_id(1)
    lane = lax.broadcasted_iota(I32, (tq, L), 1)
    n_tiles = bias_ref.shape[0]
    blocks, starts, exists = [], [], []
    for j in range(nb):
        kb = qt - n_back + j
        blocks.append(jnp.maximum(kb, 0))
        starts.append(pl.multiple_of(jnp.maximum(kb, 0) * tq, tq))
        exists.append(kb >= 0)
    q = q_ref[0].astype(F32)
    for h in range(n_heads):
        qp = q[:, (h // 2) * L:(h // 2 + 1) * L]
        keep = (lane < DH) if h % 2 == 0 else (lane >= DH)
        qz_scr[h] = jnp.where(keep, qp, 0.0).astype(BF16)

    def scores(h):
        mx = None
        for j in range(nb):
            kp = k_ref[0, pl.ds(starts[j], tq), (h // 2) * L:(h // 2 + 1) * L]
            tile = jnp.where(exists[j], h * nb + j, n_tiles - 1)
            x = _dot_nt(kp, qz_scr[h]).astype(BF16) + bias_ref[tile]
            s_scr[h % n_slots, j] = x
            xm = jnp.max(x.reshape(tq // BF16_ROWS, BF16_ROWS, tq), axis=0)
            mx = xm if mx is None else jnp.maximum(mx, xm)
        return mx

    ones = jnp.ones((BF16_ROWS, tq), BF16)
    maxes = [scores(h) for h in range(n_slots - 1)]
    for h in range(n_heads):
        if h + n_slots - 1 < n_heads:
            maxes.append(scores(h + n_slots - 1))
        slot = h % n_slots
        m = jnp.max(maxes[h].astype(F32), axis=0, keepdims=True)
        m_b = jnp.broadcast_to(m.astype(BF16), (rc, tq))
        out = None
        for j in range(nb):
            for r in range(n_rc):
                rows = slice(r * rc, (r + 1) * rc)
                p_scr[h % n_p, j, rows, :] = jnp.exp2(s_scr[slot, j, rows, :] - m_b)
            vt = jnp.concatenate([vt_ref[0, blocks[j], h * DH:(h + 1) * DH, :], ones], axis=0)
            oj = _dot(vt, p_scr[h % n_p, j])
            out = oj if out is None else out + oj
        ot_scr[h * DH:(h + 1) * DH, :] = out[:DH, :] / out[DH:DH + 1, :]
    for pair in range(n_heads // 2):
        o_ref[0, :, pair * L:(pair + 1) * L] = ot_scr[pair * L:(pair + 1) * L, :].T.astype(BF16)


def _band_prompt(q, kb16, vt16, bias_t):
    B, T, cw = q.shape
    n_heads = cw // DH
    tq = bias_t.shape[-1]
    nq = T // tq
    n_back = (bias_t.shape[0] - 1) // n_heads - 1
    assert vt16.shape == (B, nq, cw, tq)
    row = pl.BlockSpec((1, tq, cw), lambda b, t: (b, t, 0))
    return pl.pallas_call(
        functools.partial(_band_prompt_body, tq=tq, n_back=n_back),
        grid=(B, nq),
        in_specs=[row, pl.BlockSpec((1, T, cw), lambda b, t: (b, 0, 0)),
                  pl.BlockSpec((1, nq, cw, tq), lambda b, t: (b, 0, 0, 0)),
                  pl.BlockSpec(bias_t.shape, lambda b, t: (0, 0, 0),
                               pipeline_mode=pl.Buffered(1))],
        out_specs=row,
        out_shape=jax.ShapeDtypeStruct((B, T, cw), BF16),
        scratch_shapes=[pltpu.VMEM((n_heads, tq, V7X_LANES), BF16),
                        pltpu.VMEM((6, n_back + 1, tq, tq), BF16),
                        pltpu.VMEM((6, n_back + 1, tq, tq), BF16),
                        pltpu.VMEM((cw, tq), F32)],
        compiler_params=_cparams("arbitrary", "arbitrary"),
        name="band_prompt",
    )(q, kb16, vt16, bias_t)


def _band_sample_body(q_ref, kn_ref, vn_ref, kc_ref, vc_ref, bias_ref, biasn_ref, o_ref):
    n_heads = q_ref.shape[-1] // DH
    qbd = _pack_heads(q_ref[0], n_heads)
    s_c = _dot_nt(qbd, kc_ref[0].astype(BF16)) + bias_ref[...]
    s_n = _dot_nt(qbd, kn_ref[0]) + biasn_ref[...]
    m = jnp.maximum(jnp.max(s_c, axis=1, keepdims=True), jnp.max(s_n, axis=1, keepdims=True))
    p_c = jnp.exp(s_c - m)
    p_n = jnp.exp(s_n - m)
    den = jnp.sum(p_c, axis=1, keepdims=True) + jnp.sum(p_n, axis=1, keepdims=True)
    acc = _dot(p_c.astype(BF16), vc_ref[0].astype(BF16)) + _dot(p_n.astype(BF16), vn_ref[0])
    o_ref[0] = _unpack_heads(acc, den, n_heads).astype(BF16)


def _band_sample(q, kn16, vn16, cache_k, cache_v, bias):
    B, ts, cw = q.shape
    n_heads = cw // DH
    cb = cache_k.shape[1]
    npad = kn16.shape[1]
    rows = n_heads * ts
    per_b = lambda a: pl.BlockSpec((1,) + a.shape[1:], lambda b: (b, 0, 0))
    return pl.pallas_call(
        _band_sample_body,
        grid=(B,),
        in_specs=[per_b(q), per_b(kn16), per_b(vn16), per_b(cache_k), per_b(cache_v),
                  pl.BlockSpec((rows, cb), lambda b: (0, 0)),
                  pl.BlockSpec((rows, npad), lambda b: (0, cb // npad))],
        out_specs=per_b(q),
        out_shape=jax.ShapeDtypeStruct((B, ts, cw), BF16),
        compiler_params=_cparams("arbitrary"),
        name="band_sample",
    )(q, kn16, vn16, cache_k, cache_v, bias, bias)


def _pad_rows(a, n):
    return jnp.pad(a, ((0, 0), (0, n - a.shape[1]), (0, 0)))


def kernel(x_prompt, x_sample, c_prompt, c_sample, cache_a_k, cache_a_v, cache_a_kidx, state_b,
           cache_c_k, cache_c_v, w_ada, b_ada, norm_mix, norm_ffn, norm_final, w_in_even,
           w_out_even, t5_table, lb_logits, norm_b_out, w_in_odd, w_out_odd, rel_table_c,
           w_ffn_gate, w_ffn_up, w_ffn_down):
    Bp, T, D = x_prompt.shape
    Bs, Ts, _ = x_sample.shape
    P = cache_a_k.shape[2]
    c_buf = cache_c_k.shape[2]
    depth = w_ada.shape[0]
    L = V7X_LANES
    n_heads_a = cache_a_k.shape[3]
    aw = n_heads_a * DH
    n_heads_c = cache_c_k.shape[3]
    bw = w_out_even.shape[1] - aw
    iw = H_I * D_I
    ksel_p = min(K_TOP, T // 4)
    ksel_s = min(K_TOP, (P + Ts) // 4)
    c_keep = min(C_BACK * CHUNK, T)
    tq = min(T, 256)
    npad = L
    assert Ts <= npad and P % npad == 0 and c_buf % npad == 0 and T % tq == 0

    mod = _adaln_all(jnp.concatenate([c_prompt, c_sample], axis=0), w_ada, b_ada)

    def mods(l):
        parts = jnp.split(mod[l], 6, axis=-1)
        return ([p[:Bp].reshape(Bp, 1, D) for p in parts],
                [p[Bp:].reshape(Bs, 1, D) for p in parts])

    nq = T // tq
    t5_prompt = _bias_tiles(
        t5_table, groups=nq * n_heads_a, rows=tq, cols=tq, idx_fn=_t5_bucket,
        group_fn=lambda g: (g % n_heads_a, -(g // n_heads_a) * tq), keys_on_rows=True,
    ).reshape(nq, n_heads_a, tq, tq)
    t5_sample = _bias_tiles(
        t5_table, groups=n_heads_a, rows=Ts, cols=P + npad, idx_fn=_t5_bucket,
        group_fn=lambda g: (g, -P),
    ).reshape(n_heads_a * Ts, P + npad)

    xp, xs = x_prompt, x_sample
    outs_p = {k: [] for k in ("ak", "av", "ai", "bs", "ck", "cv")}
    outs_s = {k: [] for k in ("ak", "av", "ai", "bs", "ck", "cv")}
    for l in range(depth):
        (sh1p, sc1p, g1p, sh2p, sc2p, g2p), (sh1s, sc1s, g1s, sh2s, sc2s, g2s) = mods(l)
        gm = norm_mix[l].reshape(1, D)
        gn = norm_ffn[l].reshape(1, D)
        wg = w_ffn_gate[l].astype(BF16)
        wu = w_ffn_up[l].astype(BF16)
        wd = w_ffn_down[l].astype(BF16)
        gf = norm_final.reshape(1, D) if l == depth - 1 else None
        if l % 2 == 0:
            e = l // 2
            w_in = w_in_even[e]
            n_a = 3 * aw + iw + D_I + H_I
            wa = jnp.pad(w_in[:, :n_a], ((0, 0), (0, -n_a % L))).astype(BF16)
            wb = w_in[:, n_a:].astype(BF16)
            w_out = w_out_even[e].astype(BF16)
            gb_norm = norm_b_out[e].reshape(1, bw)

            q, k, v, k16, vt16, qi, ki, ki2, wi, zb = _in_even(xp, sc1p, sh1p, gm, wa, wb, aw,
                                                              keys_on_rows=True)
            oa = _dsa_prompt_t(q, qi, wi, k16, vt16, ki2, t5_prompt, ksel_p)
            ob, st = _hgrn(zb, lb_logits, gb_norm, None, e)
            outs_p["ak"].append(k.reshape(Bp, T, n_heads_a, DH))
            outs_p["av"].append(v.reshape(Bp, T, n_heads_a, DH))
            outs_p["ai"].append(ki)
            outs_p["bs"].append(st)
            xp = _mix_ffn(xp, g1p, sc2p, sh2p, g2p, gn, [oa, ob], [w_out[:aw], w_out[aw:]],
                          wg, wu, wd, gf)

            q, k, v, k16, v16, qi, ki, _, wi, zb = _in_even(xs, sc1s, sh1s, gm, wa, wb, aw,
                                                            keys_on_rows=False)
            oa = _dsa_sample(q, qi, wi, _pad_rows(k16, npad), _pad_rows(v16, npad),
                             _pad_rows(ki, npad), cache_a_k[e].reshape(Bs, P, aw),
                             cache_a_v[e].reshape(Bs, P, aw), cache_a_kidx[e], t5_sample,
                             Ts, ksel_s)
            ob, st = _hgrn(zb, lb_logits, gb_norm, state_b[e], e)
            outs_s["ak"].append(k.reshape(Bs, Ts, n_heads_a, DH))
            outs_s["av"].append(v.reshape(Bs, Ts, n_heads_a, DH))
            outs_s["ai"].append(ki)
            outs_s["bs"].append(st)
            xs = _mix_ffn(xs, g1s, sc2s, sh2s, g2s, gn, [oa, ob], [w_out[:aw], w_out[aw:]],
                          wg, wu, wd, gf)
        else:
            o = l // 2
            cw = n_heads_c * DH
            w_in = w_in_odd[o].astype(BF16)
            w_out = w_out_odd[o].astype(BF16)
            n_back = -(-(C_BACK * CHUNK) // tq)
            shift = CHUNK.bit_length() - 1

            n_band = n_heads_c * (n_back + 1)

            def band_mask(g, t, s, n_back=n_back, n_band=n_band):
                j = g % (n_back + 1)
                kc = (s + (j - n_back) * tq) >> shift
                tc = t >> shift
                return (kc <= tc) & (kc >= tc - C_BACK) & (g < n_band)

            band_p = _bias_tiles(
                rel_table_c[o], groups=n_band + 1, rows=tq, cols=tq, idx_fn=_clip_index,
                group_fn=lambda g, n_back=n_back: (jnp.minimum(g // (n_back + 1), n_heads_c - 1),
                                                   (g % (n_back + 1) - n_back) * tq),
                mask_fn=band_mask, keys_on_rows=True)
            band_s = _bias_tiles(
                rel_table_c[o], groups=n_heads_c, rows=Ts, cols=c_buf + npad, idx_fn=_clip_index,
                group_fn=lambda g: (g, -c_buf),
                mask_fn=lambda g, t, s: s < c_buf + Ts,
            ).reshape(n_heads_c * Ts, c_buf + npad)

            q, k, v, k16, vt16 = _in_odd(xp, sc1p, sh1p, gm, w_in, keys_on_rows=True)
            op = _band_prompt(q, k16, vt16, band_p)
            outs_p["ck"].append(k[:, T - c_keep:].reshape(Bp, c_keep, n_heads_c, DH))
            outs_p["cv"].append(v[:, T - c_keep:].reshape(Bp, c_keep, n_heads_c, DH))
            xp = _mix_ffn(xp, g1p, sc2p, sh2p, g2p, gn, [op], [w_out], wg, wu, wd, gf)

            q, k, v, k16, v16 = _in_odd(xs, sc1s, sh1s, gm, w_in, keys_on_rows=False)
            osm = _band_sample(q, _pad_rows(k16, npad), _pad_rows(v16, npad),
                               cache_c_k[o].reshape(Bs, c_buf, cw),
                               cache_c_v[o].reshape(Bs, c_buf, cw), band_s)
            outs_s["ck"].append(k.reshape(Bs, Ts, n_heads_c, DH))
            outs_s["cv"].append(v.reshape(Bs, Ts, n_heads_c, DH))
            xs = _mix_ffn(xs, g1s, sc2s, sh2s, g2s, gn, [osm], [w_out], wg, wu, wd, gf)

    st = lambda xs_: jnp.stack(xs_)
    return (xp, xs, st(outs_p["ak"]), st(outs_p["av"]), st(outs_p["ai"]), st(outs_p["bs"]),
            st(outs_p["ck"]), st(outs_p["cv"]), st(outs_s["ak"]), st(outs_s["av"]),
            st(outs_s["ai"]), st(outs_s["bs"]), st(outs_s["ck"]), st(outs_s["cv"]))
```

```python
import functools
import math

import jax
import jax.numpy as jnp
from jax import lax
from jax.experimental import pallas as pl
from jax.experimental.pallas import tpu as pltpu

F32, BF16, I32 = jnp.float32, jnp.bfloat16, jnp.int32

CHUNK = 64
EPS = 1e-6
DH = 64
H_I = 8
D_I = 64
K_TOP = 256
N_BUCKETS = 32
T5_MAX_DIST = 1024
DK_B = 128
DV_B = 128
C_BACK = 8
REL_CLIP = 128

V7X_LANES = 128
BF16_ROWS = 16
V7X_VMEM_LIMIT = 56 * 2**20

INT_MIN = -(2**31)
TIE_NONE = 2**30
HIGHEST = lax.Precision.HIGHEST


def _cparams(*sem):
    return pltpu.CompilerParams(dimension_semantics=sem, vmem_limit_bytes=V7X_VMEM_LIMIT)


def _dot(a, b):
    return jnp.dot(a, b, preferred_element_type=F32)


def _dot_nt(a, b):
    return lax.dot_general(a, b, (((1,), (1,)), ((), ())), preferred_element_type=F32)


def _silu(x):
    return x * jax.nn.sigmoid(x)


def _norm_mod(x, g, scale, shift):
    ms = jnp.mean(x * x, axis=-1, keepdims=True)
    return (x * lax.rsqrt(ms + EPS) * g) * (1.0 + scale) + shift


def _sortable(x):
    b = pltpu.bitcast(x + 0.0, I32)
    return b ^ ((b >> 31) & I32(0x7FFFFFFF))


LOG2E = math.log2(math.e)


def _q_scale(base2):
    return DH ** -0.5 * (LOG2E if base2 else 1.0)


def _row_tiles(B, T, rows=256):
    tt = min(T, rows)
    bb = max(1, min(B, rows // tt))
    assert T % tt == 0 and B % bb == 0
    return bb, tt


def _ada_body(c_ref, w_ref, b_ref, o_ref):
    s = _silu(c_ref[...])
    o_ref[0] = jnp.dot(s, w_ref[0], precision=HIGHEST, preferred_element_type=F32) + b_ref[0]


def _adaln_all(c_all, w_ada, b_ada):
    depth, D, E = w_ada.shape
    R = c_all.shape[0]
    tn = 1024
    return pl.pallas_call(
        _ada_body,
        grid=(depth, E // tn),
        in_specs=[pl.BlockSpec((R, D), lambda l, n: (0, 0)),
                  pl.BlockSpec((1, D, tn), lambda l, n: (l, 0, n)),
                  pl.BlockSpec((1, 1, tn), lambda l, n: (l, 0, n))],
        out_specs=pl.BlockSpec((1, R, tn), lambda l, n: (l, 0, n)),
        out_shape=jax.ShapeDtypeStruct((depth, R, E), F32),
        compiler_params=_cparams("arbitrary", "arbitrary"),
        name="adaln",
    )(c_all, w_ada, b_ada.reshape(depth, 1, E))


def _t5_bucket(rel):
    nb = N_BUCKETS // 2
    max_exact = nb // 2
    base = jnp.where(rel > 0, nb, 0)
    n = jnp.abs(rel)
    large = max_exact + (jnp.log(jnp.maximum(n, 1).astype(F32) / max_exact)
                         / math.log(T5_MAX_DIST / max_exact) * (nb - max_exact)).astype(I32)
    large = jnp.minimum(large, nb - 1)
    return base + jnp.where(n < max_exact, n, large)


def _clip_index(rel):
    return jnp.clip(-rel, -REL_CLIP, REL_CLIP) + REL_CLIP


def _bias_body(tab_ref, o_ref, *, rows, cols, width, n_tab, idx_fn, group_fn, mask_fn,
               keys_on_rows):
    g = pl.program_id(0)
    head, off = group_fn(g)
    i = lax.broadcasted_iota(I32, (1, width), 1)
    diff = jnp.where(i < width - rows, i, i - width)
    idx = idx_fn(off - diff if keys_on_rows else off + diff)
    val = jnp.zeros((1, width), F32)
    for b in range(n_tab):
        val = jnp.where(idx == b, tab_ref[head, b], val)
    x = jnp.broadcast_to(val, (rows, width))
    y = pltpu.roll(x, 0, 1, stride=1, stride_axis=0)
    tile = y[:, :cols]
    if keys_on_rows:
        tile = tile * LOG2E
    if mask_fn is not None:
        t = lax.broadcasted_iota(I32, (rows, cols), 1 if keys_on_rows else 0)
        s = lax.broadcasted_iota(I32, (rows, cols), 0 if keys_on_rows else 1)
        tile = jnp.where(mask_fn(g, t, s), tile, -jnp.inf)
    o_ref[0] = tile.astype(o_ref.dtype)


def _bias_tiles(table, *, groups, rows, cols, idx_fn, group_fn, mask_fn=None,
                keys_on_rows=False):
    n_tab = table.shape[0]
    width = -(-(rows + cols) // V7X_LANES) * V7X_LANES
    body = functools.partial(_bias_body, rows=rows, cols=cols, width=width, n_tab=n_tab,
                             idx_fn=idx_fn, group_fn=group_fn, mask_fn=mask_fn,
                             keys_on_rows=keys_on_rows)
    return pl.pallas_call(
        body,
        grid=(groups,),
        in_specs=[pl.BlockSpec(memory_space=pltpu.SMEM)],
        out_specs=pl.BlockSpec((1, rows, cols), lambda g: (g, 0, 0)),
        out_shape=jax.ShapeDtypeStruct((groups, rows, cols), BF16 if keys_on_rows else F32),
        compiler_params=_cparams("arbitrary"),
        name="bias_tiles",
    )(table.T)


def _in_even_body(x_ref, sc_ref, sh_ref, g_ref, wa_ref, wb_ref,
                  q_ref, k_ref, v_ref, kb_ref, vb_ref, qi_ref, ki_ref, ki2_ref, wi_ref, zb_ref,
                  *, aw, keys_on_rows):
    bb, tt, D = x_ref.shape
    M = bb * tt
    h = _norm_mod(x_ref[...], g_ref[...].reshape(1, 1, D), sc_ref[...], sh_ref[...]).reshape(M, D)
    h = h.astype(BF16)
    za = _dot(h, wa_ref[...])
    q_ref[...] = (za[:, :aw] * _q_scale(keys_on_rows)).astype(BF16).reshape(bb, tt, aw)
    k = za[:, aw:2 * aw]
    v = za[:, 2 * aw:3 * aw]
    k_ref[...] = k.reshape(bb, tt, aw)
    v_ref[...] = v.reshape(bb, tt, aw)
    kb_ref[...] = k.astype(BF16).reshape(bb, tt, aw)
    if keys_on_rows:
        vb_ref[0, 0] = v.T.astype(BF16)
    else:
        vb_ref[...] = v.astype(BF16).reshape(bb, tt, aw)
    iw = H_I * D_I
    zi = za[:, 3 * aw:]
    ki = zi[:, iw:iw + D_I]
    qi_ref[...] = zi[:, :iw].astype(BF16).reshape(bb, tt, iw)
    ki_ref[...] = ki.reshape(bb, tt, D_I)
    ki2_ref[...] = jnp.concatenate([ki, ki], axis=1).astype(BF16).reshape(bb, tt, 2 * D_I)
    wi_ref[...] = zi[:, iw + D_I:iw + D_I + H_I].reshape(bb, tt, H_I)
    zb_ref[...] = _dot(h, wb_ref[...]).reshape(bb, tt, zb_ref.shape[-1])


def _in_even(x, scale, shift, g, wa, wb, aw, keys_on_rows):
    B, T, D = x.shape
    bw = wb.shape[1]
    iw = H_I * D_I
    bb, tt = _row_tiles(B, T)
    assert not keys_on_rows or bb == 1
    row = lambda n: pl.BlockSpec((bb, tt, n), lambda b, t: (b, t, 0))
    mod = pl.BlockSpec((bb, 1, D), lambda b, t: (b, 0, 0))
    full = lambda a: pl.BlockSpec(a.shape, lambda b, t: (0,) * a.ndim)
    outs = [(aw, BF16), (aw, F32), (aw, F32), (aw, BF16), (aw, BF16), (iw, BF16), (D_I, F32),
            (2 * D_I, BF16), (H_I, F32), (bw, F32)]
    out_specs = [row(n) for n, _ in outs]
    out_shape = [jax.ShapeDtypeStruct((B, T, n), dt) for n, dt in outs]
    if keys_on_rows:
        out_specs[4] = pl.BlockSpec((1, 1, aw, tt), lambda b, t: (b, t, 0, 0))
        out_shape[4] = jax.ShapeDtypeStruct((B, T // tt, aw, tt), BF16)
    return pl.pallas_call(
        functools.partial(_in_even_body, aw=aw, keys_on_rows=keys_on_rows),
        grid=(B // bb, T // tt),
        in_specs=[row(D), mod, mod, full(g), full(wa), full(wb)],
        out_specs=out_specs,
        out_shape=out_shape,
        compiler_params=_cparams("arbitrary", "arbitrary"),
        name="in_even",
    )(x, scale, shift, g, wa, wb)


def _in_odd_body(x_ref, sc_ref, sh_ref, g_ref, w_ref, q_ref, k_ref, v_ref, kb_ref, vb_ref,
                 *, keys_on_rows):
    bb, tt, D = x_ref.shape
    M = bb * tt
    cw = q_ref.shape[-1]
    h = _norm_mod(x_ref[...], g_ref[...].reshape(1, 1, D), sc_ref[...], sh_ref[...]).reshape(M, D)
    z = _dot(h.astype(BF16), w_ref[...])
    q_ref[...] = (z[:, :cw] * _q_scale(keys_on_rows)).astype(BF16).reshape(bb, tt, cw)
    k = z[:, cw:2 * cw]
    v = z[:, 2 * cw:]
    k_ref[...] = k.reshape(bb, tt, cw)
    v_ref[...] = v.reshape(bb, tt, cw)
    kb_ref[...] = k.astype(BF16).reshape(bb, tt, cw)
    if keys_on_rows:
        vb_ref[0, 0] = v.T.astype(BF16)
    else:
        vb_ref[...] = v.astype(BF16).reshape(bb, tt, cw)


def _in_odd(x, scale, shift, g, w, keys_on_rows):
    B, T, D = x.shape
    cw = w.shape[1] // 3
    bb, tt = _row_tiles(B, T)
    assert not keys_on_rows or bb == 1
    row = lambda n: pl.BlockSpec((bb, tt, n), lambda b, t: (b, t, 0))
    mod = pl.BlockSpec((bb, 1, D), lambda b, t: (b, 0, 0))
    full = lambda a: pl.BlockSpec(a.shape, lambda b, t: (0,) * a.ndim)
    dts = [BF16, F32, F32, BF16, BF16]
    out_specs = [row(cw) for _ in dts]
    out_shape = [jax.ShapeDtypeStruct((B, T, cw), dt) for dt in dts]
    if keys_on_rows:
        out_specs[4] = pl.BlockSpec((1, 1, cw, tt), lambda b, t: (b, t, 0, 0))
        out_shape[4] = jax.ShapeDtypeStruct((B, T // tt, cw, tt), BF16)
    return pl.pallas_call(
        functools.partial(_in_odd_body, keys_on_rows=keys_on_rows),
        grid=(B // bb, T // tt),
        in_specs=[row(D), mod, mod, full(g), full(w)],
        out_specs=out_specs,
        out_shape=out_shape,
        compiler_params=_cparams("arbitrary", "arbitrary"),
        name="in_odd",
    )(x, scale, shift, g, w)


def _mix_ffn_body(*refs, n_mix, final):
    x_ref, g1_ref, sc2_ref, sh2_ref, g2_ref, gn_ref = refs[:6]
    mix_refs = refs[6:6 + n_mix]
    wmix_refs = refs[6 + n_mix:6 + 2 * n_mix]
    wg_ref, wu_ref, wd_ref = refs[6 + 2 * n_mix:9 + 2 * n_mix]
    gf_ref = refs[9 + 2 * n_mix] if final else None
    o_ref = refs[-1]
    bb, tt, D = x_ref.shape
    M = bb * tt
    m = None
    for mr, wr in zip(mix_refs, wmix_refs):
        part = _dot(mr[...].reshape(M, mr.shape[-1]), wr[...])
        m = part if m is None else m + part
    x1 = x_ref[...] + g1_ref[...] * m.reshape(bb, tt, D)
    h2 = _norm_mod(x1, gn_ref[...].reshape(1, 1, D), sc2_ref[...], sh2_ref[...])
    h2 = h2.reshape(M, D).astype(BF16)
    a = _dot(h2, wg_ref[...])
    b = _dot(h2, wu_ref[...])
    u = (_silu(a) * b).astype(BF16)
    f = _dot(u, wd_ref[...])
    x2 = x1 + g2_ref[...] * f.reshape(bb, tt, D)
    if final:
        ms = jnp.mean(x2 * x2, axis=-1, keepdims=True)
        x2 = x2 * lax.rsqrt(ms + EPS) * gf_ref[...].reshape(1, 1, D)
    o_ref[...] = x2


def _mix_ffn(x, g1, sc2, sh2, g2, gn, mixes, wmixes, wg, wu, wd, gf=None):
    B, T, D = x.shape
    bb, tt = _row_tiles(B, T, rows=512)
    row = lambda n: pl.BlockSpec((bb, tt, n), lambda b, t: (b, t, 0))
    mod = pl.BlockSpec((bb, 1, D), lambda b, t: (b, 0, 0))
    full = lambda a: pl.BlockSpec(a.shape, lambda b, t: (0,) * a.ndim,
                                  pipeline_mode=pl.Buffered(1))
    final = gf is not None
    args = [x, g1, sc2, sh2, g2, gn, *mixes, *wmixes, wg, wu, wd] + ([gf] if final else [])
    specs = ([row(D), mod, mod, mod, mod, full(gn)] + [row(m.shape[-1]) for m in mixes]
             + [full(w) for w in wmixes] + [full(wg), full(wu), full(wd)]
             + ([full(gf)] if final else []))
    return pl.pallas_call(
        functools.partial(_mix_ffn_body, n_mix=len(mixes), final=final),
        grid=(B // bb, T // tt),
        in_specs=specs,
        out_specs=row(D),
        out_shape=jax.ShapeDtypeStruct((B, T, D), F32),
        compiler_params=_cparams("arbitrary", "arbitrary"),
        name="mix_ffn",
    )(*args)


def _hgrn_body(*refs, C, n_sub, e, has_s0):
    if has_s0:
        z_ref, lbl_ref, gn_ref, s0_ref, ob_ref, sout_ref, st_scr = refs
    else:
        z_ref, lbl_ref, gn_ref, ob_ref, sout_ref, st_scr = refs
    step = pl.program_id(1)
    n_heads = st_scr.shape[0]
    kw = n_heads * DK_B

    @pl.when(step == 0)
    def _():
        for h in range(n_heads):
            st_scr[h] = s0_ref[0, h].T if has_s0 else jnp.zeros((DV_B, DK_B), F32)

    ll = lbl_ref[...]
    ex = jnp.exp(ll - jnp.max(ll, axis=0, keepdims=True))
    sm = ex / jnp.sum(ex, axis=0, keepdims=True)
    lb = jnp.sum(sm[:e + 1], axis=0, keepdims=True)
    r = lax.broadcasted_iota(I32, (C, C), 0)
    c = lax.broadcasted_iota(I32, (C, C), 1)
    causal = r >= c
    tri = jnp.where(causal, 1.0, 0.0)
    gn = gn_ref[...]
    heads = [(slice(h * DK_B, (h + 1) * DK_B), slice(h * DV_B, (h + 1) * DV_B))
             for h in range(n_heads)]
    intra, q_sts, decs, updates, gates = [], [], [], [], []
    for sub in range(n_sub):
        z = z_ref[0, sub * C:(sub + 1) * C, :]
        qh = _silu(z[:, :kw])
        f = lb + (1.0 - lb) * jax.nn.sigmoid(z[:, kw:2 * kw])
        kh = 1.0 - f
        vh = z[:, 2 * kw:3 * kw]
        gates.append(_silu(z[:, 3 * kw:]))
        G = jnp.dot(tri, jnp.log(f), precision=HIGHEST, preferred_element_type=F32)
        g_end = G[C - 1:C, :]
        mid = 0.5 * g_end
        e_mid = jnp.exp(mid)
        q_mid = qh * jnp.exp(G - mid)
        k_mid = kh * jnp.exp(mid - G)
        q_in = q_mid.astype(BF16)
        k_in = k_mid.astype(BF16)
        k_st = (k_mid * e_mid).astype(BF16)
        q_sts.append((q_mid * e_mid).astype(BF16))
        decs.append(e_mid * e_mid)
        o_sub, u_sub = [], []
        for sk, sv in heads:
            att = jnp.where(causal, _dot_nt(q_in[:, sk], k_in[:, sk]), 0.0)
            v_h = vh[:, sv]
            o_sub.append(_dot(att.astype(BF16), v_h.astype(BF16)))
            u_sub.append(_dot(v_h.T.astype(BF16), k_st[:, sk]))
        intra.append(o_sub)
        updates.append(u_sub)
    states = [st_scr[h] for h in range(n_heads)]
    for sub in range(n_sub):
        outs = []
        for h, (sk, sv) in enumerate(heads):
            o = intra[sub][h] + _dot_nt(q_sts[sub][:, sk], states[h].astype(BF16))
            states[h] = states[h] * decs[sub][:, sk] + updates[sub][h]
            outs.append(o * lax.rsqrt(jnp.mean(o * o, axis=-1, keepdims=True) + EPS))
        ob = jnp.concatenate(outs, axis=1) * gn * gates[sub]
        ob_ref[0, sub * C:(sub + 1) * C, :] = ob.astype(BF16)
    for h in range(n_heads):
        st_scr[h] = states[h]

    @pl.when(step == pl.num_programs(1) - 1)
    def _():
        for h in range(n_heads):
            sout_ref[0, h] = st_scr[h].T


def _hgrn(zb, lb_logits, g_norm, s0, e):
    B, T, zw = zb.shape
    kw = zw // 4
    n_heads = kw // DK_B
    C = min(CHUNK, T)
    ct = min(T, 8 * C)
    has_s0 = s0 is not None
    full = lambda a: pl.BlockSpec(a.shape, lambda b, t: (0,) * a.ndim)
    state = pl.BlockSpec((1, n_heads, DK_B, DV_B), lambda b, t: (b, 0, 0, 0))
    args = [zb, lb_logits, g_norm] + ([s0] if has_s0 else [])
    specs = ([pl.BlockSpec((1, ct, zw), lambda b, t: (b, t, 0)), full(lb_logits), full(g_norm)]
             + ([state] if has_s0 else []))
    return pl.pallas_call(
        functools.partial(_hgrn_body, C=C, n_sub=ct // C, e=e, has_s0=has_s0),
        grid=(B, T // ct),
        in_specs=specs,
        out_specs=[pl.BlockSpec((1, ct, kw), lambda b, t: (b, t, 0)), state],
        out_shape=[jax.ShapeDtypeStruct((B, T, kw), BF16),
                   jax.ShapeDtypeStruct((B, n_heads, DK_B, DV_B), F32)],
        scratch_shapes=[pltpu.VMEM((n_heads, DV_B, DK_B), F32)],
        compiler_params=_cparams("arbitrary", "arbitrary"),
        name="hgrn2",
    )(*args)


def _topk_threshold(count_ge, count_gt, count_tie_below, write_ties, rows, ksel, idx_bits):
    def thr_step(i, thr):
        cand = thr + lax.shift_left(I32(1), 31 - i)
        return jnp.where(count_ge(cand) >= ksel, cand, thr)

    thr = lax.fori_loop(0, 32, thr_step, jnp.full((rows, V7X_LANES), INT_MIN, I32))
    no_thr = thr == INT_MIN
    write_ties(thr)
    excess = jnp.where(jnp.logical_or(no_thr, count_ge(thr) <= ksel), 0, 1)

    def resolve_ties():
        need = jnp.where(no_thr, 0, ksel - count_gt(thr))

        def tie_step(i, end):
            cand = end + lax.shift_left(I32(1), idx_bits - 1 - i)
            return jnp.where(count_tie_below(cand) <= need, cand, end)

        return lax.fori_loop(0, idx_bits, tie_step, jnp.zeros((rows, V7X_LANES), I32))

    tie_end = lax.cond(jnp.max(excess) > 0, resolve_ties,
                       lambda: jnp.where(no_thr, 0, TIE_NONE).astype(I32))
    return thr, tie_end


def _lane_total(cnt):
    return jnp.broadcast_to(jnp.sum(cnt, axis=1, keepdims=True), cnt.shape)


I16_MIN = -(2**15)


def _dsa_prompt_t_body(q_ref, qi_ref, wi_ref, k_ref, vt_ref, ki2_ref, bias_ref, o_ref,
                       key_scr, hi_scr, lo_scr, tix_scr, neg_scr, qiz_scr, qz_scr,
                       m_scr, l_scr, acc_scr, s_scr, mx_scr, p_scr, *, tq, ksel, idx_bits):
    L = V7X_LANES
    n_heads = qz_scr.shape[0]
    qt = pl.program_id(1)
    nkb = qt + 1
    shift = CHUNK.bit_length() - 1
    row_i = lax.broadcasted_iota(I32, (tq, tq), 0)
    col_i = lax.broadcasted_iota(I32, (tq, tq), 1)
    t_chunk = (qt * tq + col_i) >> shift

    lane = lax.broadcasted_iota(I32, (tq, L), 1)
    qi = qi_ref[0].astype(F32)
    for h in range(H_I):
        keep = (lane < D_I) if h % 2 == 0 else (lane >= D_I)
        qiz_scr[h] = jnp.where(keep, qi[:, (h // 2) * L:(h // 2 + 1) * L], 0.0).astype(BF16)
    w_t = (wi_ref[0] * (H_I ** -0.5 * D_I ** -0.5)).T

    def index_block(kb, carry):
        ks = pl.multiple_of(kb * tq, tq)
        ki2 = ki2_ref[0, pl.ds(ks, tq), :]
        score = None
        for h in range(H_I):
            term = w_t[h:h + 1, :] * jnp.maximum(_dot_nt(ki2, qiz_scr[h]), 0.0)
            score = term if score is None else score + term
        key = _sortable(score)
        s_chunk = (ks + row_i) >> shift
        key = jnp.where(s_chunk <= t_chunk, key, INT_MIN)
        key_scr[kb] = key
        hi_scr[kb] = (key >> 16).astype(jnp.int16)
        lo_scr[kb] = ((key & 0xFFFF) + I16_MIN).astype(jnp.int16)
        return carry

    lax.fori_loop(0, nkb, index_block, 0)

    def count(scr, pred, rows, dtype):
        n_acc = 4
        one, nil = jnp.ones((), dtype), jnp.zeros((), dtype)

        def body(kb, accs):
            accs = list(accs)
            for r in range(tq // rows):
                hit = pred(scr[kb, r * rows:(r + 1) * rows, :])
                accs[r % n_acc] = accs[r % n_acc] + jnp.where(hit, one, nil)
            return tuple(accs)
        accs = lax.fori_loop(0, nkb, body, tuple(jnp.zeros((rows, tq), dtype) for _ in range(n_acc)))
        tot = functools.reduce(lambda a, b: a + b, [a.astype(I32) for a in accs])
        return jnp.sum(tot, axis=0, keepdims=True)

    def count16(scr, pred):
        return count(scr, pred, 16, jnp.int16)

    def count32(scr, pred):
        return count(scr, pred, 8, I32)

    def search16(scr, target):
        def step(i, thr):
            cand = thr + lax.shift_left(I32(1), 15 - i)
            cand16 = jnp.broadcast_to(cand.astype(jnp.int16), (16, tq))
            return jnp.where(count16(scr, lambda x: x >= cand16) >= target, cand, thr)
        return lax.fori_loop(0, 16, step, jnp.full((1, tq), I16_MIN, I32))

    t_hi = search16(hi_scr, ksel)
    t_hi16 = jnp.broadcast_to(t_hi.astype(jnp.int16), (16, tq))
    above = count16(hi_scr, lambda x: x > t_hi16)

    def keep_bucket(kb, carry):
        for r in range(tq // 16):
            rows = slice(r * 16, (r + 1) * 16)
            lo_scr[kb, rows, :] = jnp.where(hi_scr[kb, rows, :] == t_hi16, lo_scr[kb, rows, :],
                                            jnp.int16(I16_MIN))
        return carry

    lax.fori_loop(0, nkb, keep_bucket, 0)
    t_lo = search16(lo_scr, ksel - above)
    thr = lax.shift_left(t_hi, 16) + (t_lo - I16_MIN)
    thr8 = jnp.broadcast_to(thr, (8, tq))
    n_ge = count32(key_scr, lambda x: x >= thr8)
    no_thr = thr == INT_MIN
    excess = jnp.where(jnp.logical_or(no_thr, n_ge <= ksel), 0, 1)

    def resolve_ties():
        need = jnp.where(no_thr, 0, ksel - count32(key_scr, lambda x: x > thr8))

        def write(kb, carry):
            tix_scr[kb] = jnp.where(key_scr[kb] == jnp.broadcast_to(thr, (tq, tq)),
                                    kb * tq + row_i, TIE_NONE)
            return carry

        lax.fori_loop(0, nkb, write, 0)

        def step(i, end):
            cand = end + lax.shift_left(I32(1), idx_bits - 1 - i)
            cand8 = jnp.broadcast_to(cand, (8, tq))
            return jnp.where(count32(tix_scr, lambda x: x < cand8) <= need, cand, end)

        return lax.fori_loop(0, idx_bits, step, jnp.zeros((1, tq), I32))

    tie_end = lax.cond(jnp.max(excess) > 0, resolve_ties,
                       lambda: jnp.where(no_thr, 0, TIE_NONE).astype(I32))

    def write_mask(kb, carry):
        key = key_scr[kb]
        thr_b = jnp.broadcast_to(thr, (tq, tq))
        tied = jnp.where(key == thr_b, kb * tq + row_i, TIE_NONE) < jnp.broadcast_to(tie_end, (tq, tq))
        neg_scr[kb] = jnp.where(key > thr_b, 0.0, jnp.where(tied, 0.0, -jnp.inf)).astype(BF16)
        return carry

    lax.fori_loop(0, nkb, write_mask, 0)

    q = q_ref[0].astype(F32)
    for h in range(n_heads):
        qp = q[:, (h // 2) * L:(h // 2 + 1) * L]
        keep = (lane < DH) if h % 2 == 0 else (lane >= DH)
        qz_scr[h] = jnp.where(keep, qp, 0.0).astype(BF16)
        m_scr[h] = jnp.full((1, tq), -1e30, F32)
        l_scr[h] = jnp.zeros((1, tq), F32)
        acc_scr[h] = jnp.zeros((DH, tq), F32)

    rc = 32
    n_rc = tq // rc
    ones = jnp.ones((BF16_ROWS, tq), BF16)

    def scores(kb, h, half):
        ks = pl.multiple_of(kb * tq, tq)
        kp = k_ref[0, pl.ds(ks, tq), (h // 2) * L:(h // 2 + 1) * L]
        x = _dot_nt(kp, qz_scr[h]).astype(BF16) + bias_ref[qt - kb, h] + neg_scr[kb]
        s_scr[half * n_heads + h] = x
        mx_scr[half * n_heads + h] = jnp.max(x.reshape(tq // BF16_ROWS, BF16_ROWS, tq), axis=0)

    for h in range(n_heads):
        scores(0, h, 0)

    def attend_half(kb, cur):
        kb_next = jnp.minimum(kb + 1, nkb - 1)
        for h in range(n_heads):
            scores(kb_next, h, 1 - cur)
            m_old = m_scr[h]
            m_new = jnp.maximum(m_old, jnp.max(mx_scr[cur * n_heads + h].astype(F32),
                                               axis=0, keepdims=True))
            alpha = jnp.exp2(m_old - m_new)
            m_scr[h] = m_new
            m_b = jnp.broadcast_to(m_new.astype(BF16), (rc, tq))
            for r in range(n_rc):
                rows = slice(r * rc, (r + 1) * rc)
                p_scr[cur * n_heads + h, rows, :] = jnp.exp2(
                    s_scr[cur * n_heads + h, rows, :] - m_b)
            vt = jnp.concatenate([vt_ref[0, kb, h * DH:(h + 1) * DH, :], ones], axis=0)
            pv = _dot(vt, p_scr[cur * n_heads + h])
            l_scr[h] = alpha * l_scr[h] + pv[DH:DH + 1, :]
            acc_scr[h] = acc_scr[h] * alpha + pv[:DH, :]

    def attend_block(kb, carry):
        lax.cond(lax.rem(kb, 2) == 0, lambda: attend_half(kb, 0), lambda: attend_half(kb, 1))
        return carry

    lax.fori_loop(0, nkb, attend_block, 0)
    for j in range(n_heads // 2):
        o_t = jnp.concatenate([acc_scr[2 * j] / l_scr[2 * j],
                               acc_scr[2 * j + 1] / l_scr[2 * j + 1]], axis=0)
        o_ref[0, :, j * L:(j + 1) * L] = o_t.T.astype(BF16)


def _dsa_prompt_t(q, qi, wi, kb16, vt16, ki2, bias_t, ksel):
    B, T, aw = q.shape
    n_heads = aw // DH
    tq = bias_t.shape[-1]
    nq = T // tq
    assert T < 2**15 and vt16.shape == (B, nq, aw, tq)
    row = lambda n: pl.BlockSpec((1, tq, n), lambda b, t: (b, t, 0))
    seq = lambda n: pl.BlockSpec((1, T, n), lambda b, t: (b, 0, 0))
    return pl.pallas_call(
        functools.partial(_dsa_prompt_t_body, tq=tq, ksel=ksel, idx_bits=T.bit_length()),
        grid=(B, nq),
        in_specs=[row(aw), row(H_I * D_I), row(H_I), seq(aw),
                  pl.BlockSpec((1, nq, aw, tq), lambda b, t: (b, 0, 0, 0)), seq(2 * D_I),
                  pl.BlockSpec(bias_t.shape, lambda b, t: (0, 0, 0, 0),
                               pipeline_mode=pl.Buffered(1))],
        out_specs=row(aw),
        out_shape=jax.ShapeDtypeStruct((B, T, aw), BF16),
        scratch_shapes=[pltpu.VMEM((nq, tq, tq), I32), pltpu.VMEM((nq, tq, tq), jnp.int16),
                        pltpu.VMEM((nq, tq, tq), jnp.int16), pltpu.VMEM((nq, tq, tq), I32),
                        pltpu.VMEM((nq, tq, tq), BF16), pltpu.VMEM((H_I, tq, V7X_LANES), BF16),
                        pltpu.VMEM((n_heads, tq, V7X_LANES), BF16),
                        pltpu.VMEM((n_heads, 1, tq), F32), pltpu.VMEM((n_heads, 1, tq), F32),
                        pltpu.VMEM((n_heads, DH, tq), F32),
                        pltpu.VMEM((2 * n_heads, tq, tq), BF16),
                        pltpu.VMEM((2 * n_heads, BF16_ROWS, tq), BF16),
                        pltpu.VMEM((2 * n_heads, tq, tq), BF16)],
        compiler_params=_cparams("arbitrary", "arbitrary"),
        name="dsa_prompt",
    )(q, qi, wi, kb16, vt16, ki2, bias_t)


def _pack_heads(q, n_heads):
    T, W = q.shape
    head_of_lane = lax.broadcasted_iota(I32, (T, W), 1) >> (DH.bit_length() - 1)
    qf = q.astype(F32)
    return jnp.concatenate([jnp.where(head_of_lane == h, qf, 0.0) for h in range(n_heads)],
                           axis=0).astype(BF16)


def _unpack_heads(acc, den, n_heads):
    T = acc.shape[0] // n_heads
    W = acc.shape[1]
    head_of_lane = lax.broadcasted_iota(I32, (T, W), 1) >> (DH.bit_length() - 1)
    out = jnp.zeros((T, W), F32)
    for h in range(n_heads):
        blk = acc[h * T:(h + 1) * T, :] / den[h * T:(h + 1) * T, :]
        out = jnp.where(head_of_lane == h, blk, out)
    return out


def _dsa_sample_body(q_ref, qi_ref, wi_ref, kn_ref, vn_ref, kin_ref, kc_ref, vc_ref, kic_ref,
                     bias_ref, biasn_ref, o_ref,
                     key_scr, tix_scr, keyn_scr, tixn_scr, sel_scr, qbd_scr, m_scr, l_scr, acc_scr,
                     *, ts, kc_len, n_new, ksel, idx_bits):
    L = V7X_LANES
    n_heads = qbd_scr.shape[0] // ts
    nc = key_scr.shape[0]
    step = pl.program_id(1)
    past = nc * kc_len
    lane = lax.broadcasted_iota(I32, (ts, L), 1)

    @pl.when(step == 0)
    def _():
        qi = qi_ref[0].astype(F32)
        qi_hm = jnp.concatenate([qi[:, h * D_I:(h + 1) * D_I] for h in range(H_I)],
                                axis=0).astype(BF16)
        w = wi_ref[0] * (H_I ** -0.5 * D_I ** -0.5)

        def scores(ki):
            s = _dot_nt(qi_hm, ki.astype(BF16))
            acc = None
            for h in range(H_I):
                term = w[:, h:h + 1] * jnp.maximum(s[h * ts:(h + 1) * ts, :], 0.0)
                acc = term if acc is None else acc + term
            return _sortable(acc)

        for c in range(nc):
            key_scr[c] = scores(kic_ref[0, c * kc_len:(c + 1) * kc_len, :])
        keyn_scr[...] = jnp.where(lane < n_new, scores(kin_ref[0]), INT_MIN)

        def count(pred_key=None, pred_tie=None):
            cnt = jnp.zeros((ts, L), I32)
            for c in range(nc):
                for i in range(kc_len // L):
                    if pred_key is not None:
                        hit = pred_key(key_scr[c, :, i * L:(i + 1) * L])
                    else:
                        hit = pred_tie(tix_scr[c, :, i * L:(i + 1) * L])
                    cnt = cnt + jnp.where(hit, 1, 0)
            hit = pred_key(keyn_scr[...]) if pred_key is not None else pred_tie(tixn_scr[...])
            return _lane_total(cnt + jnp.where(hit, 1, 0))

        def write_ties(thr):
            for c in range(nc):
                for i in range(kc_len // L):
                    idx = c * kc_len + i * L + lane
                    tix_scr[c, :, i * L:(i + 1) * L] = jnp.where(
                        key_scr[c, :, i * L:(i + 1) * L] == thr, idx, TIE_NONE)
            tixn_scr[...] = jnp.where(keyn_scr[...] == thr, past + lane, TIE_NONE)

        thr, tie_end = _topk_threshold(
            lambda x: count(pred_key=lambda k: k >= x),
            lambda x: count(pred_key=lambda k: k > x),
            lambda x: count(pred_tie=lambda t: t < x),
            write_ties, ts, ksel, idx_bits)
        sel_scr[0] = thr
        sel_scr[1] = tie_end
        qbd_scr[...] = _pack_heads(q_ref[0], n_heads)
        m_scr[...] = jnp.full(m_scr.shape, -1e30, F32)
        l_scr[...] = jnp.zeros(l_scr.shape, F32)
        acc_scr[...] = jnp.zeros(acc_scr.shape, F32)

    thr = sel_scr[0]
    tie_end = sel_scr[1]

    def attend(keys, ties, k16, v16, bias):
        n = keys.shape[1]
        negs = []
        for i in range(n // L):
            sel = jnp.logical_or(keys[:, i * L:(i + 1) * L] > thr,
                                 ties[:, i * L:(i + 1) * L] < tie_end)
            negs.append(jnp.where(sel, 0.0, -jnp.inf))
        neg = jnp.concatenate(negs, axis=1) if len(negs) > 1 else negs[0]
        neg = jnp.concatenate([neg] * n_heads, axis=0)
        s = _dot_nt(qbd_scr[...], k16) + bias + neg
        m_old = m_scr[...]
        m_new = jnp.maximum(m_old, jnp.max(s, axis=1, keepdims=True))
        alpha = jnp.exp(m_old - m_new)
        p = jnp.exp(s - m_new)
        l_scr[...] = alpha * l_scr[...] + jnp.sum(p, axis=1, keepdims=True)
        m_scr[...] = m_new
        acc_scr[...] = acc_scr[...] * alpha + _dot(p.astype(BF16), v16)

    attend(key_scr[step], tix_scr[step], kc_ref[0].astype(BF16), vc_ref[0].astype(BF16),
           bias_ref[...])

    @pl.when(step == nc - 1)
    def _():
        attend(keyn_scr[...], tixn_scr[...], kn_ref[0], vn_ref[0], biasn_ref[...])
        o_ref[0] = _unpack_heads(acc_scr[...], l_scr[...], n_heads).astype(BF16)


def _dsa_sample(q, qi, wi, kn16, vn16, kin, cache_k, cache_v, cache_ki, bias, n_new, ksel):
    B, ts, aw = q.shape
    n_heads = aw // DH
    P = cache_k.shape[1]
    npad = kn16.shape[1]
    kc_len = min(P, 1024)
    nc = P // kc_len
    L = V7X_LANES
    per_b = lambda a: pl.BlockSpec((1,) + a.shape[1:], lambda b, c: (b, 0, 0))
    chunk = lambda n: pl.BlockSpec((1, kc_len, n), lambda b, c: (b, c, 0))
    rows = n_heads * ts
    return pl.pallas_call(
        functools.partial(_dsa_sample_body, ts=ts, kc_len=kc_len, n_new=n_new, ksel=ksel,
                          idx_bits=(P + npad).bit_length()),
        grid=(B, nc),
        in_specs=[per_b(q), per_b(qi), per_b(wi), per_b(kn16), per_b(vn16), per_b(kin),
                  chunk(aw), chunk(aw), per_b(cache_ki),
                  pl.BlockSpec((rows, kc_len), lambda b, c: (0, c)),
                  pl.BlockSpec((rows, npad), lambda b, c: (0, P // npad))],
        out_specs=pl.BlockSpec((1, ts, aw), lambda b, c: (b, 0, 0)),
        out_shape=jax.ShapeDtypeStruct((B, ts, aw), BF16),
        scratch_shapes=[pltpu.VMEM((nc, ts, kc_len), I32), pltpu.VMEM((nc, ts, kc_len), I32),
                        pltpu.VMEM((ts, npad), I32), pltpu.VMEM((ts, npad), I32),
                        pltpu.VMEM((2, ts, L), I32), pltpu.VMEM((rows, aw), BF16),
                        pltpu.VMEM((rows, 1), F32), pltpu.VMEM((rows, 1), F32),
                        pltpu.VMEM((rows, aw), F32)],
        compiler_params=_cparams("arbitrary", "arbitrary"),
        name="dsa_sample",
    )(q, qi, wi, kn16, vn16, kin, cache_k, cache_v, cache_ki, bias, bias)


def _band_prompt_body(q_ref, k_ref, vt_ref, bias_ref, o_ref, qz_scr, s_scr, p_scr, ot_scr,
                      *, tq, n_back):
    L = V7X_LANES
    n_heads = q_ref.shape[-1] // DH
    n_slots = s_scr.shape[0]
    n_p = p_scr.shape[0]
    nb = n_back + 1
    rc = 32
    n_rc = tq // rc
    qt = pl.program_id(1)
    lane = lax.broadcasted_iota(I32, (tq, L), 1)
    n_tiles = bias_ref.shape[0]
    blocks, starts, exists = [], [], []
    for j in range(nb):
        kb = qt - n_back + j
        blocks.append(jnp.maximum(kb, 0))
        starts.append(pl.multiple_of(jnp.maximum(kb, 0) * tq, tq))
        exists.append(kb >= 0)
    q = q_ref[0].astype(F32)
    for h in range(n_heads):
        qp = q[:, (h // 2) * L:(h // 2 + 1) * L]
        keep = (lane < DH) if h % 2 == 0 else (lane >= DH)
        qz_scr[h] = jnp.where(keep, qp, 0.0).astype(BF16)

    def scores(h):
        mx = None
        for j in range(nb):
            kp = k_ref[0, pl.ds(starts[j], tq), (h // 2) * L:(h // 2 + 1) * L]
            tile = jnp.where(exists[j], h * nb + j, n_tiles - 1)
            x = _dot_nt(kp, qz_scr[h]).astype(BF16) + bias_ref[tile]
            s_scr[h % n_slots, j] = x
            xm = jnp.max(x.reshape(tq // BF16_ROWS, BF16_ROWS, tq), axis=0)
            mx = xm if mx is None else jnp.maximum(mx, xm)
        return mx

    ones = jnp.ones((BF16_ROWS, tq), BF16)
    maxes = [scores(h) for h in range(n_slots - 1)]
    for h in range(n_heads):
        if h + n_slots - 1 < n_heads:
            maxes.append(scores(h + n_slots - 1))
        slot = h % n_slots
        m = jnp.max(maxes[h].astype(F32), axis=0, keepdims=True)
        m_b = jnp.broadcast_to(m.astype(BF16), (rc, tq))
        out = None
        for j in range(nb):
            for r in range(n_rc):
                rows = slice(r * rc, (r + 1) * rc)
                p_scr[h % n_p, j, rows, :] = jnp.exp2(s_scr[slot, j, rows, :] - m_b)
            vt = jnp.concatenate([vt_ref[0, blocks[j], h * DH:(h + 1) * DH, :], ones], axis=0)
            oj = _dot(vt, p_scr[h % n_p, j])
            out = oj if out is None else out + oj
        ot_scr[h * DH:(h + 1) * DH, :] = out[:DH, :] / out[DH:DH + 1, :]
    for pair in range(n_heads // 2):
        o_ref[0, :, pair * L:(pair + 1) * L] = ot_scr[pair * L:(pair + 1) * L, :].T.astype(BF16)


def _band_prompt(q, kb16, vt16, bias_t):
    B, T, cw = q.shape
    n_heads = cw // DH
    tq = bias_t.shape[-1]
    nq = T // tq
    n_back = (bias_t.shape[0] - 1) // n_heads - 1
    assert vt16.shape == (B, nq, cw, tq)
    row = pl.BlockSpec((1, tq, cw), lambda b, t: (b, t, 0))
    return pl.pallas_call(
        functools.partial(_band_prompt_body, tq=tq, n_back=n_back),
        grid=(B, nq),
        in_specs=[row, pl.BlockSpec((1, T, cw), lambda b, t: (b, 0, 0)),
                  pl.BlockSpec((1, nq, cw, tq), lambda b, t: (b, 0, 0, 0)),
                  pl.BlockSpec(bias_t.shape, lambda b, t: (0, 0, 0),
                               pipeline_mode=pl.Buffered(1))],
        out_specs=row,
        out_shape=jax.ShapeDtypeStruct((B, T, cw), BF16),
        scratch_shapes=[pltpu.VMEM((n_heads, tq, V7X_LANES), BF16),
                        pltpu.VMEM((6, n_back + 1, tq, tq), BF16),
                        pltpu.VMEM((6, n_back + 1, tq, tq), BF16),
                        pltpu.VMEM((cw, tq), F32)],
        compiler_params=_cparams("arbitrary", "arbitrary"),
        name="band_prompt",
    )(q, kb16, vt16, bias_t)


def _band_sample_body(q_ref, kn_ref, vn_ref, kc_ref, vc_ref, bias_ref, biasn_ref, o_ref):
    n_heads = q_ref.shape[-1] // DH
    qbd = _pack_heads(q_ref[0], n_heads)
    s_c = _dot_nt(qbd, kc_ref[0].astype(BF16)) + bias_ref[...]
    s_n = _dot_nt(qbd, kn_ref[0]) + biasn_ref[...]
    m = jnp.maximum(jnp.max(s_c, axis=1, keepdims=True), jnp.max(s_n, axis=1, keepdims=True))
    p_c = jnp.exp(s_c - m)
    p_n = jnp.exp(s_n - m)
    den = jnp.sum(p_c, axis=1, keepdims=True) + jnp.sum(p_n, axis=1, keepdims=True)
    acc = _dot(p_c.astype(BF16), vc_ref[0].astype(BF16)) + _dot(p_n.astype(BF16), vn_ref[0])
    o_ref[0] = _unpack_heads(acc, den, n_heads).astype(BF16)


def _band_sample(q, kn16, vn16, cache_k, cache_v, bias):
    B, ts, cw = q.shape
    n_heads = cw // DH
    cb = cache_k.shape[1]
    npad = kn16.shape[1]
    rows = n_heads * ts
    per_b = lambda a: pl.BlockSpec((1,) + a.shape[1:], lambda b: (b, 0, 0))
    return pl.pallas_call(
        _band_sample_body,
        grid=(B,),
        in_specs=[per_b(q), per_b(kn16), per_b(vn16), per_b(cache_k), per_b(cache_v),
                  pl.BlockSpec((rows, cb), lambda b: (0, 0)),
                  pl.BlockSpec((rows, npad), lambda b: (0, cb // npad))],
        out_specs=per_b(q),
        out_shape=jax.ShapeDtypeStruct((B, ts, cw), BF16),
        compiler_params=_cparams("arbitrary"),
        name="band_sample",
    )(q, kn16, vn16, cache_k, cache_v, bias, bias)


def _pad_rows(a, n):
    return jnp.pad(a, ((0, 0), (0, n - a.shape[1]), (0, 0)))


def kernel(x_prompt, x_sample, c_prompt, c_sample, cache_a_k, cache_a_v, cache_a_kidx, state_b,
           cache_c_k, cache_c_v, w_ada, b_ada, norm_mix, norm_ffn, norm_final, w_in_even,
           w_out_even, t5_table, lb_logits, norm_b_out, w_in_odd, w_out_odd, rel_table_c,
           w_ffn_gate, w_ffn_up, w_ffn_down):
    Bp, T, D = x_prompt.shape
    Bs, Ts, _ = x_sample.shape
    P = cache_a_k.shape[2]
    c_buf = cache_c_k.shape[2]
    depth = w_ada.shape[0]
    L = V7X_LANES
    n_heads_a = cache_a_k.shape[3]
    aw = n_heads_a * DH
    n_heads_c = cache_c_k.shape[3]
    bw = w_out_even.shape[1] - aw
    iw = H_I * D_I
    ksel_p = min(K_TOP, T // 4)
    ksel_s = min(K_TOP, (P + Ts) // 4)
    c_keep = min(C_BACK * CHUNK, T)
    tq = min(T, 256)
    npad = L
    assert Ts <= npad and P % npad == 0 and c_buf % npad == 0 and T % tq == 0

    mod = _adaln_all(jnp.concatenate([c_prompt, c_sample], axis=0), w_ada, b_ada)

    def mods(l):
        parts = jnp.split(mod[l], 6, axis=-1)
        return ([p[:Bp].reshape(Bp, 1, D) for p in parts],
                [p[Bp:].reshape(Bs, 1, D) for p in parts])

    nq = T // tq
    t5_prompt = _bias_tiles(
        t5_table, groups=nq * n_heads_a, rows=tq, cols=tq, idx_fn=_t5_bucket,
        group_fn=lambda g: (g % n_heads_a, -(g // n_heads_a) * tq), keys_on_rows=True,
    ).reshape(nq, n_heads_a, tq, tq)
    t5_sample = _bias_tiles(
        t5_table, groups=n_heads_a, rows=Ts, cols=P + npad, idx_fn=_t5_bucket,
        group_fn=lambda g: (g, -P),
    ).reshape(n_heads_a * Ts, P + npad)

    xp, xs = x_prompt, x_sample
    outs_p = {k: [] for k in ("ak", "av", "ai", "bs", "ck", "cv")}
    outs_s = {k: [] for k in ("ak", "av", "ai", "bs", "ck", "cv")}
    for l in range(depth):
        (sh1p, sc1p, g1p, sh2p, sc2p, g2p), (sh1s, sc1s, g1s, sh2s, sc2s, g2s) = mods(l)
        gm = norm_mix[l].reshape(1, D)
        gn = norm_ffn[l].reshape(1, D)
        wg = w_ffn_gate[l].astype(BF16)
        wu = w_ffn_up[l].astype(BF16)
        wd = w_ffn_down[l].astype(BF16)
        gf = norm_final.reshape(1, D) if l == depth - 1 else None
        if l % 2 == 0:
            e = l // 2
            w_in = w_in_even[e]
            n_a = 3 * aw + iw + D_I + H_I
            wa = jnp.pad(w_in[:, :n_a], ((0, 0), (0, -n_a % L))).astype(BF16)
            wb = w_in[:, n_a:].astype(BF16)
            w_out = w_out_even[e].astype(BF16)
            gb_norm = norm_b_out[e].reshape(1, bw)

            q, k, v, k16, vt16, qi, ki, ki2, wi, zb = _in_even(xp, sc1p, sh1p, gm, wa, wb, aw,
                                                              keys_on_rows=True)
            oa = _dsa_prompt_t(q, qi, wi, k16, vt16, ki2, t5_prompt, ksel_p)
            ob, st = _hgrn(zb, lb_logits, gb_norm, None, e)
            outs_p["ak"].append(k.reshape(Bp, T, n_heads_a, DH))
            outs_p["av"].append(v.reshape(Bp, T, n_heads_a, DH))
            outs_p["ai"].append(ki)
            outs_p["bs"].append(st)
            xp = _mix_ffn(xp, g1p, sc2p, sh2p, g2p, gn, [oa, ob], [w_out[:aw], w_out[aw:]],
                          wg, wu, wd, gf)

            q, k, v, k16, v16, qi, ki, _, wi, zb = _in_even(xs, sc1s, sh1s, gm, wa, wb, aw,
                                                            keys_on_rows=False)
            oa = _dsa_sample(q, qi, wi, _pad_rows(k16, npad), _pad_rows(v16, npad),
                             _pad_rows(ki, npad), cache_a_k[e].reshape(Bs, P, aw),
                             cache_a_v[e].reshape(Bs, P, aw), cache_a_kidx[e], t5_sample,
                             Ts, ksel_s)
            ob, st = _hgrn(zb, lb_logits, gb_norm, state_b[e], e)
            outs_s["ak"].append(k.reshape(Bs, Ts, n_heads_a, DH))
            outs_s["av"].append(v.reshape(Bs, Ts, n_heads_a, DH))
            outs_s["ai"].append(ki)
            outs_s["bs"].append(st)
            xs = _mix_ffn(xs, g1s, sc2s, sh2s, g2s, gn, [oa, ob], [w_out[:aw], w_out[aw:]],
                          wg, wu, wd, gf)
        else:
            o = l // 2
            cw = n_heads_c * DH
            w_in = w_in_odd[o].astype(BF16)
            w_out = w_out_odd[o].astype(BF16)
            n_back = -(-(C_BACK * CHUNK) // tq)
            shift = CHUNK.bit_length() - 1

            n_band = n_heads_c * (n_back + 1)

            def band_mask(g, t, s, n_back=n_back, n_band=n_band):
                j = g % (n_back + 1)
                kc = (s + (j - n_back) * tq) >> shift
                tc = t >> shift
                return (kc <= tc) & (kc >= tc - C_BACK) & (g < n_band)

            band_p = _bias_tiles(
                rel_table_c[o], groups=n_band + 1, rows=tq, cols=tq, idx_fn=_clip_index,
                group_fn=lambda g, n_back=n_back: (jnp.minimum(g // (n_back + 1), n_heads_c - 1),
                                                   (g % (n_back + 1) - n_back) * tq),
                mask_fn=band_mask, keys_on_rows=True)
            band_s = _bias_tiles(
                rel_table_c[o], groups=n_heads_c, rows=Ts, cols=c_buf + npad, idx_fn=_clip_index,
                group_fn=lambda g: (g, -c_buf),
                mask_fn=lambda g, t, s: s < c_buf + Ts,
            ).reshape(n_heads_c * Ts, c_buf + npad)

            q, k, v, k16, vt16 = _in_odd(xp, sc1p, sh1p, gm, w_in, keys_on_rows=True)
            op = _band_prompt(q, k16, vt16, band_p)
            outs_p["ck"].append(k[:, T - c_keep:].reshape(Bp, c_keep, n_heads_c, DH))
            outs_p["cv"].append(v[:, T - c_keep:].reshape(Bp, c_keep, n_heads_c, DH))
            xp = _mix_ffn(xp, g1p, sc2p, sh2p, g2p, gn, [op], [w_out], wg, wu, wd, gf)

            q, k, v, k16, v16 = _in_odd(xs, sc1s, sh1s, gm, w_in, keys_on_rows=False)
            osm = _band_sample(q, _pad_rows(k16, npad), _pad_rows(v16, npad),
                               cache_c_k[o].reshape(Bs, c_buf, cw),
                               cache_c_v[o].reshape(Bs, c_buf, cw), band_s)
            outs_s["ck"].append(k.reshape(Bs, Ts, n_heads_c, DH))
            outs_s["cv"].append(v.reshape(Bs, Ts, n_heads_c, DH))
            xs = _mix_ffn(xs, g1s, sc2s, sh2s, g2s, gn, [osm], [w_out], wg, wu, wd, gf)

    st = lambda xs_: jnp.stack(xs_)
    return (xp, xs, st(outs_p["ak"]), st(outs_p["av"]), st(outs_p["ai"]), st(outs_p["bs"]),
            st(outs_p["ck"]), st(outs_p["cv"]), st(outs_s["ak"]), st(outs_s["av"]),
            st(outs_s["ai"]), st(outs_s["bs"]), st(outs_s["ck"]), st(outs_s["cv"]))
```

```python
import functools
import math

import jax
import jax.numpy as jnp
from jax import lax
from jax.experimental import pallas as pl
from jax.experimental.pallas import tpu as pltpu

F32, BF16, I32 = jnp.float32, jnp.bfloat16, jnp.int32

CHUNK = 64
EPS = 1e-6
DH = 64
H_I = 8
D_I = 64
K_TOP = 256
N_BUCKETS = 32
T5_MAX_DIST = 1024
DK_B = 128
DV_B = 128
C_BACK = 8
REL_CLIP = 128

V7X_LANES = 128
BF16_ROWS = 16
V7X_VMEM_LIMIT = 56 * 2**20

INT_MIN = -(2**31)
TIE_NONE = 2**30
HIGHEST = lax.Precision.HIGHEST


def _cparams(*sem):
    return pltpu.CompilerParams(dimension_semantics=sem, vmem_limit_bytes=V7X_VMEM_LIMIT)


def _dot(a, b):
    return jnp.dot(a, b, preferred_element_type=F32)


def _dot_nt(a, b):
    return lax.dot_general(a, b, (((1,), (1,)), ((), ())), preferred_element_type=F32)


def _silu(x):
    return x * jax.nn.sigmoid(x)


def _norm_mod(x, g, scale, shift):
    ms = jnp.mean(x * x, axis=-1, keepdims=True)
    return (x * lax.rsqrt(ms + EPS) * g) * (1.0 + scale) + shift


def _sortable(x):
    b = pltpu.bitcast(x + 0.0, I32)
    return b ^ ((b >> 31) & I32(0x7FFFFFFF))


LOG2E = math.log2(math.e)


def _q_scale(base2):
    return DH ** -0.5 * (LOG2E if base2 else 1.0)


def _row_tiles(B, T, rows=256):
    tt = min(T, rows)
    bb = max(1, min(B, rows // tt))
    assert T % tt == 0 and B % bb == 0
    return bb, tt


def _ada_body(c_ref, w_ref, b_ref, o_ref):
    s = _silu(c_ref[...])
    o_ref[0] = jnp.dot(s, w_ref[0], precision=HIGHEST, preferred_element_type=F32) + b_ref[0]


def _adaln_all(c_all, w_ada, b_ada):
    depth, D, E = w_ada.shape
    R = c_all.shape[0]
    tn = 1024
    return pl.pallas_call(
        _ada_body,
        grid=(depth, E // tn),
        in_specs=[pl.BlockSpec((R, D), lambda l, n: (0, 0)),
                  pl.BlockSpec((1, D, tn), lambda l, n: (l, 0, n)),
                  pl.BlockSpec((1, 1, tn), lambda l, n: (l, 0, n))],
        out_specs=pl.BlockSpec((1, R, tn), lambda l, n: (l, 0, n)),
        out_shape=jax.ShapeDtypeStruct((depth, R, E), F32),
        compiler_params=_cparams("arbitrary", "arbitrary"),
        name="adaln",
    )(c_all, w_ada, b_ada.reshape(depth, 1, E))


def _t5_bucket(rel):
    nb = N_BUCKETS // 2
    max_exact = nb // 2
    base = jnp.where(rel > 0, nb, 0)
    n = jnp.abs(rel)
    large = max_exact + (jnp.log(jnp.maximum(n, 1).astype(F32) / max_exact)
                         / math.log(T5_MAX_DIST / max_exact) * (nb - max_exact)).astype(I32)
    large = jnp.minimum(large, nb - 1)
    return base + jnp.where(n < max_exact, n, large)


def _clip_index(rel):
    return jnp.clip(-rel, -REL_CLIP, REL_CLIP) + REL_CLIP


def _bias_body(tab_ref, o_ref, *, rows, cols, width, n_tab, idx_fn, group_fn, mask_fn,
               keys_on_rows):
    g = pl.program_id(0)
    head, off = group_fn(g)
    i = lax.broadcasted_iota(I32, (1, width), 1)
    diff = jnp.where(i < width - rows, i, i - width)
    idx = idx_fn(off - diff if keys_on_rows else off + diff)
    val = jnp.zeros((1, width), F32)
    for b in range(n_tab):
        val = jnp.where(idx == b, tab_ref[head, b], val)
    x = jnp.broadcast_to(val, (rows, width))
    y = pltpu.roll(x, 0, 1, stride=1, stride_axis=0)
    tile = y[:, :cols]
    if keys_on_rows:
        tile = tile * LOG2E
    if mask_fn is not None:
        t = lax.broadcasted_iota(I32, (rows, cols), 1 if keys_on_rows else 0)
        s = lax.broadcasted_iota(I32, (rows, cols), 0 if keys_on_rows else 1)
        tile = jnp.where(mask_fn(g, t, s), tile, -jnp.inf)
    o_ref[0] = tile.astype(o_ref.dtype)


def _bias_tiles(table, *, groups, rows, cols, idx_fn, group_fn, mask_fn=None,
                keys_on_rows=False):
    n_tab = table.shape[0]
    width = -(-(rows + cols) // V7X_LANES) * V7X_LANES
    body = functools.partial(_bias_body, rows=rows, cols=cols, width=width, n_tab=n_tab,
                             idx_fn=idx_fn, group_fn=group_fn, mask_fn=mask_fn,
                             keys_on_rows=keys_on_rows)
    return pl.pallas_call(
        body,
        grid=(groups,),
        in_specs=[pl.BlockSpec(memory_space=pltpu.SMEM)],
        out_specs=pl.BlockSpec((1, rows, cols), lambda g: (g, 0, 0)),
        out_shape=jax.ShapeDtypeStruct((groups, rows, cols), BF16 if keys_on_rows else F32),
        compiler_params=_cparams("arbitrary"),
        name="bias_tiles",
    )(table.T)


def _in_even_body(x_ref, sc_ref, sh_ref, g_ref, wa_ref, wb_ref,
                  q_ref, k_ref, v_ref, kb_ref, vb_ref, qi_ref, ki_ref, ki2_ref, wi_ref, zb_ref,
                  *, aw, keys_on_rows):
    bb, tt, D = x_ref.shape
    M = bb * tt
    h = _norm_mod(x_ref[...], g_ref[...].reshape(1, 1, D), sc_ref[...], sh_ref[...]).reshape(M, D)
    h = h.astype(BF16)
    za = _dot(h, wa_ref[...])
    q_ref[...] = (za[:, :aw] * _q_scale(keys_on_rows)).astype(BF16).reshape(bb, tt, aw)
    k = za[:, aw:2 * aw]
    v = za[:, 2 * aw:3 * aw]
    k_ref[...] = k.reshape(bb, tt, aw)
    v_ref[...] = v.reshape(bb, tt, aw)
    kb_ref[...] = k.astype(BF16).reshape(bb, tt, aw)
    if keys_on_rows:
        vb_ref[0, 0] = v.T.astype(BF16)
    else:
        vb_ref[...] = v.astype(BF16).reshape(bb, tt, aw)
    iw = H_I * D_I
    zi = za[:, 3 * aw:]
    ki = zi[:, iw:iw + D_I]
    qi_ref[...] = zi[:, :iw].astype(BF16).reshape(bb, tt, iw)
    ki_ref[...] = ki.reshape(bb, tt, D_I)
    ki2_ref[...] = jnp.concatenate([ki, ki], axis=1).astype(BF16).reshape(bb, tt, 2 * D_I)
    wi_ref[...] = zi[:, iw + D_I:iw + D_I + H_I].reshape(bb, tt, H_I)
    zb_ref[...] = _dot(h, wb_ref[...]).reshape(bb, tt, zb_ref.shape[-1])


def _in_even(x, scale, shift, g, wa, wb, aw, keys_on_rows):
    B, T, D = x.shape
    bw = wb.shape[1]
    iw = H_I * D_I
    bb, tt = _row_tiles(B, T)
    assert not keys_on_rows or bb == 1
    row = lambda n: pl.BlockSpec((bb, tt, n), lambda b, t: (b, t, 0))
    mod = pl.BlockSpec((bb, 1, D), lambda b, t: (b, 0, 0))
    full = lambda a: pl.BlockSpec(a.shape, lambda b, t: (0,) * a.ndim)
    outs = [(aw, BF16), (aw, F32), (aw, F32), (aw, BF16), (aw, BF16), (iw, BF16), (D_I, F32),
            (2 * D_I, BF16), (H_I, F32), (bw, F32)]
    out_specs = [row(n) for n, _ in outs]
    out_shape = [jax.ShapeDtypeStruct((B, T, n), dt) for n, dt in outs]
    if keys_on_rows:
        out_specs[4] = pl.BlockSpec((1, 1, aw, tt), lambda b, t: (b, t, 0, 0))
        out_shape[4] = jax.ShapeDtypeStruct((B, T // tt, aw, tt), BF16)
    return pl.pallas_call(
        functools.partial(_in_even_body, aw=aw, keys_on_rows=keys_on_rows),
        grid=(B // bb, T // tt),
        in_specs=[row(D), mod, mod, full(g), full(wa), full(wb)],
        out_specs=out_specs,
        out_shape=out_shape,
        compiler_params=_cparams("arbitrary", "arbitrary"),
        name="in_even",
    )(x, scale, shift, g, wa, wb)


def _in_odd_body(x_ref, sc_ref, sh_ref, g_ref, w_ref, q_ref, k_ref, v_ref, kb_ref, vb_ref,
                 *, keys_on_rows):
    bb, tt, D = x_ref.shape
    M = bb * tt
    cw = q_ref.shape[-1]
    h = _norm_mod(x_ref[...], g_ref[...].reshape(1, 1, D), sc_ref[...], sh_ref[...]).reshape(M, D)
    z = _dot(h.astype(BF16), w_ref[...])
    q_ref[...] = (z[:, :cw] * _q_scale(keys_on_rows)).astype(BF16).reshape(bb, tt, cw)
    k = z[:, cw:2 * cw]
    v = z[:, 2 * cw:]
    k_ref[...] = k.reshape(bb, tt, cw)
    v_ref[...] = v.reshape(bb, tt, cw)
    kb_ref[...] = k.astype(BF16).reshape(bb, tt, cw)
    if keys_on_rows:
        vb_ref[0, 0] = v.T.astype(BF16)
    else:
        vb_ref[...] = v.astype(BF16).reshape(bb, tt, cw)


def _in_odd(x, scale, shift, g, w, keys_on_rows):
    B, T, D = x.shape
    cw = w.shape[1] // 3
    bb, tt = _row_tiles(B, T)
    assert not keys_on_rows or bb == 1
    row = lambda n: pl.BlockSpec((bb, tt, n), lambda b, t: (b, t, 0))
    mod = pl.BlockSpec((bb, 1, D), lambda b, t: (b, 0, 0))
    full = lambda a: pl.BlockSpec(a.shape, lambda b, t: (0,) * a.ndim)
    dts = [BF16, F32, F32, BF16, BF16]
    out_specs = [row(cw) for _ in dts]
    out_shape = [jax.ShapeDtypeStruct((B, T, cw), dt) for dt in dts]
    if keys_on_rows:
        out_specs[4] = pl.BlockSpec((1, 1, cw, tt), lambda b, t: (b, t, 0, 0))
        out_shape[4] = jax.ShapeDtypeStruct((B, T // tt, cw, tt), BF16)
    return pl.pallas_call(
        functools.partial(_in_odd_body, keys_on_rows=keys_on_rows),
        grid=(B // bb, T // tt),
        in_specs=[row(D), mod, mod, full(g), full(w)],
        out_specs=out_specs,
        out_shape=out_shape,
        compiler_params=_cparams("arbitrary", "arbitrary"),
        name="in_odd",
    )(x, scale, shift, g, w)


def _mix_ffn_body(*refs, n_mix, final):
    x_ref, g1_ref, sc2_ref, sh2_ref, g2_ref, gn_ref = refs[:6]
    mix_refs = refs[6:6 + n_mix]
    wmix_refs = refs[6 + n_mix:6 + 2 * n_mix]
    wg_ref, wu_ref, wd_ref = refs[6 + 2 * n_mix:9 + 2 * n_mix]
    gf_ref = refs[9 + 2 * n_mix] if final else None
    o_ref = refs[-1]
    bb, tt, D = x_ref.shape
    M = bb * tt
    m = None
    for mr, wr in zip(mix_refs, wmix_refs):
        part = _dot(mr[...].reshape(M, mr.shape[-1]), wr[...])
        m = part if m is None else m + part
    x1 = x_ref[...] + g1_ref[...] * m.reshape(bb, tt, D)
    h2 = _norm_mod(x1, gn_ref[...].reshape(1, 1, D), sc2_ref[...], sh2_ref[...])
    h2 = h2.reshape(M, D).astype(BF16)
    a = _dot(h2, wg_ref[...])
    b = _dot(h2, wu_ref[...])
    u = (_silu(a) * b).astype(BF16)
    f = _dot(u, wd_ref[...])
    x2 = x1 + g2_ref[...] * f.reshape(bb, tt, D)
    if final:
        ms = jnp.mean(x2 * x2, axis=-1, keepdims=True)
        x2 = x2 * lax.rsqrt(ms + EPS) * gf_ref[...].reshape(1, 1, D)
    o_ref[...] = x2


def _mix_ffn(x, g1, sc2, sh2, g2, gn, mixes, wmixes, wg, wu, wd, gf=None):
    B, T, D = x.shape
    bb, tt = _row_tiles(B, T, rows=512)
    row = lambda n: pl.BlockSpec((bb, tt, n), lambda b, t: (b, t, 0))
    mod = pl.BlockSpec((bb, 1, D), lambda b, t: (b, 0, 0))
    full = lambda a: pl.BlockSpec(a.shape, lambda b, t: (0,) * a.ndim,
                                  pipeline_mode=pl.Buffered(1))
    final = gf is not None
    args = [x, g1, sc2, sh2, g2, gn, *mixes, *wmixes, wg, wu, wd] + ([gf] if final else [])
    specs = ([row(D), mod, mod, mod, mod, full(gn)] + [row(m.shape[-1]) for m in mixes]
             + [full(w) for w in wmixes] + [full(wg), full(wu), full(wd)]
             + ([full(gf)] if final else []))
    return pl.pallas_call(
        functools.partial(_mix_ffn_body, n_mix=len(mixes), final=final),
        grid=(B // bb, T // tt),
        in_specs=specs,
        out_specs=row(D),
        out_shape=jax.ShapeDtypeStruct((B, T, D), F32),
        compiler_params=_cparams("arbitrary", "arbitrary"),
        name="mix_ffn",
    )(*args)


def _hgrn_body(*refs, C, n_sub, e, has_s0):
    if has_s0:
        z_ref, lbl_ref, gn_ref, s0_ref, ob_ref, sout_ref, st_scr = refs
    else:
        z_ref, lbl_ref, gn_ref, ob_ref, sout_ref, st_scr = refs
    step = pl.program_id(1)
    n_heads = st_scr.shape[0]
    kw = n_heads * DK_B

    @pl.when(step == 0)
    def _():
        for h in range(n_heads):
            st_scr[h] = s0_ref[0, h].T if has_s0 else jnp.zeros((DV_B, DK_B), F32)

    ll = lbl_ref[...]
    ex = jnp.exp(ll - jnp.max(ll, axis=0, keepdims=True))
    sm = ex / jnp.sum(ex, axis=0, keepdims=True)
    lb = jnp.sum(sm[:e + 1], axis=0, keepdims=True)
    r = lax.broadcasted_iota(I32, (C, C), 0)
    c = lax.broadcasted_iota(I32, (C, C), 1)
    causal = r >= c
    tri = jnp.where(causal, 1.0, 0.0)
    gn = gn_ref[...]
    heads = [(slice(h * DK_B, (h + 1) * DK_B), slice(h * DV_B, (h + 1) * DV_B))
             for h in range(n_heads)]
    intra, q_sts, decs, updates, gates = [], [], [], [], []
    for sub in range(n_sub):
        z = z_ref[0, sub * C:(sub + 1) * C, :]
        qh = _silu(z[:, :kw])
        f = lb + (1.0 - lb) * jax.nn.sigmoid(z[:, kw:2 * kw])
        kh = 1.0 - f
        vh = z[:, 2 * kw:3 * kw]
        gates.append(_silu(z[:, 3 * kw:]))
        G = jnp.dot(tri, jnp.log(f), precision=HIGHEST, preferred_element_type=F32)
        g_end = G[C - 1:C, :]
        mid = 0.5 * g_end
        e_mid = jnp.exp(mid)
        q_mid = qh * jnp.exp(G - mid)
        k_mid = kh * jnp.exp(mid - G)
        q_in = q_mid.astype(BF16)
        k_in = k_mid.astype(BF16)
        k_st = (k_mid * e_mid).astype(BF16)
        q_sts.append((q_mid * e_mid).astype(BF16))
        decs.append(e_mid * e_mid)
        o_sub, u_sub = [], []
        for sk, sv in heads:
            att = jnp.where(causal, _dot_nt(q_in[:, sk], k_in[:, sk]), 0.0)
            v_h = vh[:, sv]
            o_sub.append(_dot(att.astype(BF16), v_h.astype(BF16)))
            u_sub.append(_dot(v_h.T.astype(BF16), k_st[:, sk]))
        intra.append(o_sub)
        updates.append(u_sub)
    states = [st_scr[h] for h in range(n_heads)]
    for sub in range(n_sub):
        outs = []
        for h, (sk, sv) in enumerate(heads):
            o = intra[sub][h] + _dot_nt(q_sts[sub][:, sk], states[h].astype(BF16))
            states[h] = states[h] * decs[sub][:, sk] + updates[sub][h]
            outs.append(o * lax.rsqrt(jnp.mean(o * o, axis=-1, keepdims=True) + EPS))
        ob = jnp.concatenate(outs, axis=1) * gn * gates[sub]
        ob_ref[0, sub * C:(sub + 1) * C, :] = ob.astype(BF16)
    for h in range(n_heads):
        st_scr[h] = states[h]

    @pl.when(step == pl.num_programs(1) - 1)
    def _():
        for h in range(n_heads):
            sout_ref[0, h] = st_scr[h].T


def _hgrn(zb, lb_logits, g_norm, s0, e):
    B, T, zw = zb.shape
    kw = zw // 4
    n_heads = kw // DK_B
    C = min(CHUNK, T)
    ct = min(T, 8 * C)
    has_s0 = s0 is not None
    full = lambda a: pl.BlockSpec(a.shape, lambda b, t: (0,) * a.ndim)
    state = pl.BlockSpec((1, n_heads, DK_B, DV_B), lambda b, t: (b, 0, 0, 0))
    args = [zb, lb_logits, g_norm] + ([s0] if has_s0 else [])
    specs = ([pl.BlockSpec((1, ct, zw), lambda b, t: (b, t, 0)), full(lb_logits), full(g_norm)]
             + ([state] if has_s0 else []))
    return pl.pallas_call(
        functools.partial(_hgrn_body, C=C, n_sub=ct // C, e=e, has_s0=has_s0),
        grid=(B, T // ct),
        in_specs=specs,
        out_specs=[pl.BlockSpec((1, ct, kw), lambda b, t: (b, t, 0)), state],
        out_shape=[jax.ShapeDtypeStruct((B, T, kw), BF16),
                   jax.ShapeDtypeStruct((B, n_heads, DK_B, DV_B), F32)],
        scratch_shapes=[pltpu.VMEM((n_heads, DV_B, DK_B), F32)],
        compiler_params=_cparams("arbitrary", "arbitrary"),
        name="hgrn2",
    )(*args)


def _topk_threshold(count_ge, count_gt, count_tie_below, write_ties, rows, ksel, idx_bits):
    def thr_step(i, thr):
        cand = thr + lax.shift_left(I32(1), 31 - i)
        return jnp.where(count_ge(cand) >= ksel, cand, thr)

    thr = lax.fori_loop(0, 32, thr_step, jnp.full((rows, V7X_LANES), INT_MIN, I32))
    no_thr = thr == INT_MIN
    write_ties(thr)
    excess = jnp.where(jnp.logical_or(no_thr, count_ge(thr) <= ksel), 0, 1)

    def resolve_ties():
        need = jnp.where(no_thr, 0, ksel - count_gt(thr))

        def tie_step(i, end):
            cand = end + lax.shift_left(I32(1), idx_bits - 1 - i)
            return jnp.where(count_tie_below(cand) <= need, cand, end)

        return lax.fori_loop(0, idx_bits, tie_step, jnp.zeros((rows, V7X_LANES), I32))

    tie_end = lax.cond(jnp.max(excess) > 0, resolve_ties,
                       lambda: jnp.where(no_thr, 0, TIE_NONE).astype(I32))
    return thr, tie_end


def _lane_total(cnt):
    return jnp.broadcast_to(jnp.sum(cnt, axis=1, keepdims=True), cnt.shape)


I16_MIN = -(2**15)


def _dsa_prompt_t_body(q_ref, qi_ref, wi_ref, k_ref, vt_ref, ki2_ref, bias_ref, o_ref,
                       key_scr, hi_scr, lo_scr, tix_scr, neg_scr, qiz_scr, qz_scr,
                       m_scr, l_scr, acc_scr, s_scr, mx_scr, p_scr, *, tq, ksel, idx_bits):
    L = V7X_LANES
    n_heads = qz_scr.shape[0]
    qt = pl.program_id(1)
    nkb = qt + 1
    shift = CHUNK.bit_length() - 1
    row_i = lax.broadcasted_iota(I32, (tq, tq), 0)
    col_i = lax.broadcasted_iota(I32, (tq, tq), 1)
    t_chunk = (qt * tq + col_i) >> shift

    lane = lax.broadcasted_iota(I32, (tq, L), 1)
    qi = qi_ref[0].astype(F32)
    for h in range(H_I):
        keep = (lane < D_I) if h % 2 == 0 else (lane >= D_I)
        qiz_scr[h] = jnp.where(keep, qi[:, (h // 2) * L:(h // 2 + 1) * L], 0.0).astype(BF16)
    w_t = (wi_ref[0] * (H_I ** -0.5 * D_I ** -0.5)).T

    def index_block(kb, carry):
        ks = pl.multiple_of(kb * tq, tq)
        ki2 = ki2_ref[0, pl.ds(ks, tq), :]
        score = None
        for h in range(H_I):
            term = w_t[h:h + 1, :] * jnp.maximum(_dot_nt(ki2, qiz_scr[h]), 0.0)
            score = term if score is None else score + term
        key = _sortable(score)
        s_chunk = (ks + row_i) >> shift
        key = jnp.where(s_chunk <= t_chunk, key, INT_MIN)
        key_scr[kb] = key
        hi_scr[kb] = (key >> 16).astype(jnp.int16)
        lo_scr[kb] = ((key & 0xFFFF) + I16_MIN).astype(jnp.int16)
        return carry

    lax.fori_loop(0, nkb, index_block, 0)

    def count(scr, pred, rows, dtype):
        n_acc = 4
        one, nil = jnp.ones((), dtype), jnp.zeros((), dtype)

        def body(kb, accs):
            accs = list(accs)
            for r in range(tq // rows):
                hit = pred(scr[kb, r * rows:(r + 1) * rows, :])
                accs[r % n_acc] = accs[r % n_acc] + jnp.where(hit, one, nil)
            return tuple(accs)
        accs = lax.fori_loop(0, nkb, body, tuple(jnp.zeros((rows, tq), dtype) for _ in range(n_acc)))
        tot = functools.reduce(lambda a, b: a + b, [a.astype(I32) for a in accs])
        return jnp.sum(tot, axis=0, keepdims=True)

    def count16(scr, pred):
        return count(scr, pred, 16, jnp.int16)

    def count32(scr, pred):
        return count(scr, pred, 8, I32)

    def search16(scr, target):
        def step(i, thr):
            cand = thr + lax.shift_left(I32(1), 15 - i)
            cand16 = jnp.broadcast_to(cand.astype(jnp.int16), (16, tq))
            return jnp.where(count16(scr, lambda x: x >= cand16) >= target, cand, thr)
        return lax.fori_loop(0, 16, step, jnp.full((1, tq), I16_MIN, I32))

    t_hi = search16(hi_scr, ksel)
    t_hi16 = jnp.broadcast_to(t_hi.astype(jnp.int16), (16, tq))
    above = count16(hi_scr, lambda x: x > t_hi16)

    def keep_bucket(kb, carry):
        for r in range(tq // 16):
            rows = slice(r * 16, (r + 1) * 16)
            lo_scr[kb, rows, :] = jnp.where(hi_scr[kb, rows, :] == t_hi16, lo_scr[kb, rows, :],
                                            jnp.int16(I16_MIN))
        return carry

    lax.fori_loop(0, nkb, keep_bucket, 0)
    t_lo = search16(lo_scr, ksel - above)
    thr = lax.shift_left(t_hi, 16) + (t_lo - I16_MIN)
    thr8 = jnp.broadcast_to(thr, (8, tq))
    n_ge = count32(key_scr, lambda x: x >= thr8)
    no_thr = thr == INT_MIN
    excess = jnp.where(jnp.logical_or(no_thr, n_ge <= ksel), 0, 1)

    def resolve_ties():
        need = jnp.where(no_thr, 0, ksel - count32(key_scr, lambda x: x > thr8))

        def write(kb, carry):
            tix_scr[kb] = jnp.where(key_scr[kb] == jnp.broadcast_to(thr, (tq, tq)),
                                    kb * tq + row_i, TIE_NONE)
            return carry

        lax.fori_loop(0, nkb, write, 0)

        def step(i, end):
            cand = end + lax.shift_left(I32(1), idx_bits - 1 - i)
            cand8 = jnp.broadcast_to(cand, (8, tq))
            return jnp.where(count32(tix_scr, lambda x: x < cand8) <= need, cand, end)

        return lax.fori_loop(0, idx_bits, step, jnp.zeros((1, tq), I32))

    tie_end = lax.cond(jnp.max(excess) > 0, resolve_ties,
                       lambda: jnp.where(no_thr, 0, TIE_NONE).astype(I32))

    def write_mask(kb, carry):
        key = key_scr[kb]
        thr_b = jnp.broadcast_to(thr, (tq, tq))
        tied = jnp.where(key == thr_b, kb * tq + row_i, TIE_NONE) < jnp.broadcast_to(tie_end, (tq, tq))
        neg_scr[kb] = jnp.where(key > thr_b, 0.0, jnp.where(tied, 0.0, -jnp.inf)).astype(BF16)
        return carry

    lax.fori_loop(0, nkb, write_mask, 0)

    q = q_ref[0].astype(F32)
    for h in range(n_heads):
        qp = q[:, (h // 2) * L:(h // 2 + 1) * L]
        keep = (lane < DH) if h % 2 == 0 else (lane >= DH)
        qz_scr[h] = jnp.where(keep, qp, 0.0).astype(BF16)
        m_scr[h] = jnp.full((1, tq), -1e30, F32)
        l_scr[h] = jnp.zeros((1, tq), F32)
        acc_scr[h] = jnp.zeros((DH, tq), F32)

    rc = 32
    n_rc = tq // rc
    ones = jnp.ones((BF16_ROWS, tq), BF16)

    def scores(kb, h, half):
        ks = pl.multiple_of(kb * tq, tq)
        kp = k_ref[0, pl.ds(ks, tq), (h // 2) * L:(h // 2 + 1) * L]
        x = _dot_nt(kp, qz_scr[h]).astype(BF16) + bias_ref[qt - kb, h] + neg_scr[kb]
        s_scr[half * n_heads + h] = x
        mx_scr[half * n_heads + h] = jnp.max(x.reshape(tq // BF16_ROWS, BF16_ROWS, tq), axis=0)

    for h in range(n_heads):
        scores(0, h, 0)

    def attend_half(kb, cur):
        kb_next = jnp.minimum(kb + 1, nkb - 1)
        for h in range(n_heads):
            scores(kb_next, h, 1 - cur)
            m_old = m_scr[h]
            m_new = jnp.maximum(m_old, jnp.max(mx_scr[cur * n_heads + h].astype(F32),
                                               axis=0, keepdims=True))
            alpha = jnp.exp2(m_old - m_new)
            m_scr[h] = m_new
            m_b = jnp.broadcast_to(m_new.astype(BF16), (rc, tq))
            for r in range(n_rc):
                rows = slice(r * rc, (r + 1) * rc)
                p_scr[cur * n_heads + h, rows, :] = jnp.exp2(
                    s_scr[cur * n_heads + h, rows, :] - m_b)
            vt = jnp.concatenate([vt_ref[0, kb, h * DH:(h + 1) * DH, :], ones], axis=0)
            pv = _dot(vt, p_scr[cur * n_heads + h])
            l_scr[h] = alpha * l_scr[h] + pv[DH:DH + 1, :]
            acc_scr[h] = acc_scr[h] * alpha + pv[:DH, :]

    def attend_block(kb, carry):
        lax.cond(lax.rem(kb, 2) == 0, lambda: attend_half(kb, 0), lambda: attend_half(kb, 1))
        return carry

    lax.fori_loop(0, nkb, attend_block, 0)
    for j in range(n_heads // 2):
        o_t = jnp.concatenate([acc_scr[2 * j] / l_scr[2 * j],
                               acc_scr[2 * j + 1] / l_scr[2 * j + 1]], axis=0)
        o_ref[0, :, j * L:(j + 1) * L] = o_t.T.astype(BF16)


def _dsa_prompt_t(q, qi, wi, kb16, vt16, ki2, bias_t, ksel):
    B, T, aw = q.shape
    n_heads = aw // DH
    tq = bias_t.shape[-1]
    nq = T // tq
    assert T < 2**15 and vt16.shape == (B, nq, aw, tq)
    row = lambda n: pl.BlockSpec((1, tq, n), lambda b, t: (b, t, 0))
    seq = lambda n: pl.BlockSpec((1, T, n), lambda b, t: (b, 0, 0))
    return pl.pallas_call(
        functools.partial(_dsa_prompt_t_body, tq=tq, ksel=ksel, idx_bits=T.bit_length()),
        grid=(B, nq),
        in_specs=[row(aw), row(H_I * D_I), row(H_I), seq(aw),
                  pl.BlockSpec((1, nq, aw, tq), lambda b, t: (b, 0, 0, 0)), seq(2 * D_I),
                  pl.BlockSpec(bias_t.shape, lambda b, t: (0, 0, 0, 0),
                               pipeline_mode=pl.Buffered(1))],
        out_specs=row(aw),
        out_shape=jax.ShapeDtypeStruct((B, T, aw), BF16),
        scratch_shapes=[pltpu.VMEM((nq, tq, tq), I32), pltpu.VMEM((nq, tq, tq), jnp.int16),
                        pltpu.VMEM((nq, tq, tq), jnp.int16), pltpu.VMEM((nq, tq, tq), I32),
                        pltpu.VMEM((nq, tq, tq), BF16), pltpu.VMEM((H_I, tq, V7X_LANES), BF16),
                        pltpu.VMEM((n_heads, tq, V7X_LANES), BF16),
                        pltpu.VMEM((n_heads, 1, tq), F32), pltpu.VMEM((n_heads, 1, tq), F32),
                        pltpu.VMEM((n_heads, DH, tq), F32),
                        pltpu.VMEM((2 * n_heads, tq, tq), BF16),
                        pltpu.VMEM((2 * n_heads, BF16_ROWS, tq), BF16),
                        pltpu.VMEM((2 * n_heads, tq, tq), BF16)],
        compiler_params=_cparams("arbitrary", "arbitrary"),
        name="dsa_prompt",
    )(q, qi, wi, kb16, vt16, ki2, bias_t)


def _pack_heads(q, n_heads):
    T, W = q.shape
    head_of_lane = lax.broadcasted_iota(I32, (T, W), 1) >> (DH.bit_length() - 1)
    qf = q.astype(F32)
    return jnp.concatenate([jnp.where(head_of_lane == h, qf, 0.0) for h in range(n_heads)],
                           axis=0).astype(BF16)


def _unpack_heads(acc, den, n_heads):
    T = acc.shape[0] // n_heads
    W = acc.shape[1]
    head_of_lane = lax.broadcasted_iota(I32, (T, W), 1) >> (DH.bit_length() - 1)
    out = jnp.zeros((T, W), F32)
    for h in range(n_heads):
        blk = acc[h * T:(h + 1) * T, :] / den[h * T:(h + 1) * T, :]
        out = jnp.where(head_of_lane == h, blk, out)
    return out


def _dsa_sample_body(q_ref, qi_ref, wi_ref, kn_ref, vn_ref, kin_ref, kc_ref, vc_ref, kic_ref,
                     bias_ref, biasn_ref, o_ref,
                     key_scr, tix_scr, keyn_scr, tixn_scr, sel_scr, qbd_scr, m_scr, l_scr, acc_scr,
                     *, ts, kc_len, n_new, ksel, idx_bits):
    L = V7X_LANES
    n_heads = qbd_scr.shape[0] // ts
    nc = key_scr.shape[0]
    step = pl.program_id(1)
    past = nc * kc_len
    lane = lax.broadcasted_iota(I32, (ts, L), 1)

    @pl.when(step == 0)
    def _():
        qi = qi_ref[0].astype(F32)
        qi_hm = jnp.concatenate([qi[:, h * D_I:(h + 1) * D_I] for h in range(H_I)],
                                axis=0).astype(BF16)
        w = wi_ref[0] * (H_I ** -0.5 * D_I ** -0.5)

        def scores(s):
            acc = None
            for h in range(H_I):
                term = w[:, h:h + 1] * jnp.maximum(s[h * ts:(h + 1) * ts, :], 0.0)
                acc = term if acc is None else acc + term
            return _sortable(acc)

        for c in range(nc):
            ki_t = kic_ref[0, :, c * kc_len:(c + 1) * kc_len].astype(BF16)
            key_scr[c] = scores(_dot(qi_hm, ki_t))
        keyn_scr[...] = jnp.where(lane < n_new,
                                  scores(_dot_nt(qi_hm, kin_ref[0].astype(BF16))), INT_MIN)

        def count(pred_key=None, pred_tie=None):
            cnt = jnp.zeros((ts, L), I32)
            for c in range(nc):
                for i in range(kc_len // L):
                    if pred_key is not None:
                        hit = pred_key(key_scr[c, :, i * L:(i + 1) * L])
                    else:
                        hit = pred_tie(tix_scr[c, :, i * L:(i + 1) * L])
                    cnt = cnt + jnp.where(hit, 1, 0)
            hit = pred_key(keyn_scr[...]) if pred_key is not None else pred_tie(tixn_scr[...])
            return _lane_total(cnt + jnp.where(hit, 1, 0))

        def write_ties(thr):
            for c in range(nc):
                for i in range(kc_len // L):
                    idx = c * kc_len + i * L + lane
                    tix_scr[c, :, i * L:(i + 1) * L] = jnp.where(
                        key_scr[c, :, i * L:(i + 1) * L] == thr, idx, TIE_NONE)
            tixn_scr[...] = jnp.where(keyn_scr[...] == thr, past + lane, TIE_NONE)

        thr, tie_end = _topk_threshold(
            lambda x: count(pred_key=lambda k: k >= x),
            lambda x: count(pred_key=lambda k: k > x),
            lambda x: count(pred_tie=lambda t: t < x),
            write_ties, ts, ksel, idx_bits)
        sel_scr[0] = thr
        sel_scr[1] = tie_end
        qbd_scr[...] = _pack_heads(q_ref[0], n_heads)
        m_scr[...] = jnp.full(m_scr.shape, -1e30, F32)
        l_scr[...] = jnp.zeros(l_scr.shape, F32)
        acc_scr[...] = jnp.zeros(acc_scr.shape, F32)

    thr = sel_scr[0]
    tie_end = sel_scr[1]

    def attend(keys, ties, k16, v16, bias, keys_on_lanes):
        n = keys.shape[1]
        negs = []
        for i in range(n // L):
            sel = jnp.logical_or(keys[:, i * L:(i + 1) * L] > thr,
                                 ties[:, i * L:(i + 1) * L] < tie_end)
            negs.append(jnp.where(sel, 0.0, -jnp.inf))
        neg = jnp.concatenate(negs, axis=1) if len(negs) > 1 else negs[0]
        neg = jnp.concatenate([neg] * n_heads, axis=0)
        qk = _dot(qbd_scr[...], k16) if keys_on_lanes else _dot_nt(qbd_scr[...], k16)
        s = qk + bias + neg
        m_old = m_scr[...]
        m_new = jnp.maximum(m_old, jnp.max(s, axis=1, keepdims=True))
        alpha = jnp.exp(m_old - m_new)
        p = jnp.exp(s - m_new)
        l_scr[...] = alpha * l_scr[...] + jnp.sum(p, axis=1, keepdims=True)
        m_scr[...] = m_new
        p16 = p.astype(BF16)
        pv = _dot_nt(p16, v16) if keys_on_lanes else _dot(p16, v16)
        acc_scr[...] = acc_scr[...] * alpha + pv

    attend(key_scr[step], tix_scr[step], kc_ref[0].astype(BF16), vc_ref[0].astype(BF16),
           bias_ref[...], True)

    @pl.when(step == nc - 1)
    def _():
        attend(keyn_scr[...], tixn_scr[...], kn_ref[0], vn_ref[0], biasn_ref[...], False)
        o_ref[0] = _unpack_heads(acc_scr[...], l_scr[...], n_heads).astype(BF16)


def _dsa_sample(q, qi, wi, kn16, vn16, kin, cache_k, cache_v, cache_ki, bias, n_new, ksel):
    B, ts, aw = q.shape
    n_heads = aw // DH
    P = cache_k.shape[2]
    npad = kn16.shape[1]
    kc_len = min(P, 1024)
    nc = P // kc_len
    L = V7X_LANES
    per_b = lambda a: pl.BlockSpec((1,) + a.shape[1:], lambda b, c: (b, 0, 0))
    chunk = lambda n: pl.BlockSpec((1, n, kc_len), lambda b, c: (b, 0, c))
    rows = n_heads * ts
    return pl.pallas_call(
        functools.partial(_dsa_sample_body, ts=ts, kc_len=kc_len, n_new=n_new, ksel=ksel,
                          idx_bits=(P + npad).bit_length()),
        grid=(B, nc),
        in_specs=[per_b(q), per_b(qi), per_b(wi), per_b(kn16), per_b(vn16), per_b(kin),
                  chunk(aw), chunk(aw), per_b(cache_ki),
                  pl.BlockSpec((rows, kc_len), lambda b, c: (0, c)),
                  pl.BlockSpec((rows, npad), lambda b, c: (0, P // npad))],
        out_specs=pl.BlockSpec((1, ts, aw), lambda b, c: (b, 0, 0)),
        out_shape=jax.ShapeDtypeStruct((B, ts, aw), BF16),
        scratch_shapes=[pltpu.VMEM((nc, ts, kc_len), I32), pltpu.VMEM((nc, ts, kc_len), I32),
                        pltpu.VMEM((ts, npad), I32), pltpu.VMEM((ts, npad), I32),
                        pltpu.VMEM((2, ts, L), I32), pltpu.VMEM((rows, aw), BF16),
                        pltpu.VMEM((rows, 1), F32), pltpu.VMEM((rows, 1), F32),
                        pltpu.VMEM((rows, aw), F32)],
        compiler_params=_cparams("arbitrary", "arbitrary"),
        name="dsa_sample",
    )(q, qi, wi, kn16, vn16, kin, cache_k, cache_v, cache_ki, bias, bias)


def _band_prompt_body(q_ref, k_ref, vt_ref, bias_ref, o_ref, qz_scr, s_scr, p_scr, ot_scr,
                      *, tq, n_back):
    L = V7X_LANES
    n_heads = q_ref.shape[-1] // DH
    n_slots = s_scr.shape[0]
    n_p = p_scr.shape[0]
    nb = n_back + 1
    rc = 32
    n_rc = tq // rc
    qt = pl.program_id(1)
    lane = lax.broadcasted_iota(I32, (tq, L), 1)
    n_tiles = bias_ref.shape[0]
    blocks, starts, exists = [], [], []
    for j in range(nb):
        kb = qt - n_back + j
        blocks.append(jnp.maximum(kb, 0))
        starts.append(pl.multiple_of(jnp.maximum(kb, 0) * tq, tq))
        exists.append(kb >= 0)
    q = q_ref[0].astype(F32)
    for h in range(n_heads):
        qp = q[:, (h // 2) * L:(h // 2 + 1) * L]
        keep = (lane < DH) if h % 2 == 0 else (lane >= DH)
        qz_scr[h] = jnp.where(keep, qp, 0.0).astype(BF16)

    def scores(h):
        mx = None
        for j in range(nb):
            kp = k_ref[0, pl.ds(starts[j], tq), (h // 2) * L:(h // 2 + 1) * L]
            tile = jnp.where(exists[j], h * nb + j, n_tiles - 1)
            x = _dot_nt(kp, qz_scr[h]).astype(BF16) + bias_ref[tile]
            s_scr[h % n_slots, j] = x
            xm = jnp.max(x.reshape(tq // BF16_ROWS, BF16_ROWS, tq), axis=0)
            mx = xm if mx is None else jnp.maximum(mx, xm)
        return mx

    ones = jnp.ones((BF16_ROWS, tq), BF16)
    maxes = [scores(h) for h in range(n_slots - 1)]
    for h in range(n_heads):
        if h + n_slots - 1 < n_heads:
            maxes.append(scores(h + n_slots - 1))
        slot = h % n_slots
        m = jnp.max(maxes[h].astype(F32), axis=0, keepdims=True)
        m_b = jnp.broadcast_to(m.astype(BF16), (rc, tq))
        out = None
        for j in range(nb):
            for r in range(n_rc):
                rows = slice(r * rc, (r + 1) * rc)
                p_scr[h % n_p, j, rows, :] = jnp.exp2(s_scr[slot, j, rows, :] - m_b)
            vt = jnp.concatenate([vt_ref[0, blocks[j], h * DH:(h + 1) * DH, :], ones], axis=0)
            oj = _dot(vt, p_scr[h % n_p, j])
            out = oj if out is None else out + oj
        ot_scr[h * DH:(h + 1) * DH, :] = out[:DH, :] / out[DH:DH + 1, :]
    for pair in range(n_heads // 2):
        o_ref[0, :, pair * L:(pair + 1) * L] = ot_scr[pair * L:(pair + 1) * L, :].T.astype(BF16)


def _band_prompt(q, kb16, vt16, bias_t):
    B, T, cw = q.shape
    n_heads = cw // DH
    tq = bias_t.shape[-1]
    nq = T // tq
    n_back = (bias_t.shape[0] - 1) // n_heads - 1
    assert vt16.shape == (B, nq, cw, tq)
    row = pl.BlockSpec((1, tq, cw), lambda b, t: (b, t, 0))
    return pl.pallas_call(
        functools.partial(_band_prompt_body, tq=tq, n_back=n_back),
        grid=(B, nq),
        in_specs=[row, pl.BlockSpec((1, T, cw), lambda b, t: (b, 0, 0)),
                  pl.BlockSpec((1, nq, cw, tq), lambda b, t: (b, 0, 0, 0)),
                  pl.BlockSpec(bias_t.shape, lambda b, t: (0, 0, 0),
                               pipeline_mode=pl.Buffered(1))],
        out_specs=row,
        out_shape=jax.ShapeDtypeStruct((B, T, cw), BF16),
        scratch_shapes=[pltpu.VMEM((n_heads, tq, V7X_LANES), BF16),
                        pltpu.VMEM((6, n_back + 1, tq, tq), BF16),
                        pltpu.VMEM((6, n_back + 1, tq, tq), BF16),
                        pltpu.VMEM((cw, tq), F32)],
        compiler_params=_cparams("arbitrary", "arbitrary"),
        name="band_prompt",
    )(q, kb16, vt16, bias_t)


def _band_sample_body(q_ref, kn_ref, vn_ref, kc_ref, vc_ref, bias_ref, biasn_ref, o_ref):
    n_heads = q_ref.shape[-1] // DH
    qbd = _pack_heads(q_ref[0], n_heads)
    s_c = _dot(qbd, kc_ref[0].astype(BF16)) + bias_ref[...]
    s_n = _dot_nt(qbd, kn_ref[0]) + biasn_ref[...]
    m = jnp.maximum(jnp.max(s_c, axis=1, keepdims=True), jnp.max(s_n, axis=1, keepdims=True))
    p_c = jnp.exp(s_c - m)
    p_n = jnp.exp(s_n - m)
    den = jnp.sum(p_c, axis=1, keepdims=True) + jnp.sum(p_n, axis=1, keepdims=True)
    acc = _dot_nt(p_c.astype(BF16), vc_ref[0].astype(BF16)) + _dot(p_n.astype(BF16), vn_ref[0])
    o_ref[0] = _unpack_heads(acc, den, n_heads).astype(BF16)


def _band_sample(q, kn16, vn16, cache_k, cache_v, bias):
    B, ts, cw = q.shape
    n_heads = cw // DH
    cb = cache_k.shape[2]
    npad = kn16.shape[1]
    rows = n_heads * ts
    per_b = lambda a: pl.BlockSpec((1,) + a.shape[1:], lambda b: (b, 0, 0))
    return pl.pallas_call(
        _band_sample_body,
        grid=(B,),
        in_specs=[per_b(q), per_b(kn16), per_b(vn16), per_b(cache_k), per_b(cache_v),
                  pl.BlockSpec((rows, cb), lambda b: (0, 0)),
                  pl.BlockSpec((rows, npad), lambda b: (0, cb // npad))],
        out_specs=per_b(q),
        out_shape=jax.ShapeDtypeStruct((B, ts, cw), BF16),
        compiler_params=_cparams("arbitrary"),
        name="band_sample",
    )(q, kn16, vn16, cache_k, cache_v, bias, bias)


def _keys_on_lanes(cache):
    B, P, H, dh = cache.shape
    return jnp.transpose(cache, (0, 2, 3, 1)).reshape(B, H * dh, P)


def _pad_rows(a, n):
    return jnp.pad(a, ((0, 0), (0, n - a.shape[1]), (0, 0)))


def kernel(x_prompt, x_sample, c_prompt, c_sample, cache_a_k, cache_a_v, cache_a_kidx, state_b,
           cache_c_k, cache_c_v, w_ada, b_ada, norm_mix, norm_ffn, norm_final, w_in_even,
           w_out_even, t5_table, lb_logits, norm_b_out, w_in_odd, w_out_odd, rel_table_c,
           w_ffn_gate, w_ffn_up, w_ffn_down):
    Bp, T, D = x_prompt.shape
    Bs, Ts, _ = x_sample.shape
    P = cache_a_k.shape[2]
    c_buf = cache_c_k.shape[2]
    depth = w_ada.shape[0]
    L = V7X_LANES
    n_heads_a = cache_a_k.shape[3]
    aw = n_heads_a * DH
    n_heads_c = cache_c_k.shape[3]
    bw = w_out_even.shape[1] - aw
    iw = H_I * D_I
    ksel_p = min(K_TOP, T // 4)
    ksel_s = min(K_TOP, (P + Ts) // 4)
    c_keep = min(C_BACK * CHUNK, T)
    tq = min(T, 256)
    npad = L
    assert Ts <= npad and P % npad == 0 and c_buf % npad == 0 and T % tq == 0

    mod = _adaln_all(jnp.concatenate([c_prompt, c_sample], axis=0), w_ada, b_ada)

    def mods(l):
        parts = jnp.split(mod[l], 6, axis=-1)
        return ([p[:Bp].reshape(Bp, 1, D) for p in parts],
                [p[Bp:].reshape(Bs, 1, D) for p in parts])

    nq = T // tq
    t5_prompt = _bias_tiles(
        t5_table, groups=nq * n_heads_a, rows=tq, cols=tq, idx_fn=_t5_bucket,
        group_fn=lambda g: (g % n_heads_a, -(g // n_heads_a) * tq), keys_on_rows=True,
    ).reshape(nq, n_heads_a, tq, tq)
    t5_sample = _bias_tiles(
        t5_table, groups=n_heads_a, rows=Ts, cols=P + npad, idx_fn=_t5_bucket,
        group_fn=lambda g: (g, -P),
    ).reshape(n_heads_a * Ts, P + npad)

    xp, xs = x_prompt, x_sample
    outs_p = {k: [] for k in ("ak", "av", "ai", "bs", "ck", "cv")}
    outs_s = {k: [] for k in ("ak", "av", "ai", "bs", "ck", "cv")}
    for l in range(depth):
        (sh1p, sc1p, g1p, sh2p, sc2p, g2p), (sh1s, sc1s, g1s, sh2s, sc2s, g2s) = mods(l)
        gm = norm_mix[l].reshape(1, D)
        gn = norm_ffn[l].reshape(1, D)
        wg = w_ffn_gate[l].astype(BF16)
        wu = w_ffn_up[l].astype(BF16)
        wd = w_ffn_down[l].astype(BF16)
        gf = norm_final.reshape(1, D) if l == depth - 1 else None
        if l % 2 == 0:
            e = l // 2
            w_in = w_in_even[e]
            n_a = 3 * aw + iw + D_I + H_I
            wa = jnp.pad(w_in[:, :n_a], ((0, 0), (0, -n_a % L))).astype(BF16)
            wb = w_in[:, n_a:].astype(BF16)
            w_out = w_out_even[e].astype(BF16)
            gb_norm = norm_b_out[e].reshape(1, bw)

            q, k, v, k16, vt16, qi, ki, ki2, wi, zb = _in_even(xp, sc1p, sh1p, gm, wa, wb, aw,
                                                              keys_on_rows=True)
            oa = _dsa_prompt_t(q, qi, wi, k16, vt16, ki2, t5_prompt, ksel_p)
            ob, st = _hgrn(zb, lb_logits, gb_norm, None, e)
            outs_p["ak"].append(k.reshape(Bp, T, n_heads_a, DH))
            outs_p["av"].append(v.reshape(Bp, T, n_heads_a, DH))
            outs_p["ai"].append(ki)
            outs_p["bs"].append(st)
            xp = _mix_ffn(xp, g1p, sc2p, sh2p, g2p, gn, [oa, ob], [w_out[:aw], w_out[aw:]],
                          wg, wu, wd, gf)

            q, k, v, k16, v16, qi, ki, _, wi, zb = _in_even(xs, sc1s, sh1s, gm, wa, wb, aw,
                                                            keys_on_rows=False)
            oa = _dsa_sample(q, qi, wi, _pad_rows(k16, npad), _pad_rows(v16, npad),
                             _pad_rows(ki, npad), _keys_on_lanes(cache_a_k[e]),
                             _keys_on_lanes(cache_a_v[e]), jnp.swapaxes(cache_a_kidx[e], 1, 2),
                             t5_sample,
                             Ts, ksel_s)
            ob, st = _hgrn(zb, lb_logits, gb_norm, state_b[e], e)
            outs_s["ak"].append(k.reshape(Bs, Ts, n_heads_a, DH))
            outs_s["av"].append(v.reshape(Bs, Ts, n_heads_a, DH))
            outs_s["ai"].append(ki)
            outs_s["bs"].append(st)
            xs = _mix_ffn(xs, g1s, sc2s, sh2s, g2s, gn, [oa, ob], [w_out[:aw], w_out[aw:]],
                          wg, wu, wd, gf)
        else:
            o = l // 2
            cw = n_heads_c * DH
            w_in = w_in_odd[o].astype(BF16)
            w_out = w_out_odd[o].astype(BF16)
            n_back = -(-(C_BACK * CHUNK) // tq)
            shift = CHUNK.bit_length() - 1

            n_band = n_heads_c * (n_back + 1)

            def band_mask(g, t, s, n_back=n_back, n_band=n_band):
                j = g % (n_back + 1)
                kc = (s + (j - n_back) * tq) >> shift
                tc = t >> shift
                return (kc <= tc) & (kc >= tc - C_BACK) & (g < n_band)

            band_p = _bias_tiles(
                rel_table_c[o], groups=n_band + 1, rows=tq, cols=tq, idx_fn=_clip_index,
                group_fn=lambda g, n_back=n_back: (jnp.minimum(g // (n_back + 1), n_heads_c - 1),
                                                   (g % (n_back + 1) - n_back) * tq),
                mask_fn=band_mask, keys_on_rows=True)
            band_s = _bias_tiles(
                rel_table_c[o], groups=n_heads_c, rows=Ts, cols=c_buf + npad, idx_fn=_clip_index,
                group_fn=lambda g: (g, -c_buf),
                mask_fn=lambda g, t, s: s < c_buf + Ts,
            ).reshape(n_heads_c * Ts, c_buf + npad)

            q, k, v, k16, vt16 = _in_odd(xp, sc1p, sh1p, gm, w_in, keys_on_rows=True)
            op = _band_prompt(q, k16, vt16, band_p)
            outs_p["ck"].append(k[:, T - c_keep:].reshape(Bp, c_keep, n_heads_c, DH))
            outs_p["cv"].append(v[:, T - c_keep:].reshape(Bp, c_keep, n_heads_c, DH))
            xp = _mix_ffn(xp, g1p, sc2p, sh2p, g2p, gn, [op], [w_out], wg, wu, wd, gf)

            q, k, v, k16, v16 = _in_odd(xs, sc1s, sh1s, gm, w_in, keys_on_rows=False)
            osm = _band_sample(q, _pad_rows(k16, npad), _pad_rows(v16, npad),
                               _keys_on_lanes(cache_c_k[o]), _keys_on_lanes(cache_c_v[o]),
                               band_s)
            outs_s["ck"].append(k.reshape(Bs, Ts, n_heads_c, DH))
            outs_s["cv"].append(v.reshape(Bs, Ts, n_heads_c, DH))
            xs = _mix_ffn(xs, g1s, sc2s, sh2s, g2s, gn, [osm], [w_out], wg, wu, wd, gf)

    st = lambda xs_: jnp.stack(xs_)
    return (xp, xs, st(outs_p["ak"]), st(outs_p["av"]), st(outs_p["ai"]), st(outs_p["bs"]),
            st(outs_p["ck"]), st(outs_p["cv"]), st(outs_s["ak"]), st(outs_s["av"]),
            st(outs_s["ai"]), st(outs_s["bs"]), st(outs_s["ck"]), st(outs_s["cv"]))
```
